```python
import jax, jax.numpy as jnp
from jax import lax
import numpy as np

D_MODEL = 1024
BATCH = 16
SEQ = 2048
DEPTH = 1
DEC_BATCH = 4
DEC_SEQ = 4096
PAST_LEN = 128

FNET_WIDTH = 512
FNET_GROUPS = 4
FNET_GROUP_DIM = FNET_WIDTH // FNET_GROUPS
SGU_WIDTH = 512
SGU_HEADS = 4
SGU_HEAD_DIM = SGU_WIDTH // SGU_HEADS
CHUNK = 128
EPS = 1e-6

IN_SIZES = (FNET_WIDTH, FNET_WIDTH, SGU_WIDTH, SGU_WIDTH, SGU_WIDTH, D_MODEL, D_MODEL)
IN_WIDTH = sum(IN_SIZES)
IN_SPLITS = tuple(int(v) for v in np.cumsum(IN_SIZES)[:-1])

kernel_name = "fnet_gmlp_gated_hybrid_encoder"


def rmsnorm(x, g):
    xf = x.astype(jnp.float32)
    y = xf * lax.rsqrt(jnp.mean(xf * xf, axis=-1, keepdims=True) + EPS)
    return (y * g.astype(jnp.float32)).astype(x.dtype)


def layernorm(x, g, b):
    xf = x.astype(jnp.float32)
    mu = jnp.mean(xf, axis=-1, keepdims=True)
    xc = xf - mu
    var = jnp.mean(xc * xc, axis=-1, keepdims=True)
    y = xc * lax.rsqrt(var + EPS) * g.astype(jnp.float32) + b.astype(jnp.float32)
    return y.astype(x.dtype)


def fourier_branch(a, w_fmix, b_fmix):
    B, S, _ = a.shape
    a4 = a.reshape(B, S, FNET_GROUPS, FNET_GROUP_DIM).astype(jnp.float32)
    f = jnp.fft.fft2(a4, axes=(1, 3), norm="ortho").real.astype(a.dtype)
    y = jnp.einsum("bsgc,gcd->bsgd", f, w_fmix) + b_fmix
    return y.reshape(B, S, FNET_WIDTH)


def sgu_branch(u, v, ln_g, ln_b, w_s, b_s):
    B, S, _ = v.shape
    v = layernorm(v, ln_g, ln_b)
    vc = v.reshape(B, S // CHUNK, CHUNK, SGU_HEADS, SGU_HEAD_DIM)
    mixed = jnp.einsum("hpq,bnqhc->bnphc", w_s, vc) + b_s.T[:, :, None]
    return u * mixed.reshape(B, S, SGU_WIDTH)


def layer(x, c, norm_g, w_ada, b_ada, w_in, w_fmix, b_fmix, sgu_ln_g, sgu_ln_b,
          w_s, b_s, w_pa, w_pb, w_out):
    mod = jax.nn.silu(c) @ w_ada + b_ada
    shift, scale, gate = jnp.split(mod, 3, axis=-1)
    h = rmsnorm(x, norm_g) * (1.0 + scale[:, None, :]) + shift[:, None, :]
    z = h @ w_in
    a, ga, u, v, gb, ma, mb = jnp.split(z, IN_SPLITS, axis=-1)
    y_a = fourier_branch(a, w_fmix, b_fmix) * jax.nn.silu(ga)
    y_b = sgu_branch(u, v, sgu_ln_g, sgu_ln_b, w_s, b_s) * jax.nn.silu(gb)
    merged = jax.nn.sigmoid(ma) * (y_a @ w_pa) + jax.nn.sigmoid(mb) * (y_b @ w_pb)
    out = merged @ w_out
    return x + gate[:, None, :] * out


def trunk(x, c, norm_g, w_ada, b_ada, w_in, w_fmix, b_fmix, sgu_ln_g, sgu_ln_b,
          w_s, b_s, w_pa, w_pb, w_out, final_g):
    for l in range(DEPTH):
        x = layer(x, c, norm_g[l], w_ada[l], b_ada[l], w_in[l], w_fmix[l], b_fmix[l],
                  sgu_ln_g[l], sgu_ln_b[l], w_s[l], b_s[l], w_pa[l], w_pb[l], w_out[l])
    return rmsnorm(x, final_g)


def setup_inputs(seed: int = 0) -> dict:
    key = jax.random.key(seed)
    ks = jax.random.split(key, 20)
    f32 = jnp.float32
    nrm = lambda k, shape, s: jax.random.normal(k, shape, f32) * s
    return {
        "x_prompt": nrm(ks[0], (BATCH, SEQ, D_MODEL), 1.0),
        "x_sample": nrm(ks[1], (DEC_BATCH, DEC_SEQ, D_MODEL), 1.0),
        "c_prompt": nrm(ks[2], (BATCH, D_MODEL), 1.0),
        "c_sample": nrm(ks[3], (DEC_BATCH, D_MODEL), 1.0),
        "norm_g": 1.0 + nrm(ks[4], (DEPTH, D_MODEL), 0.02),
        "w_ada": nrm(ks[5], (DEPTH, D_MODEL, 3 * D_MODEL), 0.3 * D_MODEL ** -0.5),
        "b_ada": nrm(ks[6], (DEPTH, 3 * D_MODEL), 0.02),
        "w_in": nrm(ks[7], (DEPTH, D_MODEL, IN_WIDTH), D_MODEL ** -0.5),
        "w_fmix": nrm(ks[8], (DEPTH, FNET_GROUPS, FNET_GROUP_DIM, FNET_GROUP_DIM), FNET_GROUP_DIM ** -0.5),
        "b_fmix": nrm(ks[9], (DEPTH, FNET_GROUPS, FNET_GROUP_DIM), 0.02),
        "sgu_ln_g": 1.0 + nrm(ks[10], (DEPTH, SGU_WIDTH), 0.02),
        "sgu_ln_b": nrm(ks[11], (DEPTH, SGU_WIDTH), 0.02),
        "w_s": nrm(ks[12], (DEPTH, SGU_HEADS, CHUNK, CHUNK), CHUNK ** -0.5),
        "b_s": 1.0 + nrm(ks[13], (DEPTH, SGU_HEADS, CHUNK), 0.02),
        "w_pa": nrm(ks[14], (DEPTH, FNET_WIDTH, D_MODEL), FNET_WIDTH ** -0.5),
        "w_pb": nrm(ks[15], (DEPTH, SGU_WIDTH, D_MODEL), SGU_WIDTH ** -0.5),
        "w_out": nrm(ks[16], (DEPTH, D_MODEL, D_MODEL), D_MODEL ** -0.5),
        "final_g": 1.0 + nrm(ks[17], (D_MODEL,), 0.02),
    }


def reference(x_prompt, x_sample, c_prompt, c_sample, norm_g, w_ada, b_ada, w_in, w_fmix, b_fmix,
              sgu_ln_g, sgu_ln_b, w_s, b_s, w_pa, w_pb, w_out, final_g):
    y_prompt = trunk(x_prompt, c_prompt, norm_g, w_ada, b_ada, w_in, w_fmix, b_fmix, sgu_ln_g,
                     sgu_ln_b, w_s, b_s, w_pa, w_pb, w_out, final_g)
    y_sample = trunk(x_sample, c_sample, norm_g, w_ada, b_ada, w_in, w_fmix, b_fmix, sgu_ln_g,
                     sgu_ln_b, w_s, b_s, w_pa, w_pb, w_out, final_g)
    return (y_prompt, y_sample)
```

```python
import functools
import math

import numpy as np
import jax
import jax.numpy as jnp
from jax import lax
from jax.experimental import pallas as pl
from jax.experimental.pallas import tpu as pltpu

D_MODEL = 1024
FNET_WIDTH = 512
FNET_GROUPS = 4
FNET_GROUP_DIM = 128
SGU_WIDTH = 512
SGU_HEADS = 4
SGU_HEAD_DIM = 128
CHUNK = 128
EPS = 1e-6

_GA = (0, 512)
_U = (512, 1024)
_V = (1024, 1536)
_GB = (1536, 2048)
_MA = (2048, 3072)
_MB = (3072, 4096)
REST_WIDTH = 4096

BF16 = jnp.bfloat16
F32 = jnp.float32

VMEM_LIMIT_BYTES = 56 * 1024 * 1024


def _dot(a, b):
    return jnp.dot(a, b, preferred_element_type=F32)


def _silu(x):
    return x * jax.nn.sigmoid(x)


def _mod_kernel(c_ref, w_ref, b_ref, o_ref):
    c = c_ref[...]
    o_ref[...] = jnp.dot(_silu(c), w_ref[...], preferred_element_type=F32,
                         precision=lax.Precision.HIGHEST) + b_ref[...]


def _modulation(c, w_ada, b_ada):
    n = c.shape[0]
    return pl.pallas_call(
        _mod_kernel,
        out_shape=jax.ShapeDtypeStruct((n, 3 * D_MODEL), F32),
        compiler_params=pltpu.CompilerParams(vmem_limit_bytes=VMEM_LIMIT_BYTES),
        name="adaln_mod",
    )(c, w_ada, b_ada.reshape(1, -1))


def _modulated_norm(x, mod_ref, g_ref):
    ms = jnp.mean(x * x, axis=-1, keepdims=True)
    xn = x * lax.rsqrt(ms + EPS) * g_ref[...]
    shift = mod_ref[0:1, :]
    scale = mod_ref[1:2, :]
    return xn * (1.0 + scale) + shift


def _pre_kernel(x_ref, mod_ref, g_ref, wa_ref, cs_ref, p_ref):
    h = _modulated_norm(x_ref[...], mod_ref, g_ref).astype(BF16)
    a = _dot(h, wa_ref[...]).astype(BF16)
    cs = cs_ref[...]
    for g in range(FNET_GROUPS):
        lo, hi = g * FNET_GROUP_DIM, (g + 1) * FNET_GROUP_DIM
        pg = _dot(a[:, lo:hi], cs)
        p_ref[0, :, lo:hi] = pg[:, :FNET_GROUP_DIM].astype(BF16)
        p_ref[1, :, lo:hi] = pg[:, FNET_GROUP_DIM:].astype(BF16)


def _pre_call(x, mod, norm_g, w_a, cs_c, tm):
    B, S, _ = x.shape
    return pl.pallas_call(
        _pre_kernel,
        grid=(B, S // tm),
        in_specs=[
            pl.BlockSpec((None, tm, D_MODEL), lambda b, i: (b, i, 0)),
            pl.BlockSpec((None, 3, D_MODEL), lambda b, i: (b, 0, 0)),
            pl.BlockSpec((1, D_MODEL), lambda b, i: (0, 0)),
            pl.BlockSpec((D_MODEL, FNET_WIDTH), lambda b, i: (0, 0)),
            pl.BlockSpec((FNET_GROUP_DIM, 2 * FNET_GROUP_DIM), lambda b, i: (0, 0)),
        ],
        out_specs=pl.BlockSpec((None, 2, tm, FNET_WIDTH), lambda b, i: (b, 0, i, 0)),
        out_shape=jax.ShapeDtypeStruct((B, 2, S, FNET_WIDTH), BF16),
        compiler_params=pltpu.CompilerParams(
            dimension_semantics=("arbitrary", "arbitrary"),
            vmem_limit_bytes=VMEM_LIMIT_BYTES),
        name="fnet_pre",
    )(x, mod, norm_g, w_a, cs_c)


def _fourier_kernel(d_ref, p_ref, wf_ref, bf_ref, o_ref):
    f = _dot(d_ref[...], p_ref[...]).astype(BF16)
    for g in range(FNET_GROUPS):
        lo, hi = g * FNET_GROUP_DIM, (g + 1) * FNET_GROUP_DIM
        o_ref[:, lo:hi] = _dot(f[:, lo:hi], wf_ref[g]) + bf_ref[:, lo:hi]


def _fourier_call(dmat, p, w_fmix, b_fmix, tk):
    B, S2, _ = p.shape
    S = S2 // 2
    return pl.pallas_call(
        _fourier_kernel,
        grid=(B, S // tk),
        in_specs=[
            pl.BlockSpec((tk, S2), lambda b, j: (j, 0)),
            pl.BlockSpec((None, S2, FNET_WIDTH), lambda b, j: (b, 0, 0)),
            pl.BlockSpec((FNET_GROUPS, FNET_GROUP_DIM, FNET_GROUP_DIM), lambda b, j: (0, 0, 0)),
            pl.BlockSpec((1, FNET_WIDTH), lambda b, j: (0, 0)),
        ],
        out_specs=pl.BlockSpec((None, tk, FNET_WIDTH), lambda b, j: (b, j, 0)),
        out_shape=jax.ShapeDtypeStruct((B, S, FNET_WIDTH), F32),
        compiler_params=pltpu.CompilerParams(
            dimension_semantics=("arbitrary", "arbitrary"),
            vmem_limit_bytes=VMEM_LIMIT_BYTES),
        name="fnet_seq_dft",
    )(dmat, p, w_fmix, b_fmix)


def _main_kernel(x_ref, mod_ref, ya_ref, g_ref, wr_ref, lng_ref, lnb_ref, ws_ref, bs_ref,
                 wpa_ref, wpb_ref, wo_ref, fg_ref, o_ref, *, tm):
    x = x_ref[...]
    h = _modulated_norm(x, mod_ref, g_ref).astype(BF16)

    def proj(rng):
        return _dot(h, wr_ref[:, rng[0]:rng[1]])

    y_a = (ya_ref[...] * _silu(proj(_GA))).astype(BF16)
    merged = jax.nn.sigmoid(proj(_MA)) * _dot(y_a, wpa_ref[...])

    v = proj(_V)
    mu = jnp.mean(v, axis=-1, keepdims=True)
    vc = v - mu
    var = jnp.mean(vc * vc, axis=-1, keepdims=True)
    vln = (vc * lax.rsqrt(var + EPS) * lng_ref[...] + lnb_ref[...]).astype(BF16)
    rows = []
    for c in range(tm // CHUNK):
        r0 = c * CHUNK
        cols = []
        for hd in range(SGU_HEADS):
            lo, hi = hd * SGU_HEAD_DIM, (hd + 1) * SGU_HEAD_DIM
            cols.append(_dot(ws_ref[hd], vln[r0:r0 + CHUNK, lo:hi]))
        rows.append(jnp.concatenate(cols, axis=1) + bs_ref[...])
    mixed = jnp.concatenate(rows, axis=0)
    y_b = (proj(_U) * mixed * _silu(proj(_GB))).astype(BF16)
    merged = merged + jax.nn.sigmoid(proj(_MB)) * _dot(y_b, wpb_ref[...])

    out = _dot(merged.astype(BF16), wo_ref[...])
    gate = mod_ref[2:3, :]
    xo = x + gate * out
    ms = jnp.mean(xo * xo, axis=-1, keepdims=True)
    o_ref[...] = xo * lax.rsqrt(ms + EPS) * fg_ref[...]


def _main_call(x, mod, ya, norm_g, w_rest, ln_g, ln_b, w_s, bs_full, w_pa, w_pb, w_out,
               final_g, tm):
    B, S, _ = x.shape
    const2 = lambda b, i: (0, 0)
    return pl.pallas_call(
        functools.partial(_main_kernel, tm=tm),
        grid=(B, S // tm),
        in_specs=[
            pl.BlockSpec((None, tm, D_MODEL), lambda b, i: (b, i, 0)),
            pl.BlockSpec((None, 3, D_MODEL), lambda b, i: (b, 0, 0)),
            pl.BlockSpec((None, tm, FNET_WIDTH), lambda b, i: (b, i, 0)),
            pl.BlockSpec((1, D_MODEL), const2),
            pl.BlockSpec((D_MODEL, REST_WIDTH), const2),
            pl.BlockSpec((1, SGU_WIDTH), const2),
            pl.BlockSpec((1, SGU_WIDTH), const2),
            pl.BlockSpec((SGU_HEADS, CHUNK, CHUNK), lambda b, i: (0, 0, 0)),
            pl.BlockSpec((CHUNK, SGU_WIDTH), const2),
            pl.BlockSpec((FNET_WIDTH, D_MODEL), const2),
            pl.BlockSpec((SGU_WIDTH, D_MODEL), const2),
            pl.BlockSpec((D_MODEL, D_MODEL), const2),
            pl.BlockSpec((1, D_MODEL), const2),
        ],
        out_specs=pl.BlockSpec((None, tm, D_MODEL), lambda b, i: (b, i, 0)),
        out_shape=jax.ShapeDtypeStruct((B, S, D_MODEL), F32),
        compiler_params=pltpu.CompilerParams(
            dimension_semantics=("arbitrary", "arbitrary"),
            vmem_limit_bytes=VMEM_LIMIT_BYTES),
        name="encoder_main",
    )(x, mod, ya, norm_g, w_rest, ln_g, ln_b, w_s, bs_full, w_pa, w_pb, w_out, final_g)


def _channel_dft_table():
    c = np.arange(FNET_GROUP_DIM)
    ang = 2.0 * np.pi * ((c[:, None] * c[None, :]) % FNET_GROUP_DIM) / FNET_GROUP_DIM
    scale = 1.0 / math.sqrt(FNET_GROUP_DIM)
    return np.concatenate([np.cos(ang), np.sin(ang)], axis=1) * scale


def _sequence_dft_matrix(S):
    r = int(round(math.sqrt(S)))
    while S % r:
        r -= 1
    q = S // r
    n = np.arange(S)
    th_hi = 2.0 * np.pi * ((n[:, None] * (np.arange(q)[None, :] * r)) % S) / S
    th_lo = 2.0 * np.pi * ((n[:, None] * np.arange(r)[None, :]) % S) / S
    scale = 1.0 / math.sqrt(S)
    c1 = jnp.asarray(np.cos(th_hi) * scale, F32)[:, :, None]
    s1 = jnp.asarray(np.sin(th_hi) * scale, F32)[:, :, None]
    c2 = jnp.asarray(np.cos(th_lo), F32)[:, None, :]
    s2 = jnp.asarray(np.sin(th_lo), F32)[:, None, :]
    cos = (c1 * c2 - s1 * s2).reshape(S, S)
    msin = -(s1 * c2 + c1 * s2).reshape(S, S)
    return jnp.concatenate([cos, msin], axis=1).astype(BF16)


def _trunk(x, mod, wts, tm, tk):
    B, S, _ = x.shape
    p = _pre_call(x, mod, wts["norm_g"], wts["w_a"], wts["cs_c"], tm)
    p = p.reshape(B, 2 * S, FNET_WIDTH)
    ya = _fourier_call(_sequence_dft_matrix(S), p, wts["w_fmix"], wts["b_fmix"], tk)
    return _main_call(x, mod, ya, wts["norm_g"], wts["w_rest"], wts["ln_g"], wts["ln_b"],
                      wts["w_s"], wts["bs_full"], wts["w_pa"], wts["w_pb"], wts["w_out"],
                      wts["final_g"], tm)


def kernel(x_prompt, x_sample, c_prompt, c_sample, norm_g, w_ada, b_ada, w_in, w_fmix, b_fmix,
           sgu_ln_g, sgu_ln_b, w_s, b_s, w_pa, w_pb, w_out, final_g):
    depth = norm_g.shape[0]
    assert depth == 1
    l = 0
    nb_p = x_prompt.shape[0]
    mod = _modulation(jnp.concatenate([c_prompt, c_sample], axis=0), w_ada[l], b_ada[l])
    mod = mod.reshape(-1, 3, D_MODEL)
    wts = dict(
        norm_g=norm_g[l].reshape(1, -1),
        w_a=w_in[l][:, :FNET_WIDTH].astype(BF16),
        w_rest=w_in[l][:, FNET_WIDTH:].astype(BF16),
        cs_c=jnp.asarray(_channel_dft_table(), F32).astype(BF16),
        w_fmix=w_fmix[l].astype(BF16),
        b_fmix=b_fmix[l].reshape(1, -1),
        ln_g=sgu_ln_g[l].reshape(1, -1),
        ln_b=sgu_ln_b[l].reshape(1, -1),
        w_s=w_s[l].astype(BF16),
        bs_full=jnp.repeat(b_s[l].T, SGU_HEAD_DIM, axis=1),
        w_pa=w_pa[l].astype(BF16),
        w_pb=w_pb[l].astype(BF16),
        w_out=w_out[l].astype(BF16),
        final_g=final_g.reshape(1, -1),
    )
    y_prompt = _trunk(x_prompt, mod[:nb_p], wts, tm=256, tk=256)
    y_sample = _trunk(x_sample, mod[nb_p:], wts, tm=256, tk=256)
    return (y_prompt, y_sample)
```

```python
import functools
import math

import numpy as np
import jax
import jax.numpy as jnp
from jax import lax
from jax.experimental import pallas as pl
from jax.experimental.pallas import tpu as pltpu

D_MODEL = 1024
FNET_WIDTH = 512
FNET_GROUPS = 4
FNET_GROUP_DIM = 128
SGU_WIDTH = 512
SGU_HEADS = 4
SGU_HEAD_DIM = 128
CHUNK = 128
EPS = 1e-6
DFT_LEN = 256
SEQ_TILE = 256

_GA = (0, 512)
_U = (512, 1024)
_V = (1024, 1536)
_GB = (1536, 2048)
_MA = (2048, 3072)
_MB = (3072, 4096)
REST_WIDTH = 4096

BF16 = jnp.bfloat16
F32 = jnp.float32

VMEM_LIMIT_BYTES = 56 * 1024 * 1024


def _dot(a, b):
    return jnp.dot(a, b, preferred_element_type=F32)


def _silu(x):
    return x * jax.nn.sigmoid(x)


def _mod_kernel(c_ref, w_ref, b_ref, o_ref):
    c = c_ref[...]
    o_ref[...] = jnp.dot(_silu(c), w_ref[...], preferred_element_type=F32,
                         precision=lax.Precision.HIGHEST) + b_ref[...]


def _modulation(c, w_ada, b_ada):
    n = c.shape[0]
    return pl.pallas_call(
        _mod_kernel,
        out_shape=jax.ShapeDtypeStruct((n, 3 * D_MODEL), F32),
        compiler_params=pltpu.CompilerParams(vmem_limit_bytes=VMEM_LIMIT_BYTES),
        name="adaln_mod",
    )(c, w_ada, b_ada.reshape(1, -1))


def _modulated_norm(x, mod_ref, g_ref):
    ms = jnp.mean(x * x, axis=-1, keepdims=True)
    xn = x * lax.rsqrt(ms + EPS) * g_ref[...]
    shift = mod_ref[0:1, :]
    scale = mod_ref[1:2, :]
    return xn * (1.0 + scale) + shift


def _pre_kernel(x_ref, mod_ref, g_ref, wa_ref, perm_ref, a_ref, *, n1):
    h = _modulated_norm(x_ref[...], mod_ref, g_ref).astype(BF16)
    a = _dot(h, wa_ref[...]).astype(BF16)
    ap = _dot(perm_ref[...], a).astype(BF16)
    rows = SEQ_TILE // n1
    for q in range(n1):
        a_ref[q] = ap[q * rows:(q + 1) * rows, :]


def _pre_call(x, mod, norm_g, w_a, perm, n1):
    B, S, _ = x.shape
    rows = SEQ_TILE // n1
    return pl.pallas_call(
        functools.partial(_pre_kernel, n1=n1),
        grid=(B, S // SEQ_TILE),
        in_specs=[
            pl.BlockSpec((None, SEQ_TILE, D_MODEL), lambda b, i: (b, i, 0)),
            pl.BlockSpec((None, 3, D_MODEL), lambda b, i: (b, 0, 0)),
            pl.BlockSpec((1, D_MODEL), lambda b, i: (0, 0)),
            pl.BlockSpec((D_MODEL, FNET_WIDTH), lambda b, i: (0, 0)),
            pl.BlockSpec((SEQ_TILE, SEQ_TILE), lambda b, i: (0, 0)),
        ],
        out_specs=pl.BlockSpec((None, n1, rows, FNET_WIDTH), lambda b, i: (b, 0, i, 0)),
        out_shape=jax.ShapeDtypeStruct((B, n1, DFT_LEN, FNET_WIDTH), BF16),
        compiler_params=pltpu.CompilerParams(
            dimension_semantics=("arbitrary", "arbitrary"),
            vmem_limit_bytes=VMEM_LIMIT_BYTES),
        name="fnet_pre",
    )(x, mod, norm_g, w_a, perm)


def _cmul_const(z, w):
    zr, zi = z
    wr, wi = w.real, w.imag
    tol = 1e-12
    if abs(wi) < tol:
        return (zr, zi) if abs(wr - 1) < tol else (zr * wr, zi * wr)
    if abs(wr) < tol:
        if abs(wi + 1) < tol:
            return (zi, -zr)
        if abs(wi - 1) < tol:
            return (-zi, zr)
    if abs(abs(wr) - abs(wi)) < tol:
        c = abs(wr)
        sr, si = math.copysign(1.0, wr), math.copysign(1.0, wi)
        re = (zr if sr > 0 else -zr) - (zi if si > 0 else -zi)
        im = (zi if sr > 0 else -zi) + (zr if si > 0 else -zr)
        return (re * c, im * c)
    return (zr * wr - zi * wi, zr * wi + zi * wr)


def _block_fft(zs):
    n = len(zs)
    if n == 1:
        return zs
    even = _block_fft(zs[0::2])
    odd = _block_fft(zs[1::2])
    out = [None] * n
    for k in range(n // 2):
        w = complex(math.cos(2 * math.pi * k / n), -math.sin(2 * math.pi * k / n))
        tr, ti = _cmul_const(odd[k], w)
        er, ei = even[k]
        out[k] = (er + tr, ei + ti)
        out[k + n // 2] = (er - tr, ei - ti)
    return out


def _fourier_kernel(a_ref, m_ref, cs_ref, wf_ref, bf_ref, o_ref, *, n1):
    gs = []
    for q in range(n1):
        g = _dot(m_ref[q], a_ref[q])
        gs.append((g[:DFT_LEN], g[DFT_LEN:]))
    us = _block_fft(gs)
    width = o_ref.shape[-1]
    for k1 in range(n1):
        ur, ui = us[k1]
        for g in range(width // FNET_GROUP_DIM):
            lo, hi = g * FNET_GROUP_DIM, (g + 1) * FNET_GROUP_DIM
            z = jnp.concatenate([ur[:, lo:hi], ui[:, lo:hi]], axis=1).astype(BF16)
            f = _dot(z, cs_ref[...]).astype(BF16)
            o_ref[k1 * DFT_LEN:(k1 + 1) * DFT_LEN, lo:hi] = _dot(f, wf_ref[g]) + bf_ref[:, lo:hi]


def _fourier_call(a_perm, m_tab, cs_c, w_fmix, b_fmix):
    B, n1, _, _ = a_perm.shape
    S = n1 * DFT_LEN
    width = FNET_WIDTH // 2
    gpb = width // FNET_GROUP_DIM
    return pl.pallas_call(
        functools.partial(_fourier_kernel, n1=n1),
        grid=(B, FNET_WIDTH // width),
        in_specs=[
            pl.BlockSpec((None, n1, DFT_LEN, width), lambda b, j: (b, 0, 0, j)),
            pl.BlockSpec((n1, 2 * DFT_LEN, DFT_LEN), lambda b, j: (0, 0, 0)),
            pl.BlockSpec((2 * FNET_GROUP_DIM, FNET_GROUP_DIM), lambda b, j: (0, 0)),
            pl.BlockSpec((gpb, FNET_GROUP_DIM, FNET_GROUP_DIM), lambda b, j: (j, 0, 0)),
            pl.BlockSpec((1, width), lambda b, j: (0, j)),
        ],
        out_specs=pl.BlockSpec((None, S, width), lambda b, j: (b, 0, j)),
        out_shape=jax.ShapeDtypeStruct((B, S, FNET_WIDTH), F32),
        compiler_params=pltpu.CompilerParams(
            dimension_semantics=("arbitrary", "arbitrary"),
            vmem_limit_bytes=VMEM_LIMIT_BYTES),
        name="fnet_fft",
    )(a_perm, m_tab, cs_c, w_fmix, b_fmix)


def _main_kernel(x_ref, mod_ref, ya_ref, g_ref, wr_ref, lng_ref, lnb_ref, ws_ref, bs_ref,
                 wpa_ref, wpb_ref, wo_ref, fg_ref, o_ref, *, tm):
    x = x_ref[...]
    h = _modulated_norm(x, mod_ref, g_ref).astype(BF16)

    def proj(rng):
        return _dot(h, wr_ref[:, rng[0]:rng[1]])

    y_a = (ya_ref[...] * _silu(proj(_GA))).astype(BF16)
    merged = jax.nn.sigmoid(proj(_MA)) * _dot(y_a, wpa_ref[...])

    v = proj(_V)
    mu = jnp.mean(v, axis=-1, keepdims=True)
    vc = v - mu
    var = jnp.mean(vc * vc, axis=-1, keepdims=True)
    vln = (vc * lax.rsqrt(var + EPS) * lng_ref[...] + lnb_ref[...]).astype(BF16)
    rows = []
    for c in range(tm // CHUNK):
        r0 = c * CHUNK
        cols = []
        for hd in range(SGU_HEADS):
            lo, hi = hd * SGU_HEAD_DIM, (hd + 1) * SGU_HEAD_DIM
            cols.append(_dot(ws_ref[hd], vln[r0:r0 + CHUNK, lo:hi]))
        rows.append(jnp.concatenate(cols, axis=1) + bs_ref[...])
    mixed = jnp.concatenate(rows, axis=0)
    y_b = (proj(_U) * mixed * _silu(proj(_GB))).astype(BF16)
    merged = merged + jax.nn.sigmoid(proj(_MB)) * _dot(y_b, wpb_ref[...])

    out = _dot(merged.astype(BF16), wo_ref[...])
    gate = mod_ref[2:3, :]
    xo = x + gate * out
    ms = jnp.mean(xo * xo, axis=-1, keepdims=True)
    o_ref[...] = xo * lax.rsqrt(ms + EPS) * fg_ref[...]


def _main_call(x, mod, ya, norm_g, w_rest, ln_g, ln_b, w_s, bs_full, w_pa, w_pb, w_out,
               final_g, tm):
    B, S, _ = x.shape
    const2 = lambda b, i: (0, 0)
    return pl.pallas_call(
        functools.partial(_main_kernel, tm=tm),
        grid=(B, S // tm),
        in_specs=[
            pl.BlockSpec((None, tm, D_MODEL), lambda b, i: (b, i, 0)),
            pl.BlockSpec((None, 3, D_MODEL), lambda b, i: (b, 0, 0)),
            pl.BlockSpec((None, tm, FNET_WIDTH), lambda b, i: (b, i, 0)),
            pl.BlockSpec((1, D_MODEL), const2),
            pl.BlockSpec((D_MODEL, REST_WIDTH), const2),
            pl.BlockSpec((1, SGU_WIDTH), const2),
            pl.BlockSpec((1, SGU_WIDTH), const2),
            pl.BlockSpec((SGU_HEADS, CHUNK, CHUNK), lambda b, i: (0, 0, 0)),
            pl.BlockSpec((CHUNK, SGU_WIDTH), const2),
            pl.BlockSpec((FNET_WIDTH, D_MODEL), const2),
            pl.BlockSpec((SGU_WIDTH, D_MODEL), const2),
            pl.BlockSpec((D_MODEL, D_MODEL), const2),
            pl.BlockSpec((1, D_MODEL), const2),
        ],
        out_specs=pl.BlockSpec((None, tm, D_MODEL), lambda b, i: (b, i, 0)),
        out_shape=jax.ShapeDtypeStruct((B, S, D_MODEL), F32),
        compiler_params=pltpu.CompilerParams(
            dimension_semantics=("arbitrary", "arbitrary"),
            vmem_limit_bytes=VMEM_LIMIT_BYTES),
        name="encoder_main",
    )(x, mod, ya, norm_g, w_rest, ln_g, ln_b, w_s, bs_full, w_pa, w_pb, w_out, final_g)


def _channel_dft_table():
    c = np.arange(FNET_GROUP_DIM)
    ang = 2.0 * np.pi * ((c[:, None] * c[None, :]) % FNET_GROUP_DIM) / FNET_GROUP_DIM
    scale = 1.0 / math.sqrt(FNET_GROUP_DIM)
    return np.concatenate([np.cos(ang), np.sin(ang)], axis=0) * scale


def _regroup_matrix(n1):
    rows = SEQ_TILE // n1
    p = np.zeros((SEQ_TILE, SEQ_TILE), np.float32)
    for q in range(n1):
        for j in range(rows):
            p[q * rows + j, j * n1 + q] = 1.0
    return p


def _sequence_dft_table(n1):
    S = n1 * DFT_LEN
    k2 = np.arange(DFT_LEN)
    th1 = 2.0 * np.pi * ((k2[None, :] * np.arange(n1)[:, None]) % S) / S
    th2 = 2.0 * np.pi * ((k2[:, None] * k2[None, :]) % DFT_LEN) / DFT_LEN
    scale = 1.0 / math.sqrt(S)
    c1 = jnp.asarray(np.cos(th1) * scale, F32)[:, :, None]
    s1 = jnp.asarray(np.sin(th1) * scale, F32)[:, :, None]
    c2 = jnp.asarray(np.cos(th2), F32)[None]
    s2 = jnp.asarray(np.sin(th2), F32)[None]
    cos = c1 * c2 - s1 * s2
    msin = -(s1 * c2 + c1 * s2)
    return jnp.concatenate([cos, msin], axis=1).astype(BF16)


def _trunk(x, mod, wts):
    B, S, _ = x.shape
    assert S % DFT_LEN == 0
    n1 = S // DFT_LEN
    assert n1 & (n1 - 1) == 0 and SEQ_TILE % n1 == 0
    perm = jnp.asarray(_regroup_matrix(n1), F32).astype(BF16)
    a_perm = _pre_call(x, mod, wts["norm_g"], wts["w_a"], perm, n1)
    ya = _fourier_call(a_perm, _sequence_dft_table(n1), wts["cs_c"], wts["w_fmix"], wts["b_fmix"])
    return _main_call(x, mod, ya, wts["norm_g"], wts["w_rest"], wts["ln_g"], wts["ln_b"],
                      wts["w_s"], wts["bs_full"], wts["w_pa"], wts["w_pb"], wts["w_out"],
                      wts["final_g"], SEQ_TILE)


def kernel(x_prompt, x_sample, c_prompt, c_sample, norm_g, w_ada, b_ada, w_in, w_fmix, b_fmix,
           sgu_ln_g, sgu_ln_b, w_s, b_s, w_pa, w_pb, w_out, final_g):
    depth = norm_g.shape[0]
    assert depth == 1
    l = 0
    nb_p = x_prompt.shape[0]
    mod = _modulation(jnp.concatenate([c_prompt, c_sample], axis=0), w_ada[l], b_ada[l])
    mod = mod.reshape(-1, 3, D_MODEL)
    wts = dict(
        norm_g=norm_g[l].reshape(1, -1),
        w_a=w_in[l][:, :FNET_WIDTH].astype(BF16),
        w_rest=w_in[l][:, FNET_WIDTH:].astype(BF16),
        cs_c=jnp.asarray(_channel_dft_table(), F32).astype(BF16),
        w_fmix=w_fmix[l].astype(BF16),
        b_fmix=b_fmix[l].reshape(1, -1),
        ln_g=sgu_ln_g[l].reshape(1, -1),
        ln_b=sgu_ln_b[l].reshape(1, -1),
        w_s=w_s[l].astype(BF16),
        bs_full=jnp.repeat(b_s[l].T, SGU_HEAD_DIM, axis=1),
        w_pa=w_pa[l].astype(BF16),
        w_pb=w_pb[l].astype(BF16),
        w_out=w_out[l].astype(BF16),
        final_g=final_g.reshape(1, -1),
    )
    y_prompt = _trunk(x_prompt, mod[:nb_p], wts)
    y_sample = _trunk(x_sample, mod[nb_p:], wts)
    return (y_prompt, y_sample)
```

```python
import functools
import math

import numpy as np
import jax
import jax.numpy as jnp
from jax import lax
from jax.experimental import pallas as pl
from jax.experimental.pallas import tpu as pltpu

D_MODEL = 1024
FNET_WIDTH = 512
FNET_GROUPS = 4
FNET_GROUP_DIM = 128
SGU_WIDTH = 512
SGU_HEADS = 4
SGU_HEAD_DIM = 128
CHUNK = 128
EPS = 1e-6
DFT_LEN = 256
SEQ_TILE = 256
MAIN_TILE = 512

_GA = (0, 512)
_U = (512, 1024)
_V = (1024, 1536)
_GB = (1536, 2048)
_MA = (2048, 3072)
_MB = (3072, 4096)
REST_WIDTH = 4096

BF16 = jnp.bfloat16
F32 = jnp.float32

VMEM_LIMIT_BYTES = 56 * 1024 * 1024


def _dot(a, b):
    return jnp.dot(a, b, preferred_element_type=F32)


def _silu(x):
    return x * jax.nn.sigmoid(x)


def _mod_kernel(c_ref, w_ref, b_ref, o_ref):
    c = c_ref[...]
    o_ref[...] = jnp.dot(_silu(c), w_ref[...], preferred_element_type=F32,
                         precision=lax.Precision.HIGHEST) + b_ref[...]


def _modulation(c, w_ada, b_ada):
    n = c.shape[0]
    return pl.pallas_call(
        _mod_kernel,
        out_shape=jax.ShapeDtypeStruct((n, 3 * D_MODEL), F32),
        compiler_params=pltpu.CompilerParams(vmem_limit_bytes=VMEM_LIMIT_BYTES),
        name="adaln_mod",
    )(c, w_ada, b_ada.reshape(1, -1))


def _modulated_norm(x, mod_ref, g_ref):
    ms = jnp.mean(x * x, axis=-1, keepdims=True)
    xn = x * lax.rsqrt(ms + EPS) * g_ref[...]
    shift = mod_ref[0:1, :]
    scale = mod_ref[1:2, :]
    return xn * (1.0 + scale) + shift


def _pre_kernel(x_ref, mod_ref, g_ref, wa_ref, perm_ref, a_ref, *, n1):
    h = _modulated_norm(x_ref[...], mod_ref, g_ref).astype(BF16)
    a = _dot(h, wa_ref[...]).astype(BF16)
    ap = _dot(perm_ref[...], a).astype(BF16)
    rows = SEQ_TILE // n1
    for q in range(n1):
        a_ref[q] = ap[q * rows:(q + 1) * rows, :]


def _pre_call(x, mod, norm_g, w_a, perm, n1):
    B, S, _ = x.shape
    rows = SEQ_TILE // n1
    return pl.pallas_call(
        functools.partial(_pre_kernel, n1=n1),
        grid=(B, S // SEQ_TILE),
        in_specs=[
            pl.BlockSpec((None, SEQ_TILE, D_MODEL), lambda b, i: (b, i, 0)),
            pl.BlockSpec((None, 3, D_MODEL), lambda b, i: (b, 0, 0)),
            pl.BlockSpec((1, D_MODEL), lambda b, i: (0, 0)),
            pl.BlockSpec((D_MODEL, FNET_WIDTH), lambda b, i: (0, 0)),
            pl.BlockSpec((SEQ_TILE, SEQ_TILE), lambda b, i: (0, 0)),
        ],
        out_specs=pl.BlockSpec((None, n1, rows, FNET_WIDTH), lambda b, i: (b, 0, i, 0)),
        out_shape=jax.ShapeDtypeStruct((B, n1, DFT_LEN, FNET_WIDTH), BF16),
        compiler_params=pltpu.CompilerParams(
            dimension_semantics=("arbitrary", "arbitrary"),
            vmem_limit_bytes=VMEM_LIMIT_BYTES),
        name="fnet_pre",
    )(x, mod, norm_g, w_a, perm)


def _cmul_const(z, w):
    zr, zi = z
    wr, wi = w.real, w.imag
    tol = 1e-12
    if abs(wi) < tol:
        return (zr, zi) if abs(wr - 1) < tol else (zr * wr, zi * wr)
    if abs(wr) < tol:
        if abs(wi + 1) < tol:
            return (zi, -zr)
        if abs(wi - 1) < tol:
            return (-zi, zr)
    if abs(abs(wr) - abs(wi)) < tol:
        c = abs(wr)
        sr, si = math.copysign(1.0, wr), math.copysign(1.0, wi)
        re = (zr if sr > 0 else -zr) - (zi if si > 0 else -zi)
        im = (zi if sr > 0 else -zi) + (zr if si > 0 else -zr)
        return (re * c, im * c)
    return (zr * wr - zi * wi, zr * wi + zi * wr)


def _block_fft(zs):
    n = len(zs)
    if n == 1:
        return zs
    even = _block_fft(zs[0::2])
    odd = _block_fft(zs[1::2])
    out = [None] * n
    for k in range(n // 2):
        w = complex(math.cos(2 * math.pi * k / n), -math.sin(2 * math.pi * k / n))
        tr, ti = _cmul_const(odd[k], w)
        er, ei = even[k]
        out[k] = (er + tr, ei + ti)
        out[k + n // 2] = (er - tr, ei - ti)
    return out


def _fourier_kernel(a_ref, m_ref, cs_ref, wf_ref, bf_ref, o_ref, *, n1):
    gs = []
    for q in range(n1):
        g = _dot(m_ref[q], a_ref[q])
        gs.append((g[:DFT_LEN], g[DFT_LEN:]))
    us = _block_fft(gs)
    width = o_ref.shape[-1]
    for k1 in range(n1):
        ur, ui = us[k1]
        for g in range(width // FNET_GROUP_DIM):
            lo, hi = g * FNET_GROUP_DIM, (g + 1) * FNET_GROUP_DIM
            z = jnp.concatenate([ur[:, lo:hi], ui[:, lo:hi]], axis=1).astype(BF16)
            f = _dot(z, cs_ref[...]).astype(BF16)
            o_ref[k1 * DFT_LEN:(k1 + 1) * DFT_LEN, lo:hi] = _dot(f, wf_ref[g]) + bf_ref[:, lo:hi]


def _fourier_call(a_perm, m_tab, cs_c, w_fmix, b_fmix):
    B, n1, _, _ = a_perm.shape
    S = n1 * DFT_LEN
    width = FNET_WIDTH // 2
    gpb = width // FNET_GROUP_DIM
    return pl.pallas_call(
        functools.partial(_fourier_kernel, n1=n1),
        grid=(B, FNET_WIDTH // width),
        in_specs=[
            pl.BlockSpec((None, n1, DFT_LEN, width), lambda b, j: (b, 0, 0, j)),
            pl.BlockSpec((n1, 2 * DFT_LEN, DFT_LEN), lambda b, j: (0, 0, 0)),
            pl.BlockSpec((2 * FNET_GROUP_DIM, FNET_GROUP_DIM), lambda b, j: (0, 0)),
            pl.BlockSpec((gpb, FNET_GROUP_DIM, FNET_GROUP_DIM), lambda b, j: (j, 0, 0)),
            pl.BlockSpec((1, width), lambda b, j: (0, j)),
        ],
        out_specs=pl.BlockSpec((None, S, width), lambda b, j: (b, 0, j)),
        out_shape=jax.ShapeDtypeStruct((B, S, FNET_WIDTH), F32),
        compiler_params=pltpu.CompilerParams(
            dimension_semantics=("arbitrary", "arbitrary"),
            vmem_limit_bytes=VMEM_LIMIT_BYTES),
        name="fnet_fft",
    )(a_perm, m_tab, cs_c, w_fmix, b_fmix)


def _main_kernel(x_ref, mod_ref, ya_ref, g_ref, wr_ref, lng_ref, lnb_ref, ws_ref, bs_ref,
                 wpa_ref, wpb_ref, wo_ref, fg_ref, o_ref, *, tm):
    x = x_ref[...]
    h = _modulated_norm(x, mod_ref, g_ref).astype(BF16)

    def proj(rng):
        return _dot(h, wr_ref[:, rng[0]:rng[1]])

    y_a = (ya_ref[...] * _silu(proj(_GA))).astype(BF16)
    merged = jax.nn.sigmoid(proj(_MA)) * _dot(y_a, wpa_ref[...])

    v = proj(_V)
    mu = jnp.mean(v, axis=-1, keepdims=True)
    vc = v - mu
    var = jnp.mean(vc * vc, axis=-1, keepdims=True)
    vln = (vc * lax.rsqrt(var + EPS) * lng_ref[...] + lnb_ref[...]).astype(BF16)
    rows = []
    for c in range(tm // CHUNK):
        r0 = c * CHUNK
        cols = []
        for hd in range(SGU_HEADS):
            lo, hi = hd * SGU_HEAD_DIM, (hd + 1) * SGU_HEAD_DIM
            cols.append(_dot(ws_ref[hd], vln[r0:r0 + CHUNK, lo:hi]))
        rows.append(jnp.concatenate(cols, axis=1) + bs_ref[...])
    mixed = jnp.concatenate(rows, axis=0)
    y_b = (proj(_U) * mixed * _silu(proj(_GB))).astype(BF16)
    merged = merged + jax.nn.sigmoid(proj(_MB)) * _dot(y_b, wpb_ref[...])

    out = _dot(merged.astype(BF16), wo_ref[...])
    gate = mod_ref[2:3, :]
    xo = x + gate * out
    ms = jnp.mean(xo * xo, axis=-1, keepdims=True)
    o_ref[...] = xo * lax.rsqrt(ms + EPS) * fg_ref[...]


def _main_call(x, mod, ya, norm_g, w_rest, ln_g, ln_b, w_s, bs_full, w_pa, w_pb, w_out,
               final_g, tm):
    B, S, _ = x.shape
    const2 = lambda b, i: (0, 0)
    return pl.pallas_call(
        functools.partial(_main_kernel, tm=tm),
        grid=(B, S // tm),
        in_specs=[
            pl.BlockSpec((None, tm, D_MODEL), lambda b, i: (b, i, 0)),
            pl.BlockSpec((None, 3, D_MODEL), lambda b, i: (b, 0, 0)),
            pl.BlockSpec((None, tm, FNET_WIDTH), lambda b, i: (b, i, 0)),
            pl.BlockSpec((1, D_MODEL), const2),
            pl.BlockSpec((D_MODEL, REST_WIDTH), const2, pipeline_mode=pl.Buffered(1)),
            pl.BlockSpec((1, SGU_WIDTH), const2),
            pl.BlockSpec((1, SGU_WIDTH), const2),
            pl.BlockSpec((SGU_HEADS, CHUNK, CHUNK), lambda b, i: (0, 0, 0)),
            pl.BlockSpec((CHUNK, SGU_WIDTH), const2),
            pl.BlockSpec((FNET_WIDTH, D_MODEL), const2, pipeline_mode=pl.Buffered(1)),
            pl.BlockSpec((SGU_WIDTH, D_MODEL), const2, pipeline_mode=pl.Buffered(1)),
            pl.BlockSpec((D_MODEL, D_MODEL), const2, pipeline_mode=pl.Buffered(1)),
            pl.BlockSpec((1, D_MODEL), const2),
        ],
        out_specs=pl.BlockSpec((None, tm, D_MODEL), lambda b, i: (b, i, 0)),
        out_shape=jax.ShapeDtypeStruct((B, S, D_MODEL), F32),
        compiler_params=pltpu.CompilerParams(
            dimension_semantics=("arbitrary", "arbitrary"),
            vmem_limit_bytes=VMEM_LIMIT_BYTES),
        name="encoder_main",
    )(x, mod, ya, norm_g, w_rest, ln_g, ln_b, w_s, bs_full, w_pa, w_pb, w_out, final_g)


def _channel_dft_table():
    c = np.arange(FNET_GROUP_DIM)
    ang = 2.0 * np.pi * ((c[:, None] * c[None, :]) % FNET_GROUP_DIM) / FNET_GROUP_DIM
    scale = 1.0 / math.sqrt(FNET_GROUP_DIM)
    return np.concatenate([np.cos(ang), np.sin(ang)], axis=0) * scale


def _regroup_matrix(n1):
    rows = SEQ_TILE // n1
    p = np.zeros((SEQ_TILE, SEQ_TILE), np.float32)
    for q in range(n1):
        for j in range(rows):
            p[q * rows + j, j * n1 + q] = 1.0
    return p


def _sequence_dft_table(n1):
    S = n1 * DFT_LEN
    k2 = np.arange(DFT_LEN)
    th1 = 2.0 * np.pi * ((k2[None, :] * np.arange(n1)[:, None]) % S) / S
    th2 = 2.0 * np.pi * ((k2[:, None] * k2[None, :]) % DFT_LEN) / DFT_LEN
    scale = 1.0 / math.sqrt(S)
    c1 = jnp.asarray(np.cos(th1) * scale, F32)[:, :, None]
    s1 = jnp.asarray(np.sin(th1) * scale, F32)[:, :, None]
    c2 = jnp.asarray(np.cos(th2), F32)[None]
    s2 = jnp.asarray(np.sin(th2), F32)[None]
    cos = c1 * c2 - s1 * s2
    msin = -(s1 * c2 + c1 * s2)
    return jnp.concatenate([cos, msin], axis=1).astype(BF16)


def _trunk(x, mod, wts):
    B, S, _ = x.shape
    assert S % DFT_LEN == 0
    n1 = S // DFT_LEN
    assert n1 & (n1 - 1) == 0 and SEQ_TILE % n1 == 0
    perm = jnp.asarray(_regroup_matrix(n1), F32).astype(BF16)
    a_perm = _pre_call(x, mod, wts["norm_g"], wts["w_a"], perm, n1)
    ya = _fourier_call(a_perm, _sequence_dft_table(n1), wts["cs_c"], wts["w_fmix"], wts["b_fmix"])
    return _main_call(x, mod, ya, wts["norm_g"], wts["w_rest"], wts["ln_g"], wts["ln_b"],
                      wts["w_s"], wts["bs_full"], wts["w_pa"], wts["w_pb"], wts["w_out"],
                      wts["final_g"], MAIN_TILE)


def kernel(x_prompt, x_sample, c_prompt, c_sample, norm_g, w_ada, b_ada, w_in, w_fmix, b_fmix,
           sgu_ln_g, sgu_ln_b, w_s, b_s, w_pa, w_pb, w_out, final_g):
    depth = norm_g.shape[0]
    assert depth == 1
    l = 0
    nb_p = x_prompt.shape[0]
    mod = _modulation(jnp.concatenate([c_prompt, c_sample], axis=0), w_ada[l], b_ada[l])
    mod = mod.reshape(-1, 3, D_MODEL)
    wts = dict(
        norm_g=norm_g[l].reshape(1, -1),
        w_a=w_in[l][:, :FNET_WIDTH].astype(BF16),
        w_rest=w_in[l][:, FNET_WIDTH:].astype(BF16),
        cs_c=jnp.asarray(_channel_dft_table(), F32).astype(BF16),
        w_fmix=w_fmix[l].astype(BF16),
        b_fmix=b_fmix[l].reshape(1, -1),
        ln_g=sgu_ln_g[l].reshape(1, -1),
        ln_b=sgu_ln_b[l].reshape(1, -1),
        w_s=w_s[l].astype(BF16),
        bs_full=jnp.repeat(b_s[l].T, SGU_HEAD_DIM, axis=1),
        w_pa=w_pa[l].astype(BF16),
        w_pb=w_pb[l].astype(BF16),
        w_out=w_out[l].astype(BF16),
        final_g=final_g.reshape(1, -1),
    )
    y_prompt = _trunk(x_prompt, mod[:nb_p], wts)
    y_sample = _trunk(x_sample, mod[nb_p:], wts)
    return (y_prompt, y_sample)
```

```python
import functools
import math

import numpy as np
import jax
import jax.numpy as jnp
from jax import lax
from jax.experimental import pallas as pl
from jax.experimental.pallas import tpu as pltpu

D_MODEL = 1024
FNET_WIDTH = 512
FNET_GROUPS = 4
FNET_GROUP_DIM = 128
SGU_WIDTH = 512
SGU_HEADS = 4
SGU_HEAD_DIM = 128
CHUNK = 128
EPS = 1e-6
DFT_LEN = 256
SEQ_TILE = 256
PRE_TILE = 1024
MAIN_TILE = 512

_GA = (0, 512)
_U = (512, 1024)
_V = (1024, 1536)
_GB = (1536, 2048)
_MA = (2048, 3072)
_MB = (3072, 4096)
REST_WIDTH = 4096

BF16 = jnp.bfloat16
F32 = jnp.float32

VMEM_LIMIT_BYTES = 56 * 1024 * 1024


def _dot(a, b):
    return jnp.dot(a, b, preferred_element_type=F32)


def _silu(x):
    return x * jax.nn.sigmoid(x)


def _mod_kernel(c_ref, w_ref, b_ref, cs_ref, wf_ref, o_ref, e_ref):
    c = c_ref[...]
    o_ref[...] = jnp.dot(_silu(c), w_ref[...], preferred_element_type=F32,
                         precision=lax.Precision.HIGHEST) + b_ref[...]
    for g in range(FNET_GROUPS):
        e_ref[g] = jnp.dot(cs_ref[...], wf_ref[g], preferred_element_type=F32,
                           precision=lax.Precision.HIGHEST).astype(BF16)


def _modulation(c, w_ada, b_ada, cs_c, w_fmix):
    n = c.shape[0]
    return pl.pallas_call(
        _mod_kernel,
        out_shape=(jax.ShapeDtypeStruct((n, 3 * D_MODEL), F32),
                   jax.ShapeDtypeStruct((FNET_GROUPS, 2 * FNET_GROUP_DIM, FNET_GROUP_DIM), BF16)),
        compiler_params=pltpu.CompilerParams(vmem_limit_bytes=VMEM_LIMIT_BYTES),
        name="adaln_mod",
    )(c, w_ada, b_ada.reshape(1, -1), cs_c, w_fmix)


def _modulated_norm(x, mod_ref, g_ref):
    ms = jnp.mean(x * x, axis=-1, keepdims=True)
    xn = x * lax.rsqrt(ms + EPS) * g_ref[...]
    shift = mod_ref[0:1, :]
    scale = mod_ref[1:2, :]
    return xn * (1.0 + scale) + shift


def _pre_kernel(x_ref, mod_ref, g_ref, wa_ref, perm_ref, a_ref, *, n1):
    h = _modulated_norm(x_ref[...], mod_ref, g_ref).astype(BF16)
    a = _dot(h, wa_ref[...]).astype(BF16)
    rows = SEQ_TILE // n1
    for s in range(PRE_TILE // SEQ_TILE):
        ap = _dot(perm_ref[...], a[s * SEQ_TILE:(s + 1) * SEQ_TILE]).astype(BF16)
        for q in range(n1):
            a_ref[q, s * rows:(s + 1) * rows, :] = ap[q * rows:(q + 1) * rows, :]


def _pre_call(x, mod, norm_g, w_a, perm, n1):
    B, S, _ = x.shape
    rows = PRE_TILE // n1
    return pl.pallas_call(
        functools.partial(_pre_kernel, n1=n1),
        grid=(B, S // PRE_TILE),
        in_specs=[
            pl.BlockSpec((None, PRE_TILE, D_MODEL), lambda b, i: (b, i, 0)),
            pl.BlockSpec((None, 3, D_MODEL), lambda b, i: (b, 0, 0)),
            pl.BlockSpec((1, D_MODEL), lambda b, i: (0, 0)),
            pl.BlockSpec((D_MODEL, FNET_WIDTH), lambda b, i: (0, 0)),
            pl.BlockSpec((SEQ_TILE, SEQ_TILE), lambda b, i: (0, 0)),
        ],
        out_specs=pl.BlockSpec((None, n1, rows, FNET_WIDTH), lambda b, i: (b, 0, i, 0)),
        out_shape=jax.ShapeDtypeStruct((B, n1, DFT_LEN, FNET_WIDTH), BF16),
        compiler_params=pltpu.CompilerParams(
            dimension_semantics=("arbitrary", "arbitrary"),
            vmem_limit_bytes=VMEM_LIMIT_BYTES),
        name="fnet_pre",
    )(x, mod, norm_g, w_a, perm)


def _cmul_const(z, w):
    zr, zi = z
    wr, wi = w.real, w.imag
    tol = 1e-12
    if abs(wi) < tol:
        return (zr, zi) if abs(wr - 1) < tol else (zr * wr, zi * wr)
    if abs(wr) < tol:
        if abs(wi + 1) < tol:
            return (zi, -zr)
        if abs(wi - 1) < tol:
            return (-zi, zr)
    if abs(abs(wr) - abs(wi)) < tol:
        c = abs(wr)
        sr, si = math.copysign(1.0, wr), math.copysign(1.0, wi)
        re = (zr if sr > 0 else -zr) - (zi if si > 0 else -zi)
        im = (zi if sr > 0 else -zi) + (zr if si > 0 else -zr)
        return (re * c, im * c)
    return (zr * wr - zi * wi, zr * wi + zi * wr)


def _block_fft(zs):
    n = len(zs)
    if n == 1:
        return zs
    even = _block_fft(zs[0::2])
    odd = _block_fft(zs[1::2])
    out = [None] * n
    for k in range(n // 2):
        w = complex(math.cos(2 * math.pi * k / n), -math.sin(2 * math.pi * k / n))
        tr, ti = _cmul_const(odd[k], w)
        er, ei = even[k]
        out[k] = (er + tr, ei + ti)
        out[k + n // 2] = (er - tr, ei - ti)
    return out


def _fourier_kernel(a_ref, m_ref, e_ref, bf_ref, o_ref, *, n1):
    gs = []
    for q in range(n1):
        g = _dot(m_ref[q], a_ref[q])
        gs.append((g[:DFT_LEN], g[DFT_LEN:]))
    us = _block_fft(gs)
    width = o_ref.shape[-1]
    for g in range(width // FNET_GROUP_DIM):
        lo, hi = g * FNET_GROUP_DIM, (g + 1) * FNET_GROUP_DIM
        z = jnp.concatenate(
            [jnp.concatenate([ur[:, lo:hi], ui[:, lo:hi]], axis=1).astype(BF16) for ur, ui in us], axis=0)
        o_ref[:, lo:hi] = _dot(z, e_ref[g]) + bf_ref[:, lo:hi]


def _fourier_call(a_perm, m_tab, e_tab, b_fmix):
    B, n1, _, _ = a_perm.shape
    S = n1 * DFT_LEN
    width = FNET_WIDTH // 2
    gpb = width // FNET_GROUP_DIM
    return pl.pallas_call(
        functools.partial(_fourier_kernel, n1=n1),
        grid=(B, FNET_WIDTH // width),
        in_specs=[
            pl.BlockSpec((None, n1, DFT_LEN, width), lambda b, j: (b, 0, 0, j)),
            pl.BlockSpec((n1, 2 * DFT_LEN, DFT_LEN), lambda b, j: (0, 0, 0)),
            pl.BlockSpec((gpb, 2 * FNET_GROUP_DIM, FNET_GROUP_DIM), lambda b, j: (j, 0, 0)),
            pl.BlockSpec((1, width), lambda b, j: (0, j)),
        ],
        out_specs=pl.BlockSpec((None, S, width), lambda b, j: (b, 0, j)),
        out_shape=jax.ShapeDtypeStruct((B, S, FNET_WIDTH), F32),
        compiler_params=pltpu.CompilerParams(
            dimension_semantics=("arbitrary", "arbitrary"),
            vmem_limit_bytes=VMEM_LIMIT_BYTES),
        name="fnet_fft",
    )(a_perm, m_tab, e_tab, b_fmix)


def _main_kernel(x_ref, mod_ref, ya_ref, g_ref, wr_ref, lng_ref, lnb_ref, ws_ref, bs_ref,
                 wpa_ref, wpb_ref, wo_ref, fg_ref, o_ref, *, tm):
    x = x_ref[...]
    h = _modulated_norm(x, mod_ref, g_ref).astype(BF16)

    def proj(rng):
        return _dot(h, wr_ref[:, rng[0]:rng[1]])

    y_a = (ya_ref[...] * _silu(proj(_GA))).astype(BF16)
    merged = jax.nn.sigmoid(proj(_MA)) * _dot(y_a, wpa_ref[...])

    v = proj(_V)
    mu = jnp.mean(v, axis=-1, keepdims=True)
    vc = v - mu
    var = jnp.mean(vc * vc, axis=-1, keepdims=True)
    vln = (vc * lax.rsqrt(var + EPS) * lng_ref[...] + lnb_ref[...]).astype(BF16)
    rows = []
    for c in range(tm // CHUNK):
        r0 = c * CHUNK
        cols = []
        for hd in range(SGU_HEADS):
            lo, hi = hd * SGU_HEAD_DIM, (hd + 1) * SGU_HEAD_DIM
            cols.append(_dot(ws_ref[hd], vln[r0:r0 + CHUNK, lo:hi]))
        rows.append(jnp.concatenate(cols, axis=1) + bs_ref[...])
    mixed = jnp.concatenate(rows, axis=0)
    y_b = (proj(_U) * mixed * _silu(proj(_GB))).astype(BF16)
    merged = merged + jax.nn.sigmoid(proj(_MB)) * _dot(y_b, wpb_ref[...])

    out = _dot(merged.astype(BF16), wo_ref[...])
    gate = mod_ref[2:3, :]
    xo = x + gate * out
    ms = jnp.mean(xo * xo, axis=-1, keepdims=True)
    o_ref[...] = xo * lax.rsqrt(ms + EPS) * fg_ref[...]


def _main_call(x, mod, ya, norm_g, w_rest, ln_g, ln_b, w_s, bs_full, w_pa, w_pb, w_out,
               final_g, tm):
    B, S, _ = x.shape
    const2 = lambda b, i: (0, 0)
    return pl.pallas_call(
        functools.partial(_main_kernel, tm=tm),
        grid=(B, S // tm),
        in_specs=[
            pl.BlockSpec((None, tm, D_MODEL), lambda b, i: (b, i, 0)),
            pl.BlockSpec((None, 3, D_MODEL), lambda b, i: (b, 0, 0)),
            pl.BlockSpec((None, tm, FNET_WIDTH), lambda b, i: (b, i, 0)),
            pl.BlockSpec((1, D_MODEL), const2),
            pl.BlockSpec((D_MODEL, REST_WIDTH), const2, pipeline_mode=pl.Buffered(1)),
            pl.BlockSpec((1, SGU_WIDTH), const2),
            pl.BlockSpec((1, SGU_WIDTH), const2),
            pl.BlockSpec((SGU_HEADS, CHUNK, CHUNK), lambda b, i: (0, 0, 0)),
            pl.BlockSpec((CHUNK, SGU_WIDTH), const2),
            pl.BlockSpec((FNET_WIDTH, D_MODEL), const2, pipeline_mode=pl.Buffered(1)),
            pl.BlockSpec((SGU_WIDTH, D_MODEL), const2, pipeline_mode=pl.Buffered(1)),
            pl.BlockSpec((D_MODEL, D_MODEL), const2, pipeline_mode=pl.Buffered(1)),
            pl.BlockSpec((1, D_MODEL), const2),
        ],
        out_specs=pl.BlockSpec((None, tm, D_MODEL), lambda b, i: (b, i, 0)),
        out_shape=jax.ShapeDtypeStruct((B, S, D_MODEL), F32),
        compiler_params=pltpu.CompilerParams(
            dimension_semantics=("arbitrary", "arbitrary"),
            vmem_limit_bytes=VMEM_LIMIT_BYTES),
        name="encoder_main",
    )(x, mod, ya, norm_g, w_rest, ln_g, ln_b, w_s, bs_full, w_pa, w_pb, w_out, final_g)


def _channel_dft_table():
    c = np.arange(FNET_GROUP_DIM)
    ang = 2.0 * np.pi * ((c[:, None] * c[None, :]) % FNET_GROUP_DIM) / FNET_GROUP_DIM
    scale = 1.0 / math.sqrt(FNET_GROUP_DIM)
    return np.concatenate([np.cos(ang), np.sin(ang)], axis=0) * scale


def _regroup_matrix(n1):
    rows = SEQ_TILE // n1
    p = np.zeros((SEQ_TILE, SEQ_TILE), np.float32)
    for q in range(n1):
        for j in range(rows):
            p[q * rows + j, j * n1 + q] = 1.0
    return p


def _sequence_dft_table(n1):
    S = n1 * DFT_LEN
    k2 = np.arange(DFT_LEN)
    th1 = 2.0 * np.pi * ((k2[None, :] * np.arange(n1)[:, None]) % S) / S
    th2 = 2.0 * np.pi * ((k2[:, None] * k2[None, :]) % DFT_LEN) / DFT_LEN
    scale = 1.0 / math.sqrt(S)
    c1 = jnp.asarray(np.cos(th1) * scale, F32)[:, :, None]
    s1 = jnp.asarray(np.sin(th1) * scale, F32)[:, :, None]
    c2 = jnp.asarray(np.cos(th2), F32)[None]
    s2 = jnp.asarray(np.sin(th2), F32)[None]
    cos = c1 * c2 - s1 * s2
    msin = -(s1 * c2 + c1 * s2)
    return jnp.concatenate([cos, msin], axis=1).astype(BF16)


def _trunk(x, mod, wts):
    B, S, _ = x.shape
    assert S % DFT_LEN == 0
    n1 = S // DFT_LEN
    assert n1 & (n1 - 1) == 0 and SEQ_TILE % n1 == 0 and S % PRE_TILE == 0 and S % MAIN_TILE == 0
    perm = jnp.asarray(_regroup_matrix(n1), F32).astype(BF16)
    a_perm = _pre_call(x, mod, wts["norm_g"], wts["w_a"], perm, n1)
    ya = _fourier_call(a_perm, _sequence_dft_table(n1), wts["e_tab"], wts["b_fmix"])
    return _main_call(x, mod, ya, wts["norm_g"], wts["w_rest"], wts["ln_g"], wts["ln_b"],
                      wts["w_s"], wts["bs_full"], wts["w_pa"], wts["w_pb"], wts["w_out"],
                      wts["final_g"], MAIN_TILE)


def kernel(x_prompt, x_sample, c_prompt, c_sample, norm_g, w_ada, b_ada, w_in, w_fmix, b_fmix,
           sgu_ln_g, sgu_ln_b, w_s, b_s, w_pa, w_pb, w_out, final_g):
    depth = norm_g.shape[0]
    assert depth == 1
    l = 0
    nb_p = x_prompt.shape[0]
    mod, e_tab = _modulation(jnp.concatenate([c_prompt, c_sample], axis=0), w_ada[l], b_ada[l],
                             jnp.asarray(_channel_dft_table(), F32), w_fmix[l])
    mod = mod.reshape(-1, 3, D_MODEL)
    wts = dict(
        norm_g=norm_g[l].reshape(1, -1),
        w_a=w_in[l][:, :FNET_WIDTH].astype(BF16),
        w_rest=w_in[l][:, FNET_WIDTH:].astype(BF16),
        e_tab=e_tab,
        b_fmix=b_fmix[l].reshape(1, -1),
        ln_g=sgu_ln_g[l].reshape(1, -1),
        ln_b=sgu_ln_b[l].reshape(1, -1),
        w_s=w_s[l].astype(BF16),
        bs_full=jnp.repeat(b_s[l].T, SGU_HEAD_DIM, axis=1),
        w_pa=w_pa[l].astype(BF16),
        w_pb=w_pb[l].astype(BF16),
        w_out=w_out[l].astype(BF16),
        final_g=final_g.reshape(1, -1),
    )
    y_prompt = _trunk(x_prompt, mod[:nb_p], wts)
    y_sample = _trunk(x_sample, mod[nb_p:], wts)
    return (y_prompt, y_sample)
```

```python
import functools
import math

import numpy as np
import jax
import jax.numpy as jnp
from jax import lax
from jax.experimental import pallas as pl
from jax.experimental.pallas import tpu as pltpu

D_MODEL = 1024
FNET_WIDTH = 512
FNET_GROUPS = 4
FNET_GROUP_DIM = 128
SGU_WIDTH = 512
SGU_HEADS = 4
SGU_HEAD_DIM = 128
CHUNK = 128
EPS = 1e-6
DFT_LEN = 256
SEQ_TILE = 256
PRE_TILE = 1024
MAIN_TILE = 1024
MAIN_SUB = 256

_GA = (0, 512)
_U = (512, 1024)
_V = (1024, 1536)
_GB = (1536, 2048)
_MA = (2048, 3072)
_MB = (3072, 4096)
REST_WIDTH = 4096

BF16 = jnp.bfloat16
F32 = jnp.float32

VMEM_LIMIT_BYTES = 56 * 1024 * 1024


def _dot(a, b):
    return jnp.dot(a, b, preferred_element_type=F32)


def _silu(x):
    return x * jax.nn.sigmoid(x)


def _mod_kernel(c_ref, w_ref, b_ref, cs_ref, wf_ref, o_ref, e_ref):
    c = c_ref[...]
    o_ref[...] = jnp.dot(_silu(c), w_ref[...], preferred_element_type=F32,
                         precision=lax.Precision.HIGHEST) + b_ref[...]
    for g in range(FNET_GROUPS):
        e_ref[g] = jnp.dot(cs_ref[...], wf_ref[g], preferred_element_type=F32,
                           precision=lax.Precision.HIGHEST).astype(BF16)


def _modulation(c, w_ada, b_ada, cs_c, w_fmix):
    n = c.shape[0]
    return pl.pallas_call(
        _mod_kernel,
        out_shape=(jax.ShapeDtypeStruct((n, 3 * D_MODEL), F32),
                   jax.ShapeDtypeStruct((FNET_GROUPS, 2 * FNET_GROUP_DIM, FNET_GROUP_DIM), BF16)),
        compiler_params=pltpu.CompilerParams(vmem_limit_bytes=VMEM_LIMIT_BYTES),
        name="adaln_mod",
    )(c, w_ada, b_ada.reshape(1, -1), cs_c, w_fmix)


def _modulated_norm(x, mod_ref, g_ref):
    ms = jnp.mean(x * x, axis=-1, keepdims=True)
    xn = x * lax.rsqrt(ms + EPS) * g_ref[...]
    shift = mod_ref[0:1, :]
    scale = mod_ref[1:2, :]
    return xn * (1.0 + scale) + shift


def _pre_kernel(x_ref, mod_ref, g_ref, wa_ref, perm_ref, a_ref, *, n1):
    h = _modulated_norm(x_ref[...], mod_ref, g_ref).astype(BF16)
    a = _dot(h, wa_ref[...]).astype(BF16)
    rows = SEQ_TILE // n1
    for s in range(PRE_TILE // SEQ_TILE):
        ap = _dot(perm_ref[...], a[s * SEQ_TILE:(s + 1) * SEQ_TILE]).astype(BF16)
        for q in range(n1):
            a_ref[q, s * rows:(s + 1) * rows, :] = ap[q * rows:(q + 1) * rows, :]


def _pre_call(x, mod, norm_g, w_a, perm, n1):
    B, S, _ = x.shape
    rows = PRE_TILE // n1
    return pl.pallas_call(
        functools.partial(_pre_kernel, n1=n1),
        grid=(B, S // PRE_TILE),
        in_specs=[
            pl.BlockSpec((None, PRE_TILE, D_MODEL), lambda b, i: (b, i, 0)),
            pl.BlockSpec((None, 3, D_MODEL), lambda b, i: (b, 0, 0)),
            pl.BlockSpec((1, D_MODEL), lambda b, i: (0, 0)),
            pl.BlockSpec((D_MODEL, FNET_WIDTH), lambda b, i: (0, 0)),
            pl.BlockSpec((SEQ_TILE, SEQ_TILE), lambda b, i: (0, 0)),
        ],
        out_specs=pl.BlockSpec((None, n1, rows, FNET_WIDTH), lambda b, i: (b, 0, i, 0)),
        out_shape=jax.ShapeDtypeStruct((B, n1, DFT_LEN, FNET_WIDTH), BF16),
        compiler_params=pltpu.CompilerParams(
            dimension_semantics=("arbitrary", "arbitrary"),
            vmem_limit_bytes=VMEM_LIMIT_BYTES),
        name="fnet_pre",
    )(x, mod, norm_g, w_a, perm)


def _cmul_const(z, w):
    zr, zi = z
    wr, wi = w.real, w.imag
    tol = 1e-12
    if abs(wi) < tol:
        return (zr, zi) if abs(wr - 1) < tol else (zr * wr, zi * wr)
    if abs(wr) < tol:
        if abs(wi + 1) < tol:
            return (zi, -zr)
        if abs(wi - 1) < tol:
            return (-zi, zr)
    if abs(abs(wr) - abs(wi)) < tol:
        c = abs(wr)
        sr, si = math.copysign(1.0, wr), math.copysign(1.0, wi)
        re = (zr if sr > 0 else -zr) - (zi if si > 0 else -zi)
        im = (zi if sr > 0 else -zi) + (zr if si > 0 else -zr)
        return (re * c, im * c)
    return (zr * wr - zi * wi, zr * wi + zi * wr)


def _block_fft(zs):
    n = len(zs)
    if n == 1:
        return zs
    even = _block_fft(zs[0::2])
    odd = _block_fft(zs[1::2])
    out = [None] * n
    for k in range(n // 2):
        w = complex(math.cos(2 * math.pi * k / n), -math.sin(2 * math.pi * k / n))
        tr, ti = _cmul_const(odd[k], w)
        er, ei = even[k]
        out[k] = (er + tr, ei + ti)
        out[k + n // 2] = (er - tr, ei - ti)
    return out


def _fourier_kernel(a_ref, m_ref, e_ref, bf_ref, o_ref, *, n1):
    gs = []
    for q in range(n1):
        g = _dot(m_ref[q], a_ref[q])
        gs.append((g[:DFT_LEN], g[DFT_LEN:]))
    us = _block_fft(gs)
    width = o_ref.shape[-1]
    for g in range(width // FNET_GROUP_DIM):
        lo, hi = g * FNET_GROUP_DIM, (g + 1) * FNET_GROUP_DIM
        z = jnp.concatenate(
            [jnp.concatenate([ur[:, lo:hi], ui[:, lo:hi]], axis=1).astype(BF16) for ur, ui in us], axis=0)
        o_ref[:, lo:hi] = _dot(z, e_ref[g]) + bf_ref[:, lo:hi]


def _fourier_call(a_perm, m_tab, e_tab, b_fmix):
    B, n1, _, _ = a_perm.shape
    S = n1 * DFT_LEN
    width = FNET_WIDTH // 2
    gpb = width // FNET_GROUP_DIM
    return pl.pallas_call(
        functools.partial(_fourier_kernel, n1=n1),
        grid=(B, FNET_WIDTH // width),
        in_specs=[
            pl.BlockSpec((None, n1, DFT_LEN, width), lambda b, j: (b, 0, 0, j)),
            pl.BlockSpec((n1, 2 * DFT_LEN, DFT_LEN), lambda b, j: (0, 0, 0)),
            pl.BlockSpec((gpb, 2 * FNET_GROUP_DIM, FNET_GROUP_DIM), lambda b, j: (j, 0, 0)),
            pl.BlockSpec((1, width), lambda b, j: (0, j)),
        ],
        out_specs=pl.BlockSpec((None, S, width), lambda b, j: (b, 0, j)),
        out_shape=jax.ShapeDtypeStruct((B, S, FNET_WIDTH), F32),
        compiler_params=pltpu.CompilerParams(
            dimension_semantics=("arbitrary", "arbitrary"),
            vmem_limit_bytes=VMEM_LIMIT_BYTES),
        name="fnet_fft",
    )(a_perm, m_tab, e_tab, b_fmix)


def _main_kernel(x_ref, mod_ref, ya_ref, g_ref, wr_ref, lng_ref, lnb_ref, ws_ref, bs_ref,
                 wpa_ref, wpb_ref, wo_ref, fg_ref, o_ref, *, tm):
    def proj(st, rng):
        return _dot(st["h"], wr_ref[:, rng[0]:rng[1]])

    def load_norm(st):
        st["x"] = x_ref[st["rows"], :]
        st["h"] = _modulated_norm(st["x"], mod_ref, g_ref).astype(BF16)

    def project(st):
        st["v"] = proj(st, _V)
        st["ga"] = proj(st, _GA)
        st["u"] = proj(st, _U)
        st["gb"] = proj(st, _GB)
        st["ma"] = proj(st, _MA)

    def layernorm(st):
        v = st.pop("v")
        mu = jnp.mean(v, axis=-1, keepdims=True)
        vc = v - mu
        var = jnp.mean(vc * vc, axis=-1, keepdims=True)
        st["vln"] = (vc * lax.rsqrt(var + EPS) * lng_ref[...] + lnb_ref[...]).astype(BF16)

    def spatial_mix(st):
        vln = st.pop("vln")
        rows = []
        for c in range(MAIN_SUB // CHUNK):
            r0 = c * CHUNK
            cols = []
            for hd in range(SGU_HEADS):
                lo, hi = hd * SGU_HEAD_DIM, (hd + 1) * SGU_HEAD_DIM
                cols.append(_dot(ws_ref[hd], vln[r0:r0 + CHUNK, lo:hi]))
            rows.append(jnp.concatenate(cols, axis=1) + bs_ref[...])
        st["mixed"] = jnp.concatenate(rows, axis=0)

    def fourier_path(st):
        y_a = (ya_ref[st["rows"], :] * _silu(st.pop("ga"))).astype(BF16)
        st["merged"] = jax.nn.sigmoid(st.pop("ma")) * _dot(y_a, wpa_ref[...])
        st["mb"] = proj(st, _MB)

    def gating_path(st):
        y_b = (st.pop("u") * st.pop("mixed") * _silu(st.pop("gb"))).astype(BF16)
        st["merged"] = st["merged"] + jax.nn.sigmoid(st.pop("mb")) * _dot(y_b, wpb_ref[...])

    def out_proj(st):
        st["out"] = _dot(st.pop("merged").astype(BF16), wo_ref[...])

    def finish(st):
        xo = st.pop("x") + mod_ref[2:3, :] * st.pop("out")
        ms = jnp.mean(xo * xo, axis=-1, keepdims=True)
        o_ref[st["rows"], :] = xo * lax.rsqrt(ms + EPS) * fg_ref[...]

    stages = (load_norm, project, layernorm, spatial_mix, fourier_path, gating_path, out_proj, finish)
    subs = [dict(rows=slice(i * MAIN_SUB, (i + 1) * MAIN_SUB)) for i in range(tm // MAIN_SUB)]
    for t in range(len(stages) + len(subs) - 1):
        for i, st in enumerate(subs):
            if 0 <= t - i < len(stages):
                stages[t - i](st)


def _main_call(x, mod, ya, norm_g, w_rest, ln_g, ln_b, w_s, bs_full, w_pa, w_pb, w_out,
               final_g, tm):
    B, S, _ = x.shape
    const2 = lambda b, i: (0, 0)
    return pl.pallas_call(
        functools.partial(_main_kernel, tm=tm),
        grid=(B, S // tm),
        in_specs=[
            pl.BlockSpec((None, tm, D_MODEL), lambda b, i: (b, i, 0)),
            pl.BlockSpec((None, 3, D_MODEL), lambda b, i: (b, 0, 0)),
            pl.BlockSpec((None, tm, FNET_WIDTH), lambda b, i: (b, i, 0)),
            pl.BlockSpec((1, D_MODEL), const2),
            pl.BlockSpec((D_MODEL, REST_WIDTH), const2, pipeline_mode=pl.Buffered(1)),
            pl.BlockSpec((1, SGU_WIDTH), const2),
            pl.BlockSpec((1, SGU_WIDTH), const2),
            pl.BlockSpec((SGU_HEADS, CHUNK, CHUNK), lambda b, i: (0, 0, 0)),
            pl.BlockSpec((CHUNK, SGU_WIDTH), const2),
            pl.BlockSpec((FNET_WIDTH, D_MODEL), const2, pipeline_mode=pl.Buffered(1)),
            pl.BlockSpec((SGU_WIDTH, D_MODEL), const2, pipeline_mode=pl.Buffered(1)),
            pl.BlockSpec((D_MODEL, D_MODEL), const2, pipeline_mode=pl.Buffered(1)),
            pl.BlockSpec((1, D_MODEL), const2),
        ],
        out_specs=pl.BlockSpec((None, tm, D_MODEL), lambda b, i: (b, i, 0)),
        out_shape=jax.ShapeDtypeStruct((B, S, D_MODEL), F32),
        compiler_params=pltpu.CompilerParams(
            dimension_semantics=("arbitrary", "arbitrary"),
            vmem_limit_bytes=VMEM_LIMIT_BYTES),
        name="encoder_main",
    )(x, mod, ya, norm_g, w_rest, ln_g, ln_b, w_s, bs_full, w_pa, w_pb, w_out, final_g)


def _channel_dft_table():
    c = np.arange(FNET_GROUP_DIM)
    ang = 2.0 * np.pi * ((c[:, None] * c[None, :]) % FNET_GROUP_DIM) / FNET_GROUP_DIM
    scale = 1.0 / math.sqrt(FNET_GROUP_DIM)
    return np.concatenate([np.cos(ang), np.sin(ang)], axis=0) * scale


def _regroup_matrix(n1):
    rows = SEQ_TILE // n1
    p = np.zeros((SEQ_TILE, SEQ_TILE), np.float32)
    for q in range(n1):
        for j in range(rows):
            p[q * rows + j, j * n1 + q] = 1.0
    return p


def _sequence_dft_table(n1):
    S = n1 * DFT_LEN
    k2 = np.arange(DFT_LEN)
    th1 = 2.0 * np.pi * ((k2[None, :] * np.arange(n1)[:, None]) % S) / S
    th2 = 2.0 * np.pi * ((k2[:, None] * k2[None, :]) % DFT_LEN) / DFT_LEN
    scale = 1.0 / math.sqrt(S)
    c1 = jnp.asarray(np.cos(th1) * scale, F32)[:, :, None]
    s1 = jnp.asarray(np.sin(th1) * scale, F32)[:, :, None]
    c2 = jnp.asarray(np.cos(th2), F32)[None]
    s2 = jnp.asarray(np.sin(th2), F32)[None]
    cos = c1 * c2 - s1 * s2
    msin = -(s1 * c2 + c1 * s2)
    return jnp.concatenate([cos, msin], axis=1).astype(BF16)


def _trunk(x, mod, wts):
    B, S, _ = x.shape
    assert S % DFT_LEN == 0
    n1 = S // DFT_LEN
    assert n1 & (n1 - 1) == 0 and SEQ_TILE % n1 == 0 and S % PRE_TILE == 0 and S % MAIN_TILE == 0
    perm = jnp.asarray(_regroup_matrix(n1), F32).astype(BF16)
    a_perm = _pre_call(x, mod, wts["norm_g"], wts["w_a"], perm, n1)
    ya = _fourier_call(a_perm, _sequence_dft_table(n1), wts["e_tab"], wts["b_fmix"])
    return _main_call(x, mod, ya, wts["norm_g"], wts["w_rest"], wts["ln_g"], wts["ln_b"],
                      wts["w_s"], wts["bs_full"], wts["w_pa"], wts["w_pb"], wts["w_out"],
                      wts["final_g"], MAIN_TILE)


def kernel(x_prompt, x_sample, c_prompt, c_sample, norm_g, w_ada, b_ada, w_in, w_fmix, b_fmix,
           sgu_ln_g, sgu_ln_b, w_s, b_s, w_pa, w_pb, w_out, final_g):
    depth = norm_g.shape[0]
    assert depth == 1
    l = 0
    nb_p = x_prompt.shape[0]
    mod, e_tab = _modulation(jnp.concatenate([c_prompt, c_sample], axis=0), w_ada[l], b_ada[l],
                             jnp.asarray(_channel_dft_table(), F32), w_fmix[l])
    mod = mod.reshape(-1, 3, D_MODEL)
    wts = dict(
        norm_g=norm_g[l].reshape(1, -1),
        w_a=w_in[l][:, :FNET_WIDTH].astype(BF16),
        w_rest=w_in[l][:, FNET_WIDTH:].astype(BF16),
        e_tab=e_tab,
        b_fmix=b_fmix[l].reshape(1, -1),
        ln_g=sgu_ln_g[l].reshape(1, -1),
        ln_b=sgu_ln_b[l].reshape(1, -1),
        w_s=w_s[l].astype(BF16),
        bs_full=jnp.repeat(b_s[l].T, SGU_HEAD_DIM, axis=1),
        w_pa=w_pa[l].astype(BF16),
        w_pb=w_pb[l].astype(BF16),
        w_out=w_out[l].astype(BF16),
        final_g=final_g.reshape(1, -1),
    )
    y_prompt = _trunk(x_prompt, mod[:nb_p], wts)
    y_sample = _trunk(x_sample, mod[nb_p:], wts)
    return (y_prompt, y_sample)
```

```python
import functools
import math

import numpy as np
import jax
import jax.numpy as jnp
from jax import lax
from jax.experimental import pallas as pl
from jax.experimental.pallas import tpu as pltpu

D_MODEL = 1024
FNET_WIDTH = 512
FNET_GROUPS = 4
FNET_GROUP_DIM = 128
SGU_WIDTH = 512
SGU_HEADS = 4
SGU_HEAD_DIM = 128
CHUNK = 128
EPS = 1e-6
DFT_LEN = 256
SEQ_TILE = 256
PRE_TILE = 1024
MAIN_TILE = 1024
MAIN_SUB = 256

_GA = (512, 1024)
_U = (1024, 1536)
_V = (1536, 2048)
_GB = (2048, 2560)
_MA = (2560, 3584)
_MB = (3584, 4608)
IN_WIDTH = 4608

BF16 = jnp.bfloat16
F32 = jnp.float32

VMEM_LIMIT_BYTES = 56 * 1024 * 1024


def _dot(a, b):
    return jnp.dot(a, b, preferred_element_type=F32)


def _silu(x):
    return x * jax.nn.sigmoid(x)


def _mod_kernel(c_ref, w_ref, b_ref, cs_ref, wf_ref, o_ref, e_ref):
    c = c_ref[...]
    o_ref[...] = jnp.dot(_silu(c), w_ref[...], preferred_element_type=F32,
                         precision=lax.Precision.HIGHEST) + b_ref[...]
    for g in range(FNET_GROUPS):
        e_ref[g] = jnp.dot(cs_ref[...], wf_ref[g], preferred_element_type=F32,
                           precision=lax.Precision.HIGHEST).astype(BF16)


def _modulation(c, w_ada, b_ada, cs_c, w_fmix):
    n = c.shape[0]
    return pl.pallas_call(
        _mod_kernel,
        out_shape=(jax.ShapeDtypeStruct((n, 3 * D_MODEL), F32),
                   jax.ShapeDtypeStruct((FNET_GROUPS, 2 * FNET_GROUP_DIM, FNET_GROUP_DIM), BF16)),
        compiler_params=pltpu.CompilerParams(vmem_limit_bytes=VMEM_LIMIT_BYTES),
        name="adaln_mod",
    )(c, w_ada, b_ada.reshape(1, -1), cs_c, w_fmix)


def _modulated_norm(x, mod_ref, g_ref):
    ms = jnp.mean(x * x, axis=-1, keepdims=True)
    xn = x * lax.rsqrt(ms + EPS) * g_ref[...]
    shift = mod_ref[0:1, :]
    scale = mod_ref[1:2, :]
    return xn * (1.0 + scale) + shift


def _pre_kernel(x_ref, mod_ref, g_ref, win_ref, perm_ref, a_ref, *, n1):
    h = _modulated_norm(x_ref[...], mod_ref, g_ref).astype(BF16)
    a = _dot(h, win_ref[...]).astype(BF16)
    rows = SEQ_TILE // n1
    for s in range(PRE_TILE // SEQ_TILE):
        ap = _dot(perm_ref[...], a[s * SEQ_TILE:(s + 1) * SEQ_TILE]).astype(BF16)
        for q in range(n1):
            a_ref[q, s * rows:(s + 1) * rows, :] = ap[q * rows:(q + 1) * rows, :]


def _pre_call(x, mod, norm_g, w_in, perm, n1):
    B, S, _ = x.shape
    rows = PRE_TILE // n1
    return pl.pallas_call(
        functools.partial(_pre_kernel, n1=n1),
        grid=(B, S // PRE_TILE),
        in_specs=[
            pl.BlockSpec((None, PRE_TILE, D_MODEL), lambda b, i: (b, i, 0)),
            pl.BlockSpec((None, 3, D_MODEL), lambda b, i: (b, 0, 0)),
            pl.BlockSpec((1, D_MODEL), lambda b, i: (0, 0)),
            pl.BlockSpec((D_MODEL, FNET_WIDTH), lambda b, i: (0, 0)),
            pl.BlockSpec((SEQ_TILE, SEQ_TILE), lambda b, i: (0, 0)),
        ],
        out_specs=pl.BlockSpec((None, n1, rows, FNET_WIDTH), lambda b, i: (b, 0, i, 0)),
        out_shape=jax.ShapeDtypeStruct((B, n1, DFT_LEN, FNET_WIDTH), BF16),
        compiler_params=pltpu.CompilerParams(
            dimension_semantics=("arbitrary", "arbitrary"),
            vmem_limit_bytes=VMEM_LIMIT_BYTES),
        name="fnet_pre",
    )(x, mod, norm_g, w_in, perm)


def _cmul_const(z, w):
    zr, zi = z
    wr, wi = w.real, w.imag
    tol = 1e-12
    if abs(wi) < tol:
        return (zr, zi) if abs(wr - 1) < tol else (zr * wr, zi * wr)
    if abs(wr) < tol:
        if abs(wi + 1) < tol:
            return (zi, -zr)
        if abs(wi - 1) < tol:
            return (-zi, zr)
    if abs(abs(wr) - abs(wi)) < tol:
        c = abs(wr)
        sr, si = math.copysign(1.0, wr), math.copysign(1.0, wi)
        re = (zr if sr > 0 else -zr) - (zi if si > 0 else -zi)
        im = (zi if sr > 0 else -zi) + (zr if si > 0 else -zr)
        return (re * c, im * c)
    return (zr * wr - zi * wi, zr * wi + zi * wr)


def _block_fft(zs):
    n = len(zs)
    if n == 1:
        return zs
    even = _block_fft(zs[0::2])
    odd = _block_fft(zs[1::2])
    out = [None] * n
    for k in range(n // 2):
        w = complex(math.cos(2 * math.pi * k / n), -math.sin(2 * math.pi * k / n))
        tr, ti = _cmul_const(odd[k], w)
        er, ei = even[k]
        out[k] = (er + tr, ei + ti)
        out[k + n // 2] = (er - tr, ei - ti)
    return out


def _fourier_kernel(a_ref, m_ref, e_ref, bf_ref, o_ref, *, n1):
    gs = []
    for q in range(n1):
        g = _dot(m_ref[q], a_ref[q])
        gs.append((g[:DFT_LEN], g[DFT_LEN:]))
    us = _block_fft(gs)
    width = o_ref.shape[-1]
    for g in range(width // FNET_GROUP_DIM):
        lo, hi = g * FNET_GROUP_DIM, (g + 1) * FNET_GROUP_DIM
        z = jnp.concatenate(
            [jnp.concatenate([ur[:, lo:hi], ui[:, lo:hi]], axis=1).astype(BF16) for ur, ui in us], axis=0)
        o_ref[:, lo:hi] = _dot(z, e_ref[g]) + bf_ref[:, lo:hi]


def _fourier_call(a_perm, m_tab, e_tab, b_fmix):
    B, n1, _, _ = a_perm.shape
    S = n1 * DFT_LEN
    width = FNET_WIDTH // 2
    gpb = width // FNET_GROUP_DIM
    return pl.pallas_call(
        functools.partial(_fourier_kernel, n1=n1),
        grid=(B, FNET_WIDTH // width),
        in_specs=[
            pl.BlockSpec((None, n1, DFT_LEN, width), lambda b, j: (b, 0, 0, j)),
            pl.BlockSpec((n1, 2 * DFT_LEN, DFT_LEN), lambda b, j: (0, 0, 0)),
            pl.BlockSpec((gpb, 2 * FNET_GROUP_DIM, FNET_GROUP_DIM), lambda b, j: (j, 0, 0)),
            pl.BlockSpec((1, width), lambda b, j: (0, j)),
        ],
        out_specs=pl.BlockSpec((None, S, width), lambda b, j: (b, 0, j)),
        out_shape=jax.ShapeDtypeStruct((B, S, FNET_WIDTH), F32),
        compiler_params=pltpu.CompilerParams(
            dimension_semantics=("arbitrary", "arbitrary"),
            vmem_limit_bytes=VMEM_LIMIT_BYTES),
        name="fnet_fft",
    )(a_perm, m_tab, e_tab, b_fmix)


def _main_kernel(x_ref, mod_ref, ya_ref, g_ref, win_ref, lng_ref, lnb_ref, ws_ref, bs_ref,
                 wpa_ref, wpb_ref, wo_ref, fg_ref, o_ref, *, tm):
    def proj(st, rng):
        return _dot(st["h"], win_ref[:, rng[0]:rng[1]])

    def load_norm(st):
        st["x"] = x_ref[st["rows"], :]
        st["h"] = _modulated_norm(st["x"], mod_ref, g_ref).astype(BF16)

    def project(st):
        st["v"] = proj(st, _V)
        st["ga"] = proj(st, _GA)
        st["u"] = proj(st, _U)
        st["gb"] = proj(st, _GB)
        st["ma"] = proj(st, _MA)

    def layernorm(st):
        v = st.pop("v")
        mu = jnp.mean(v, axis=-1, keepdims=True)
        vc = v - mu
        var = jnp.mean(vc * vc, axis=-1, keepdims=True)
        st["vln"] = (vc * lax.rsqrt(var + EPS) * lng_ref[...] + lnb_ref[...]).astype(BF16)

    def spatial_mix(st):
        vln = st.pop("vln")
        rows = []
        for c in range(MAIN_SUB // CHUNK):
            r0 = c * CHUNK
            cols = []
            for hd in range(SGU_HEADS):
                lo, hi = hd * SGU_HEAD_DIM, (hd + 1) * SGU_HEAD_DIM
                cols.append(_dot(ws_ref[hd], vln[r0:r0 + CHUNK, lo:hi]))
            rows.append(jnp.concatenate(cols, axis=1) + bs_ref[...])
        st["mixed"] = jnp.concatenate(rows, axis=0)

    def fourier_path(st):
        y_a = (ya_ref[st["rows"], :] * _silu(st.pop("ga"))).astype(BF16)
        st["merged"] = jax.nn.sigmoid(st.pop("ma")) * _dot(y_a, wpa_ref[...])
        st["mb"] = proj(st, _MB)

    def gating_path(st):
        y_b = (st.pop("u") * st.pop("mixed") * _silu(st.pop("gb"))).astype(BF16)
        st["merged"] = st["merged"] + jax.nn.sigmoid(st.pop("mb")) * _dot(y_b, wpb_ref[...])

    def out_proj(st):
        st["out"] = _dot(st.pop("merged").astype(BF16), wo_ref[...])

    def finish(st):
        xo = st.pop("x") + mod_ref[2:3, :] * st.pop("out")
        ms = jnp.mean(xo * xo, axis=-1, keepdims=True)
        o_ref[st["rows"], :] = xo * lax.rsqrt(ms + EPS) * fg_ref[...]

    stages = (load_norm, project, layernorm, spatial_mix, fourier_path, gating_path, out_proj, finish)
    subs = [dict(rows=slice(i * MAIN_SUB, (i + 1) * MAIN_SUB)) for i in range(tm // MAIN_SUB)]
    for t in range(len(stages) + len(subs) - 1):
        for i, st in enumerate(subs):
            if 0 <= t - i < len(stages):
                stages[t - i](st)


def _main_call(x, mod, ya, norm_g, w_in, ln_g, ln_b, w_s, bs_full, w_pa, w_pb, w_out,
               final_g, tm):
    B, S, _ = x.shape
    const2 = lambda b, i: (0, 0)
    return pl.pallas_call(
        functools.partial(_main_kernel, tm=tm),
        grid=(B, S // tm),
        in_specs=[
            pl.BlockSpec((None, tm, D_MODEL), lambda b, i: (b, i, 0)),
            pl.BlockSpec((None, 3, D_MODEL), lambda b, i: (b, 0, 0)),
            pl.BlockSpec((None, tm, FNET_WIDTH), lambda b, i: (b, i, 0)),
            pl.BlockSpec((1, D_MODEL), const2),
            pl.BlockSpec((D_MODEL, IN_WIDTH), const2, pipeline_mode=pl.Buffered(1)),
            pl.BlockSpec((1, SGU_WIDTH), const2),
            pl.BlockSpec((1, SGU_WIDTH), const2),
            pl.BlockSpec((SGU_HEADS, CHUNK, CHUNK), lambda b, i: (0, 0, 0)),
            pl.BlockSpec((CHUNK, SGU_WIDTH), const2),
            pl.BlockSpec((FNET_WIDTH, D_MODEL), const2, pipeline_mode=pl.Buffered(1)),
            pl.BlockSpec((SGU_WIDTH, D_MODEL), const2, pipeline_mode=pl.Buffered(1)),
            pl.BlockSpec((D_MODEL, D_MODEL), const2, pipeline_mode=pl.Buffered(1)),
            pl.BlockSpec((1, D_MODEL), const2),
        ],
        out_specs=pl.BlockSpec((None, tm, D_MODEL), lambda b, i: (b, i, 0)),
        out_shape=jax.ShapeDtypeStruct((B, S, D_MODEL), F32),
        compiler_params=pltpu.CompilerParams(
            dimension_semantics=("arbitrary", "arbitrary"),
            vmem_limit_bytes=VMEM_LIMIT_BYTES),
        name="encoder_main",
    )(x, mod, ya, norm_g, w_in, ln_g, ln_b, w_s, bs_full, w_pa, w_pb, w_out, final_g)


def _channel_dft_table():
    c = np.arange(FNET_GROUP_DIM)
    ang = 2.0 * np.pi * ((c[:, None] * c[None, :]) % FNET_GROUP_DIM) / FNET_GROUP_DIM
    scale = 1.0 / math.sqrt(FNET_GROUP_DIM)
    return np.concatenate([np.cos(ang), np.sin(ang)], axis=0) * scale


def _regroup_matrix(n1):
    rows = SEQ_TILE // n1
    p = np.zeros((SEQ_TILE, SEQ_TILE), np.float32)
    for q in range(n1):
        for j in range(rows):
            p[q * rows + j, j * n1 + q] = 1.0
    return p


def _sequence_dft_table(n1):
    S = n1 * DFT_LEN
    k2 = np.arange(DFT_LEN)
    th1 = 2.0 * np.pi * ((k2[None, :] * np.arange(n1)[:, None]) % S) / S
    th2 = 2.0 * np.pi * ((k2[:, None] * k2[None, :]) % DFT_LEN) / DFT_LEN
    scale = 1.0 / math.sqrt(S)
    c1 = jnp.asarray(np.cos(th1) * scale, F32)[:, :, None]
    s1 = jnp.asarray(np.sin(th1) * scale, F32)[:, :, None]
    c2 = jnp.asarray(np.cos(th2), F32)[None]
    s2 = jnp.asarray(np.sin(th2), F32)[None]
    cos = c1 * c2 - s1 * s2
    msin = -(s1 * c2 + c1 * s2)
    return jnp.concatenate([cos, msin], axis=1).astype(BF16)


def _trunk(x, mod, wts):
    B, S, _ = x.shape
    assert S % DFT_LEN == 0
    n1 = S // DFT_LEN
    assert n1 & (n1 - 1) == 0 and SEQ_TILE % n1 == 0 and S % PRE_TILE == 0 and S % MAIN_TILE == 0
    perm = jnp.asarray(_regroup_matrix(n1), F32).astype(BF16)
    a_perm = _pre_call(x, mod, wts["norm_g"], wts["w_in"], perm, n1)
    ya = _fourier_call(a_perm, _sequence_dft_table(n1), wts["e_tab"], wts["b_fmix"])
    return _main_call(x, mod, ya, wts["norm_g"], wts["w_in"], wts["ln_g"], wts["ln_b"],
                      wts["w_s"], wts["bs_full"], wts["w_pa"], wts["w_pb"], wts["w_out"],
                      wts["final_g"], MAIN_TILE)


def kernel(x_prompt, x_sample, c_prompt, c_sample, norm_g, w_ada, b_ada, w_in, w_fmix, b_fmix,
           sgu_ln_g, sgu_ln_b, w_s, b_s, w_pa, w_pb, w_out, final_g):
    depth = norm_g.shape[0]
    assert depth == 1
    l = 0
    nb_p = x_prompt.shape[0]
    mod, e_tab = _modulation(jnp.concatenate([c_prompt, c_sample], axis=0), w_ada[l], b_ada[l],
                             jnp.asarray(_channel_dft_table(), F32), w_fmix[l])
    mod = mod.reshape(-1, 3, D_MODEL)
    wts = dict(
        norm_g=norm_g[l].reshape(1, -1),
        w_in=w_in[l].astype(BF16),
        e_tab=e_tab,
        b_fmix=b_fmix[l].reshape(1, -1),
        ln_g=sgu_ln_g[l].reshape(1, -1),
        ln_b=sgu_ln_b[l].reshape(1, -1),
        w_s=w_s[l].astype(BF16),
        bs_full=jnp.repeat(b_s[l].T, SGU_HEAD_DIM, axis=1),
        w_pa=w_pa[l].astype(BF16),
        w_pb=w_pb[l].astype(BF16),
        w_out=w_out[l].astype(BF16),
        final_g=final_g.reshape(1, -1),
    )
    y_prompt = _trunk(x_prompt, mod[:nb_p], wts)
    y_sample = _trunk(x_sample, mod[nb_p:], wts)
    return (y_prompt, y_sample)
```

```python
import functools
import math

import numpy as np
import jax
import jax.numpy as jnp
from jax import lax
from jax.experimental import pallas as pl
from jax.experimental.pallas import tpu as pltpu

D_MODEL = 1024
FNET_WIDTH = 512
FNET_GROUPS = 4
FNET_GROUP_DIM = 128
SGU_WIDTH = 512
SGU_HEADS = 4
SGU_HEAD_DIM = 128
CHUNK = 128
EPS = 1e-6
DFT_LEN = 256
SEQ_TILE = 256
PRE_TILE = 1024
MAIN_TILE = 1024
MAIN_SUB = 256
FFT_WIDTH = 256

_GA = (512, 1024)
_U = (1024, 1536)
_V = (1536, 2048)
_GB = (2048, 2560)
_MA = (2560, 3584)
_MB = (3584, 4608)
IN_WIDTH = 4608

BF16 = jnp.bfloat16
F32 = jnp.float32

VMEM_LIMIT_BYTES = 56 * 1024 * 1024


def _dot(a, b):
    return jnp.dot(a, b, preferred_element_type=F32)


def _silu(x):
    return x * jax.nn.sigmoid(x)


def _mod_kernel(c_ref, w_ref, b_ref, cs_ref, wf_ref, o_ref, e_ref):
    c = c_ref[...]
    o_ref[...] = jnp.dot(_silu(c), w_ref[...], preferred_element_type=F32,
                         precision=lax.Precision.HIGHEST) + b_ref[...]
    for g in range(FNET_GROUPS):
        e_ref[g] = jnp.dot(cs_ref[...], wf_ref[g], preferred_element_type=F32,
                           precision=lax.Precision.HIGHEST).astype(BF16)


def _modulation(c, w_ada, b_ada, cs_c, w_fmix):
    n = c.shape[0]
    return pl.pallas_call(
        _mod_kernel,
        out_shape=(jax.ShapeDtypeStruct((n, 3 * D_MODEL), F32),
                   jax.ShapeDtypeStruct((FNET_GROUPS, 2 * FNET_GROUP_DIM, FNET_GROUP_DIM), BF16)),
        compiler_params=pltpu.CompilerParams(vmem_limit_bytes=VMEM_LIMIT_BYTES),
        name="adaln_mod",
    )(c, w_ada, b_ada.reshape(1, -1), cs_c, w_fmix)


def _modulated_norm(x, mod_ref, g_ref):
    ms = jnp.mean(x * x, axis=-1, keepdims=True)
    xn = x * lax.rsqrt(ms + EPS) * g_ref[...]
    shift = mod_ref[0:1, :]
    scale = mod_ref[1:2, :]
    return xn * (1.0 + scale) + shift


def _pre_kernel(x_ref, mod_ref, g_ref, win_ref, perm_ref, a_ref, *, n1):
    h = _modulated_norm(x_ref[...], mod_ref, g_ref).astype(BF16)
    a = _dot(h, win_ref[...]).astype(BF16)
    rows = SEQ_TILE // n1
    for s in range(PRE_TILE // SEQ_TILE):
        ap = _dot(perm_ref[...], a[s * SEQ_TILE:(s + 1) * SEQ_TILE]).astype(BF16)
        for q in range(n1):
            a_ref[q, s * rows:(s + 1) * rows, :] = ap[q * rows:(q + 1) * rows, :]


def _pre_call(x, mod, norm_g, w_in, perm, n1):
    B, S, _ = x.shape
    rows = PRE_TILE // n1
    return pl.pallas_call(
        functools.partial(_pre_kernel, n1=n1),
        grid=(B, S // PRE_TILE),
        in_specs=[
            pl.BlockSpec((None, PRE_TILE, D_MODEL), lambda b, i: (b, i, 0)),
            pl.BlockSpec((None, 3, D_MODEL), lambda b, i: (b, 0, 0)),
            pl.BlockSpec((1, D_MODEL), lambda b, i: (0, 0)),
            pl.BlockSpec((D_MODEL, FNET_WIDTH), lambda b, i: (0, 0)),
            pl.BlockSpec((SEQ_TILE, SEQ_TILE), lambda b, i: (0, 0)),
        ],
        out_specs=pl.BlockSpec((None, n1, rows, FNET_WIDTH), lambda b, i: (b, 0, i, 0)),
        out_shape=jax.ShapeDtypeStruct((B, n1, DFT_LEN, FNET_WIDTH), BF16),
        compiler_params=pltpu.CompilerParams(
            dimension_semantics=("arbitrary", "arbitrary"),
            vmem_limit_bytes=VMEM_LIMIT_BYTES),
        name="fnet_pre",
    )(x, mod, norm_g, w_in, perm)


def _cmul_const(z, w):
    zr, zi = z
    wr, wi = w.real, w.imag
    tol = 1e-12
    if abs(wi) < tol:
        return (zr, zi) if abs(wr - 1) < tol else (zr * wr, zi * wr)
    if abs(wr) < tol:
        if abs(wi + 1) < tol:
            return (zi, -zr)
        if abs(wi - 1) < tol:
            return (-zi, zr)
    if abs(abs(wr) - abs(wi)) < tol:
        c = abs(wr)
        sr, si = math.copysign(1.0, wr), math.copysign(1.0, wi)
        re = (zr if sr > 0 else -zr) - (zi if si > 0 else -zi)
        im = (zi if sr > 0 else -zi) + (zr if si > 0 else -zr)
        return (re * c, im * c)
    return (zr * wr - zi * wi, zr * wi + zi * wr)


def _block_fft(zs):
    n = len(zs)
    if n == 1:
        return zs
    even = _block_fft(zs[0::2])
    odd = _block_fft(zs[1::2])
    out = [None] * n
    for k in range(n // 2):
        w = complex(math.cos(2 * math.pi * k / n), -math.sin(2 * math.pi * k / n))
        tr, ti = _cmul_const(odd[k], w)
        er, ei = even[k]
        out[k] = (er + tr, ei + ti)
        out[k + n // 2] = (er - tr, ei - ti)
    return out


def _run_skewed(stages, states):
    for t in range(len(stages) + len(states) - 1):
        for i, st in enumerate(states):
            if 0 <= t - i < len(stages):
                stages[t - i](st)


def _fourier_kernel(a_ref, m_ref, e_ref, bf_ref, o_ref, *, n1):
    def block_dfts(st):
        st["g"] = []
        for q in range(n1):
            g = _dot(m_ref[q], a_ref[q, :, st["cols"]])
            st["g"].append((g[:DFT_LEN], g[DFT_LEN:]))

    def across_blocks(st):
        st["u"] = _block_fft(st.pop("g"))

    def channel_map(st):
        us = st.pop("u")
        for g in st["groups"]:
            lo, hi = g * FNET_GROUP_DIM - st["cols"].start, (g + 1) * FNET_GROUP_DIM - st["cols"].start
            z = jnp.concatenate(
                [jnp.concatenate([ur[:, lo:hi], ui[:, lo:hi]], axis=1).astype(BF16) for ur, ui in us], axis=0)
            out = slice(g * FNET_GROUP_DIM, (g + 1) * FNET_GROUP_DIM)
            o_ref[:, out] = _dot(z, e_ref[g]) + bf_ref[:, out]

    halves = [dict(cols=slice(j * FFT_WIDTH, (j + 1) * FFT_WIDTH),
                   groups=range(j * FFT_WIDTH // FNET_GROUP_DIM, (j + 1) * FFT_WIDTH // FNET_GROUP_DIM))
              for j in range(FNET_WIDTH // FFT_WIDTH)]
    _run_skewed((block_dfts, across_blocks, channel_map), halves)


def _fourier_call(a_perm, m_tab, e_tab, b_fmix):
    B, n1, _, _ = a_perm.shape
    S = n1 * DFT_LEN
    return pl.pallas_call(
        functools.partial(_fourier_kernel, n1=n1),
        grid=(B,),
        in_specs=[
            pl.BlockSpec((None, n1, DFT_LEN, FNET_WIDTH), lambda b: (b, 0, 0, 0)),
            pl.BlockSpec((n1, 2 * DFT_LEN, DFT_LEN), lambda b: (0, 0, 0)),
            pl.BlockSpec((FNET_GROUPS, 2 * FNET_GROUP_DIM, FNET_GROUP_DIM), lambda b: (0, 0, 0)),
            pl.BlockSpec((1, FNET_WIDTH), lambda b: (0, 0)),
        ],
        out_specs=pl.BlockSpec((None, S, FNET_WIDTH), lambda b: (b, 0, 0)),
        out_shape=jax.ShapeDtypeStruct((B, S, FNET_WIDTH), F32),
        compiler_params=pltpu.CompilerParams(
            dimension_semantics=("arbitrary",),
            vmem_limit_bytes=VMEM_LIMIT_BYTES),
        name="fnet_fft",
    )(a_perm, m_tab, e_tab, b_fmix)


def _main_kernel(x_ref, mod_ref, ya_ref, g_ref, win_ref, lng_ref, lnb_ref, ws_ref, bs_ref,
                 wpa_ref, wpb_ref, wo_ref, fg_ref, o_ref, *, tm):
    def proj(st, rng):
        return _dot(st["h"], win_ref[:, rng[0]:rng[1]])

    def load_norm(st):
        st["x"] = x_ref[st["rows"], :]
        st["h"] = _modulated_norm(st["x"], mod_ref, g_ref).astype(BF16)

    def project(st):
        st["v"] = proj(st, _V)
        st["ga"] = proj(st, _GA)
        st["u"] = proj(st, _U)
        st["gb"] = proj(st, _GB)
        st["ma"] = proj(st, _MA)

    def layernorm(st):
        v = st.pop("v")
        mu = jnp.mean(v, axis=-1, keepdims=True)
        vc = v - mu
        var = jnp.mean(vc * vc, axis=-1, keepdims=True)
        st["vln"] = (vc * lax.rsqrt(var + EPS) * lng_ref[...] + lnb_ref[...]).astype(BF16)

    def spatial_mix(st):
        vln = st.pop("vln")
        rows = []
        for c in range(MAIN_SUB // CHUNK):
            r0 = c * CHUNK
            cols = []
            for hd in range(SGU_HEADS):
                lo, hi = hd * SGU_HEAD_DIM, (hd + 1) * SGU_HEAD_DIM
                cols.append(_dot(ws_ref[hd], vln[r0:r0 + CHUNK, lo:hi]))
            rows.append(jnp.concatenate(cols, axis=1) + bs_ref[...])
        st["mixed"] = jnp.concatenate(rows, axis=0)

    def fourier_path(st):
        y_a = (ya_ref[st["rows"], :] * _silu(st.pop("ga"))).astype(BF16)
        st["merged"] = jax.nn.sigmoid(st.pop("ma")) * _dot(y_a, wpa_ref[...])
        st["mb"] = proj(st, _MB)

    def gating_path(st):
        y_b = (st.pop("u") * st.pop("mixed") * _silu(st.pop("gb"))).astype(BF16)
        st["merged"] = st["merged"] + jax.nn.sigmoid(st.pop("mb")) * _dot(y_b, wpb_ref[...])

    def out_proj(st):
        st["out"] = _dot(st.pop("merged").astype(BF16), wo_ref[...])

    def finish(st):
        xo = st.pop("x") + mod_ref[2:3, :] * st.pop("out")
        ms = jnp.mean(xo * xo, axis=-1, keepdims=True)
        o_ref[st["rows"], :] = xo * lax.rsqrt(ms + EPS) * fg_ref[...]

    stages = (load_norm, project, layernorm, spatial_mix, fourier_path, gating_path, out_proj, finish)
    subs = [dict(rows=slice(i * MAIN_SUB, (i + 1) * MAIN_SUB)) for i in range(tm // MAIN_SUB)]
    _run_skewed(stages, subs)


def _main_call(x, mod, ya, norm_g, w_in, ln_g, ln_b, w_s, bs_full, w_pa, w_pb, w_out,
               final_g, tm):
    B, S, _ = x.shape
    const2 = lambda b, i: (0, 0)
    return pl.pallas_call(
        functools.partial(_main_kernel, tm=tm),
        grid=(B, S // tm),
        in_specs=[
            pl.BlockSpec((None, tm, D_MODEL), lambda b, i: (b, i, 0)),
            pl.BlockSpec((None, 3, D_MODEL), lambda b, i: (b, 0, 0)),
            pl.BlockSpec((None, tm, FNET_WIDTH), lambda b, i: (b, i, 0)),
            pl.BlockSpec((1, D_MODEL), const2),
            pl.BlockSpec((D_MODEL, IN_WIDTH), const2, pipeline_mode=pl.Buffered(1)),
            pl.BlockSpec((1, SGU_WIDTH), const2),
            pl.BlockSpec((1, SGU_WIDTH), const2),
            pl.BlockSpec((SGU_HEADS, CHUNK, CHUNK), lambda b, i: (0, 0, 0)),
            pl.BlockSpec((CHUNK, SGU_WIDTH), const2),
            pl.BlockSpec((FNET_WIDTH, D_MODEL), const2, pipeline_mode=pl.Buffered(1)),
            pl.BlockSpec((SGU_WIDTH, D_MODEL), const2, pipeline_mode=pl.Buffered(1)),
            pl.BlockSpec((D_MODEL, D_MODEL), const2, pipeline_mode=pl.Buffered(1)),
            pl.BlockSpec((1, D_MODEL), const2),
        ],
        out_specs=pl.BlockSpec((None, tm, D_MODEL), lambda b, i: (b, i, 0)),
        out_shape=jax.ShapeDtypeStruct((B, S, D_MODEL), F32),
        compiler_params=pltpu.CompilerParams(
            dimension_semantics=("arbitrary", "arbitrary"),
            vmem_limit_bytes=VMEM_LIMIT_BYTES),
        name="encoder_main",
    )(x, mod, ya, norm_g, w_in, ln_g, ln_b, w_s, bs_full, w_pa, w_pb, w_out, final_g)


def _channel_dft_table():
    c = np.arange(FNET_GROUP_DIM)
    ang = 2.0 * np.pi * ((c[:, None] * c[None, :]) % FNET_GROUP_DIM) / FNET_GROUP_DIM
    scale = 1.0 / math.sqrt(FNET_GROUP_DIM)
    return np.concatenate([np.cos(ang), np.sin(ang)], axis=0) * scale


def _regroup_matrix(n1):
    rows = SEQ_TILE // n1
    p = np.zeros((SEQ_TILE, SEQ_TILE), np.float32)
    for q in range(n1):
        for j in range(rows):
            p[q * rows + j, j * n1 + q] = 1.0
    return p


def _sequence_dft_table(n1):
    S = n1 * DFT_LEN
    k2 = np.arange(DFT_LEN)
    th1 = 2.0 * np.pi * ((k2[None, :] * np.arange(n1)[:, None]) % S) / S
    th2 = 2.0 * np.pi * ((k2[:, None] * k2[None, :]) % DFT_LEN) / DFT_LEN
    scale = 1.0 / math.sqrt(S)
    c1 = jnp.asarray(np.cos(th1) * scale, F32)[:, :, None]
    s1 = jnp.asarray(np.sin(th1) * scale, F32)[:, :, None]
    c2 = jnp.asarray(np.cos(th2), F32)[None]
    s2 = jnp.asarray(np.sin(th2), F32)[None]
    cos = c1 * c2 - s1 * s2
    msin = -(s1 * c2 + c1 * s2)
    return jnp.concatenate([cos, msin], axis=1).astype(BF16)


def _trunk(x, mod, wts):
    B, S, _ = x.shape
    assert S % DFT_LEN == 0
    n1 = S // DFT_LEN
    assert n1 & (n1 - 1) == 0 and SEQ_TILE % n1 == 0 and S % PRE_TILE == 0 and S % MAIN_TILE == 0
    perm = jnp.asarray(_regroup_matrix(n1), F32).astype(BF16)
    a_perm = _pre_call(x, mod, wts["norm_g"], wts["w_in"], perm, n1)
    ya = _fourier_call(a_perm, _sequence_dft_table(n1), wts["e_tab"], wts["b_fmix"])
    return _main_call(x, mod, ya, wts["norm_g"], wts["w_in"], wts["ln_g"], wts["ln_b"],
                      wts["w_s"], wts["bs_full"], wts["w_pa"], wts["w_pb"], wts["w_out"],
                      wts["final_g"], MAIN_TILE)


def kernel(x_prompt, x_sample, c_prompt, c_sample, norm_g, w_ada, b_ada, w_in, w_fmix, b_fmix,
           sgu_ln_g, sgu_ln_b, w_s, b_s, w_pa, w_pb, w_out, final_g):
    depth = norm_g.shape[0]
    assert depth == 1
    l = 0
    nb_p = x_prompt.shape[0]
    mod, e_tab = _modulation(jnp.concatenate([c_prompt, c_sample], axis=0), w_ada[l], b_ada[l],
                             jnp.asarray(_channel_dft_table(), F32), w_fmix[l])
    mod = mod.reshape(-1, 3, D_MODEL)
    wts = dict(
        norm_g=norm_g[l].reshape(1, -1),
        w_in=w_in[l].astype(BF16),
        e_tab=e_tab,
        b_fmix=b_fmix[l].reshape(1, -1),
        ln_g=sgu_ln_g[l].reshape(1, -1),
        ln_b=sgu_ln_b[l].reshape(1, -1),
        w_s=w_s[l].astype(BF16),
        bs_full=jnp.repeat(b_s[l].T, SGU_HEAD_DIM, axis=1),
        w_pa=w_pa[l].astype(BF16),
        w_pb=w_pb[l].astype(BF16),
        w_out=w_out[l].astype(BF16),
        final_g=final_g.reshape(1, -1),
    )
    y_prompt = _trunk(x_prompt, mod[:nb_p], wts)
    y_sample = _trunk(x_sample, mod[nb_p:], wts)
    return (y_prompt, y_sample)
```

```python
import functools
import math

import numpy as np
import jax
import jax.numpy as jnp
from jax import lax
from jax.experimental import pallas as pl
from jax.experimental.pallas import tpu as pltpu

D_MODEL = 1024
FNET_WIDTH = 512
FNET_GROUPS = 4
FNET_GROUP_DIM = 128
SGU_WIDTH = 512
SGU_HEADS = 4
SGU_HEAD_DIM = 128
CHUNK = 128
EPS = 1e-6
MOD_BLOCK = 512
DFT_LEN = 256
SEQ_TILE = 256
PRE_TILE = 2048
MAIN_TILE = 1024
MAIN_SUB = 256
FFT_WIDTH = 256

_GA = (512, 1024)
_U = (1024, 1536)
_V = (1536, 2048)
_GB = (2048, 2560)
_MA = (2560, 3584)
_MB = (3584, 4608)
IN_WIDTH = 4608

BF16 = jnp.bfloat16
F32 = jnp.float32

VMEM_LIMIT_BYTES = 56 * 1024 * 1024


def _dot(a, b):
    return jnp.dot(a, b, preferred_element_type=F32)


def _silu(x):
    return x * jax.nn.sigmoid(x)


def _split_bf16(x):
    hi = x.astype(BF16)
    return hi, (x - hi.astype(F32)).astype(BF16)


def _dot_3pass(a, b):
    a_hi, a_lo = _split_bf16(a)
    b_hi, b_lo = _split_bf16(b)
    return _dot(a_hi, b_hi) + (_dot(a_lo, b_hi) + _dot(a_hi, b_lo))


def _mod_kernel(c_ref, w_ref, b_ref, cs_ref, wf_ref, o_ref, e_ref):
    o_ref[...] = _dot_3pass(_silu(c_ref[...]), w_ref[...]) + b_ref[...]

    @pl.when(pl.program_id(0) == 0)
    def _():
        for g in range(FNET_GROUPS):
            e_ref[g] = _dot_3pass(cs_ref[...], wf_ref[g]).astype(BF16)


def _modulation(c, w_ada, b_ada, cs_c, w_fmix):
    n = c.shape[0]
    e_shape = (FNET_GROUPS, 2 * FNET_GROUP_DIM, FNET_GROUP_DIM)
    return pl.pallas_call(
        _mod_kernel,
        grid=(3 * D_MODEL // MOD_BLOCK,),
        in_specs=[
            pl.BlockSpec((n, D_MODEL), lambda j: (0, 0)),
            pl.BlockSpec((D_MODEL, MOD_BLOCK), lambda j: (0, j)),
            pl.BlockSpec((1, MOD_BLOCK), lambda j: (0, j)),
            pl.BlockSpec((2 * FNET_GROUP_DIM, FNET_GROUP_DIM), lambda j: (0, 0)),
            pl.BlockSpec((FNET_GROUPS, FNET_GROUP_DIM, FNET_GROUP_DIM), lambda j: (0, 0, 0)),
        ],
        out_specs=(pl.BlockSpec((n, MOD_BLOCK), lambda j: (0, j)),
                   pl.BlockSpec(e_shape, lambda j: (0, 0, 0))),
        out_shape=(jax.ShapeDtypeStruct((n, 3 * D_MODEL), F32),
                   jax.ShapeDtypeStruct(e_shape, BF16)),
        compiler_params=pltpu.CompilerParams(
            dimension_semantics=("arbitrary",),
            vmem_limit_bytes=VMEM_LIMIT_BYTES),
        name="adaln_mod",
    )(c, w_ada, b_ada.reshape(1, -1), cs_c, w_fmix)


def _modulated_norm(x, mod_ref, g_ref):
    ms = jnp.mean(x * x, axis=-1, keepdims=True)
    xn = x * lax.rsqrt(ms + EPS) * g_ref[...]
    shift = mod_ref[0:1, :]
    scale = mod_ref[1:2, :]
    return xn * (1.0 + scale) + shift


def _pre_kernel(x_ref, mod_ref, g_ref, win_ref, perm_ref, a_ref, *, n1):
    h = _modulated_norm(x_ref[...], mod_ref, g_ref).astype(BF16)
    a = _dot(h, win_ref[...]).astype(BF16)
    rows = SEQ_TILE // n1
    for s in range(PRE_TILE // SEQ_TILE):
        ap = _dot(perm_ref[...], a[s * SEQ_TILE:(s + 1) * SEQ_TILE]).astype(BF16)
        for q in range(n1):
            a_ref[q, s * rows:(s + 1) * rows, :] = ap[q * rows:(q + 1) * rows, :]


def _pre_call(x, mod, norm_g, w_in, perm, n1):
    B, S, _ = x.shape
    rows = PRE_TILE // n1
    return pl.pallas_call(
        functools.partial(_pre_kernel, n1=n1),
        grid=(B, S // PRE_TILE),
        in_specs=[
            pl.BlockSpec((None, PRE_TILE, D_MODEL), lambda b, i: (b, i, 0)),
            pl.BlockSpec((None, 3, D_MODEL), lambda b, i: (b, 0, 0)),
            pl.BlockSpec((1, D_MODEL), lambda b, i: (0, 0)),
            pl.BlockSpec((D_MODEL, FNET_WIDTH), lambda b, i: (0, 0)),
            pl.BlockSpec((SEQ_TILE, SEQ_TILE), lambda b, i: (0, 0)),
        ],
        out_specs=pl.BlockSpec((None, n1, rows, FNET_WIDTH), lambda b, i: (b, 0, i, 0)),
        out_shape=jax.ShapeDtypeStruct((B, n1, DFT_LEN, FNET_WIDTH), BF16),
        compiler_params=pltpu.CompilerParams(
            dimension_semantics=("arbitrary", "arbitrary"),
            vmem_limit_bytes=VMEM_LIMIT_BYTES),
        name="fnet_pre",
    )(x, mod, norm_g, w_in, perm)


def _cmul_const(z, w):
    zr, zi = z
    wr, wi = w.real, w.imag
    tol = 1e-12
    if abs(wi) < tol:
        return (zr, zi) if abs(wr - 1) < tol else (zr * wr, zi * wr)
    if abs(wr) < tol:
        if abs(wi + 1) < tol:
            return (zi, -zr)
        if abs(wi - 1) < tol:
            return (-zi, zr)
    if abs(abs(wr) - abs(wi)) < tol:
        c = abs(wr)
        sr, si = math.copysign(1.0, wr), math.copysign(1.0, wi)
        re = (zr if sr > 0 else -zr) - (zi if si > 0 else -zi)
        im = (zi if sr > 0 else -zi) + (zr if si > 0 else -zr)
        return (re * c, im * c)
    return (zr * wr - zi * wi, zr * wi + zi * wr)


def _block_fft(zs):
    n = len(zs)
    if n == 1:
        return zs
    even = _block_fft(zs[0::2])
    odd = _block_fft(zs[1::2])
    out = [None] * n
    for k in range(n // 2):
        w = complex(math.cos(2 * math.pi * k / n), -math.sin(2 * math.pi * k / n))
        tr, ti = _cmul_const(odd[k], w)
        er, ei = even[k]
        out[k] = (er + tr, ei + ti)
        out[k + n // 2] = (er - tr, ei - ti)
    return out


def _run_skewed(stages, states):
    for t in range(len(stages) + len(states) - 1):
        for i, st in enumerate(states):
            if 0 <= t - i < len(stages):
                stages[t - i](st)


def _fourier_kernel(a_ref, m_ref, e_ref, bf_ref, o_ref, *, n1):
    def block_dfts(st):
        st["g"] = []
        for q in range(n1):
            g = _dot(m_ref[q], a_ref[q, :, st["cols"]])
            st["g"].append((g[:DFT_LEN], g[DFT_LEN:]))

    def across_blocks(st):
        st["u"] = _block_fft(st.pop("g"))

    def channel_map(st):
        us = st.pop("u")
        for g in st["groups"]:
            lo, hi = g * FNET_GROUP_DIM - st["cols"].start, (g + 1) * FNET_GROUP_DIM - st["cols"].start
            z = jnp.concatenate(
                [jnp.concatenate([ur[:, lo:hi], ui[:, lo:hi]], axis=1).astype(BF16) for ur, ui in us], axis=0)
            out = slice(g * FNET_GROUP_DIM, (g + 1) * FNET_GROUP_DIM)
            o_ref[:, out] = _dot(z, e_ref[g]) + bf_ref[:, out]

    halves = [dict(cols=slice(j * FFT_WIDTH, (j + 1) * FFT_WIDTH),
                   groups=range(j * FFT_WIDTH // FNET_GROUP_DIM, (j + 1) * FFT_WIDTH // FNET_GROUP_DIM))
              for j in range(FNET_WIDTH // FFT_WIDTH)]
    _run_skewed((block_dfts, across_blocks, channel_map), halves)


def _fourier_call(a_perm, m_tab, e_tab, b_fmix):
    B, n1, _, _ = a_perm.shape
    S = n1 * DFT_LEN
    return pl.pallas_call(
        functools.partial(_fourier_kernel, n1=n1),
        grid=(B,),
        in_specs=[
            pl.BlockSpec((None, n1, DFT_LEN, FNET_WIDTH), lambda b: (b, 0, 0, 0)),
            pl.BlockSpec((n1, 2 * DFT_LEN, DFT_LEN), lambda b: (0, 0, 0)),
            pl.BlockSpec((FNET_GROUPS, 2 * FNET_GROUP_DIM, FNET_GROUP_DIM), lambda b: (0, 0, 0)),
            pl.BlockSpec((1, FNET_WIDTH), lambda b: (0, 0)),
        ],
        out_specs=pl.BlockSpec((None, S, FNET_WIDTH), lambda b: (b, 0, 0)),
        out_shape=jax.ShapeDtypeStruct((B, S, FNET_WIDTH), F32),
        compiler_params=pltpu.CompilerParams(
            dimension_semantics=("arbitrary",),
            vmem_limit_bytes=VMEM_LIMIT_BYTES),
        name="fnet_fft",
    )(a_perm, m_tab, e_tab, b_fmix)


def _main_kernel(x_ref, mod_ref, ya_ref, g_ref, win_ref, lng_ref, lnb_ref, ws_ref, bs_ref,
                 wpa_ref, wpb_ref, wo_ref, fg_ref, o_ref, *, tm):
    def proj(st, rng):
        return _dot(st["h"], win_ref[:, rng[0]:rng[1]])

    def load_norm(st):
        st["x"] = x_ref[st["rows"], :]
        st["h"] = _modulated_norm(st["x"], mod_ref, g_ref).astype(BF16)

    def project(st):
        st["v"] = proj(st, _V)
        st["ga"] = proj(st, _GA)
        st["u"] = proj(st, _U)
        st["gb"] = proj(st, _GB)
        st["ma"] = proj(st, _MA)

    def layernorm(st):
        v = st.pop("v")
        mu = jnp.mean(v, axis=-1, keepdims=True)
        vc = v - mu
        var = jnp.mean(vc * vc, axis=-1, keepdims=True)
        st["vln"] = (vc * lax.rsqrt(var + EPS) * lng_ref[...] + lnb_ref[...]).astype(BF16)

    def spatial_mix(st):
        vln = st.pop("vln")
        rows = []
        for c in range(MAIN_SUB // CHUNK):
            r0 = c * CHUNK
            cols = []
            for hd in range(SGU_HEADS):
                lo, hi = hd * SGU_HEAD_DIM, (hd + 1) * SGU_HEAD_DIM
                cols.append(_dot(ws_ref[hd], vln[r0:r0 + CHUNK, lo:hi]))
            rows.append(jnp.concatenate(cols, axis=1) + bs_ref[...])
        st["mixed"] = jnp.concatenate(rows, axis=0)

    def fourier_path(st):
        y_a = (ya_ref[st["rows"], :] * _silu(st.pop("ga"))).astype(BF16)
        st["merged"] = jax.nn.sigmoid(st.pop("ma")) * _dot(y_a, wpa_ref[...])
        st["mb"] = proj(st, _MB)

    def gating_path(st):
        y_b = (st.pop("u") * st.pop("mixed") * _silu(st.pop("gb"))).astype(BF16)
        st["merged"] = st["merged"] + jax.nn.sigmoid(st.pop("mb")) * _dot(y_b, wpb_ref[...])

    def out_proj(st):
        st["out"] = _dot(st.pop("merged").astype(BF16), wo_ref[...])

    def finish(st):
        xo = st.pop("x") + mod_ref[2:3, :] * st.pop("out")
        ms = jnp.mean(xo * xo, axis=-1, keepdims=True)
        o_ref[st["rows"], :] = xo * lax.rsqrt(ms + EPS) * fg_ref[...]

    stages = (load_norm, project, layernorm, spatial_mix, fourier_path, gating_path, out_proj, finish)
    subs = [dict(rows=slice(i * MAIN_SUB, (i + 1) * MAIN_SUB)) for i in range(tm // MAIN_SUB)]
    _run_skewed(stages, subs)


def _main_call(x, mod, ya, norm_g, w_in, ln_g, ln_b, w_s, bs_full, w_pa, w_pb, w_out,
               final_g, tm):
    B, S, _ = x.shape
    const2 = lambda b, i: (0, 0)
    return pl.pallas_call(
        functools.partial(_main_kernel, tm=tm),
        grid=(B, S // tm),
        in_specs=[
            pl.BlockSpec((None, tm, D_MODEL), lambda b, i: (b, i, 0)),
            pl.BlockSpec((None, 3, D_MODEL), lambda b, i: (b, 0, 0)),
            pl.BlockSpec((None, tm, FNET_WIDTH), lambda b, i: (b, i, 0)),
            pl.BlockSpec((1, D_MODEL), const2),
            pl.BlockSpec((D_MODEL, IN_WIDTH), const2, pipeline_mode=pl.Buffered(1)),
            pl.BlockSpec((1, SGU_WIDTH), const2),
            pl.BlockSpec((1, SGU_WIDTH), const2),
            pl.BlockSpec((SGU_HEADS, CHUNK, CHUNK), lambda b, i: (0, 0, 0)),
            pl.BlockSpec((CHUNK, SGU_WIDTH), const2),
            pl.BlockSpec((FNET_WIDTH, D_MODEL), const2, pipeline_mode=pl.Buffered(1)),
            pl.BlockSpec((SGU_WIDTH, D_MODEL), const2, pipeline_mode=pl.Buffered(1)),
            pl.BlockSpec((D_MODEL, D_MODEL), const2, pipeline_mode=pl.Buffered(1)),
            pl.BlockSpec((1, D_MODEL), const2),
        ],
        out_specs=pl.BlockSpec((None, tm, D_MODEL), lambda b, i: (b, i, 0)),
        out_shape=jax.ShapeDtypeStruct((B, S, D_MODEL), F32),
        compiler_params=pltpu.CompilerParams(
            dimension_semantics=("arbitrary", "arbitrary"),
            vmem_limit_bytes=VMEM_LIMIT_BYTES),
        name="encoder_main",
    )(x, mod, ya, norm_g, w_in, ln_g, ln_b, w_s, bs_full, w_pa, w_pb, w_out, final_g)


def _channel_dft_table():
    c = np.arange(FNET_GROUP_DIM)
    ang = 2.0 * np.pi * ((c[:, None] * c[None, :]) % FNET_GROUP_DIM) / FNET_GROUP_DIM
    scale = 1.0 / math.sqrt(FNET_GROUP_DIM)
    return np.concatenate([np.cos(ang), np.sin(ang)], axis=0) * scale


def _regroup_matrix(n1):
    rows = SEQ_TILE // n1
    p = np.zeros((SEQ_TILE, SEQ_TILE), np.float32)
    for q in range(n1):
        for j in range(rows):
            p[q * rows + j, j * n1 + q] = 1.0
    return p


def _sequence_dft_table(n1):
    S = n1 * DFT_LEN
    k2 = np.arange(DFT_LEN)
    th1 = 2.0 * np.pi * ((k2[None, :] * np.arange(n1)[:, None]) % S) / S
    th2 = 2.0 * np.pi * ((k2[:, None] * k2[None, :]) % DFT_LEN) / DFT_LEN
    scale = 1.0 / math.sqrt(S)
    c1 = jnp.asarray(np.cos(th1) * scale, F32)[:, :, None]
    s1 = jnp.asarray(np.sin(th1) * scale, F32)[:, :, None]
    c2 = jnp.asarray(np.cos(th2), F32)[None]
    s2 = jnp.asarray(np.sin(th2), F32)[None]
    cos = c1 * c2 - s1 * s2
    msin = -(s1 * c2 + c1 * s2)
    return jnp.concatenate([cos, msin], axis=1).astype(BF16)


def _trunk(x, mod, wts):
    B, S, _ = x.shape
    assert S % DFT_LEN == 0
    n1 = S // DFT_LEN
    assert n1 & (n1 - 1) == 0 and SEQ_TILE % n1 == 0 and S % PRE_TILE == 0 and S % MAIN_TILE == 0
    perm = jnp.asarray(_regroup_matrix(n1), F32).astype(BF16)
    a_perm = _pre_call(x, mod, wts["norm_g"], wts["w_in"], perm, n1)
    ya = _fourier_call(a_perm, _sequence_dft_table(n1), wts["e_tab"], wts["b_fmix"])
    return _main_call(x, mod, ya, wts["norm_g"], wts["w_in"], wts["ln_g"], wts["ln_b"],
                      wts["w_s"], wts["bs_full"], wts["w_pa"], wts["w_pb"], wts["w_out"],
                      wts["final_g"], MAIN_TILE)


def kernel(x_prompt, x_sample, c_prompt, c_sample, norm_g, w_ada, b_ada, w_in, w_fmix, b_fmix,
           sgu_ln_g, sgu_ln_b, w_s, b_s, w_pa, w_pb, w_out, final_g):
    depth = norm_g.shape[0]
    assert depth == 1
    l = 0
    nb_p = x_prompt.shape[0]
    mod, e_tab = _modulation(jnp.concatenate([c_prompt, c_sample], axis=0), w_ada[l], b_ada[l],
                             jnp.asarray(_channel_dft_table(), F32), w_fmix[l])
    mod = mod.reshape(-1, 3, D_MODEL)
    wts = dict(
        norm_g=norm_g[l].reshape(1, -1),
        w_in=w_in[l].astype(BF16),
        e_tab=e_tab,
        b_fmix=b_fmix[l].reshape(1, -1),
        ln_g=sgu_ln_g[l].reshape(1, -1),
        ln_b=sgu_ln_b[l].reshape(1, -1),
        w_s=w_s[l].astype(BF16),
        bs_full=jnp.repeat(b_s[l].T, SGU_HEAD_DIM, axis=1),
        w_pa=w_pa[l].astype(BF16),
        w_pb=w_pb[l].astype(BF16),
        w_out=w_out[l].astype(BF16),
        final_g=final_g.reshape(1, -1),
    )
    y_prompt = _trunk(x_prompt, mod[:nb_p], wts)
    y_sample = _trunk(x_sample, mod[nb_p:], wts)
    return (y_prompt, y_sample)
```

```python
import functools
import math

import numpy as np
import jax
import jax.numpy as jnp
from jax import lax
from jax.experimental import pallas as pl
from jax.experimental.pallas import tpu as pltpu

D_MODEL = 1024
FNET_WIDTH = 512
FNET_GROUPS = 4
FNET_GROUP_DIM = 128
SGU_WIDTH = 512
SGU_HEADS = 4
SGU_HEAD_DIM = 128
CHUNK = 128
EPS = 1e-6
PREP_STEPS = 4
DFT_LEN = 256
SEQ_TILE = 256
PRE_TILE = 2048
MAIN_TILE = 1024
MAIN_SUB = 256
FFT_WIDTH = 256

_GA = (512, 1024)
_U = (1024, 1536)
_V = (1536, 2048)
_GB = (2048, 2560)
_MA = (2560, 3584)
_MB = (3584, 4608)
IN_WIDTH = 4608

BF16 = jnp.bfloat16
F32 = jnp.float32

VMEM_LIMIT_BYTES = 56 * 1024 * 1024


def _dot(a, b):
    return jnp.dot(a, b, preferred_element_type=F32)


def _silu(x):
    return x * jax.nn.sigmoid(x)


def _split_bf16(x):
    hi = x.astype(BF16)
    return hi, (x - hi.astype(F32)).astype(BF16)


def _dot_3pass(a, b):
    a_hi, a_lo = _split_bf16(a)
    b_hi, b_lo = _split_bf16(b)
    return _dot(a_hi, b_hi) + (_dot(a_lo, b_hi) + _dot(a_hi, b_lo))


def _prep_kernel(c_ref, wada_ref, bada_ref, cs_ref, wf_ref, ws_ref, win_ref, wpa_ref, wpb_ref, wo_ref,
                 mod_ref, e_ref, ws_o, win_o, wpa_o, wpb_o, wo_o):
    mod_ref[...] = _dot_3pass(_silu(c_ref[...]), wada_ref[...]) + bada_ref[...]
    win_o[...] = win_ref[...].astype(BF16)
    wpa_o[...] = wpa_ref[...].astype(BF16)
    wpb_o[...] = wpb_ref[...].astype(BF16)
    wo_o[...] = wo_ref[...].astype(BF16)

    @pl.when(pl.program_id(0) == 0)
    def _():
        ws_o[...] = ws_ref[...].astype(BF16)
        for g in range(FNET_GROUPS):
            e_ref[g] = _dot_3pass(cs_ref[...], wf_ref[g]).astype(BF16)


def _prepare(c, w_ada, b_ada, cs_c, w_fmix, w_s, w_in, w_pa, w_pb, w_out):
    n = c.shape[0]
    e_shape = (FNET_GROUPS, 2 * FNET_GROUP_DIM, FNET_GROUP_DIM)
    whole = lambda shape: pl.BlockSpec(shape, lambda j: (0,) * len(shape))
    cols = lambda a: pl.BlockSpec((a.shape[0], a.shape[1] // PREP_STEPS), lambda j: (0, j))
    streamed = (w_in, w_pa, w_pb, w_out)
    assert all(a.shape[1] % (PREP_STEPS * FNET_GROUP_DIM) == 0 for a in (w_ada,) + streamed)
    return pl.pallas_call(
        _prep_kernel,
        grid=(PREP_STEPS,),
        in_specs=[whole(c.shape), cols(w_ada), pl.BlockSpec((1, w_ada.shape[1] // PREP_STEPS), lambda j: (0, j)),
                  whole(cs_c.shape), whole(w_fmix.shape), whole(w_s.shape)] + [cols(a) for a in streamed],
        out_specs=[pl.BlockSpec((n, w_ada.shape[1] // PREP_STEPS), lambda j: (0, j)),
                   whole(e_shape), whole(w_s.shape)] + [cols(a) for a in streamed],
        out_shape=[jax.ShapeDtypeStruct((n, w_ada.shape[1]), F32), jax.ShapeDtypeStruct(e_shape, BF16),
                   jax.ShapeDtypeStruct(w_s.shape, BF16)] + [jax.ShapeDtypeStruct(a.shape, BF16) for a in streamed],
        compiler_params=pltpu.CompilerParams(
            dimension_semantics=("arbitrary",),
            vmem_limit_bytes=VMEM_LIMIT_BYTES),
        name="adaln_prep",
    )(c, w_ada, b_ada.reshape(1, -1), cs_c, w_fmix, w_s, *streamed)


def _modulated_norm(x, mod_ref, g_ref):
    ms = jnp.mean(x * x, axis=-1, keepdims=True)
    xn = x * lax.rsqrt(ms + EPS) * g_ref[...]
    shift = mod_ref[0:1, :]
    scale = mod_ref[1:2, :]
    return xn * (1.0 + scale) + shift


def _pre_kernel(x_ref, mod_ref, g_ref, win_ref, perm_ref, a_ref, *, n1):
    h = _modulated_norm(x_ref[...], mod_ref, g_ref).astype(BF16)
    a = _dot(h, win_ref[...]).astype(BF16)
    rows = SEQ_TILE // n1
    for s in range(PRE_TILE // SEQ_TILE):
        ap = _dot(perm_ref[...], a[s * SEQ_TILE:(s + 1) * SEQ_TILE]).astype(BF16)
        for q in range(n1):
            a_ref[q, s * rows:(s + 1) * rows, :] = ap[q * rows:(q + 1) * rows, :]


def _pre_call(x, mod, norm_g, w_in, perm, n1):
    B, S, _ = x.shape
    rows = PRE_TILE // n1
    return pl.pallas_call(
        functools.partial(_pre_kernel, n1=n1),
        grid=(B, S // PRE_TILE),
        in_specs=[
            pl.BlockSpec((None, PRE_TILE, D_MODEL), lambda b, i: (b, i, 0)),
            pl.BlockSpec((None, 3, D_MODEL), lambda b, i: (b, 0, 0)),
            pl.BlockSpec((1, D_MODEL), lambda b, i: (0, 0)),
            pl.BlockSpec((D_MODEL, FNET_WIDTH), lambda b, i: (0, 0)),
            pl.BlockSpec((SEQ_TILE, SEQ_TILE), lambda b, i: (0, 0)),
        ],
        out_specs=pl.BlockSpec((None, n1, rows, FNET_WIDTH), lambda b, i: (b, 0, i, 0)),
        out_shape=jax.ShapeDtypeStruct((B, n1, DFT_LEN, FNET_WIDTH), BF16),
        compiler_params=pltpu.CompilerParams(
            dimension_semantics=("arbitrary", "arbitrary"),
            vmem_limit_bytes=VMEM_LIMIT_BYTES),
        name="fnet_pre",
    )(x, mod, norm_g, w_in, perm)


def _cmul_const(z, w):
    zr, zi = z
    wr, wi = w.real, w.imag
    tol = 1e-12
    if abs(wi) < tol:
        return (zr, zi) if abs(wr - 1) < tol else (zr * wr, zi * wr)
    if abs(wr) < tol:
        if abs(wi + 1) < tol:
            return (zi, -zr)
        if abs(wi - 1) < tol:
            return (-zi, zr)
    if abs(abs(wr) - abs(wi)) < tol:
        c = abs(wr)
        sr, si = math.copysign(1.0, wr), math.copysign(1.0, wi)
        re = (zr if sr > 0 else -zr) - (zi if si > 0 else -zi)
        im = (zi if sr > 0 else -zi) + (zr if si > 0 else -zr)
        return (re * c, im * c)
    return (zr * wr - zi * wi, zr * wi + zi * wr)


def _block_fft(zs):
    n = len(zs)
    if n == 1:
        return zs
    even = _block_fft(zs[0::2])
    odd = _block_fft(zs[1::2])
    out = [None] * n
    for k in range(n // 2):
        w = complex(math.cos(2 * math.pi * k / n), -math.sin(2 * math.pi * k / n))
        tr, ti = _cmul_const(odd[k], w)
        er, ei = even[k]
        out[k] = (er + tr, ei + ti)
        out[k + n // 2] = (er - tr, ei - ti)
    return out


def _run_skewed(stages, states):
    for t in range(len(stages) + len(states) - 1):
        for i, st in enumerate(states):
            if 0 <= t - i < len(stages):
                stages[t - i](st)


def _fourier_kernel(a_ref, m_ref, e_ref, bf_ref, o_ref, *, n1):
    def block_dfts(st):
        st["g"] = []
        for q in range(n1):
            g = _dot(m_ref[q], a_ref[q, :, st["cols"]])
            st["g"].append((g[:DFT_LEN], g[DFT_LEN:]))

    def across_blocks(st):
        st["u"] = _block_fft(st.pop("g"))

    def channel_map(st):
        us = st.pop("u")
        for g in st["groups"]:
            lo, hi = g * FNET_GROUP_DIM - st["cols"].start, (g + 1) * FNET_GROUP_DIM - st["cols"].start
            z = jnp.concatenate(
                [jnp.concatenate([ur[:, lo:hi], ui[:, lo:hi]], axis=1).astype(BF16) for ur, ui in us], axis=0)
            out = slice(g * FNET_GROUP_DIM, (g + 1) * FNET_GROUP_DIM)
            o_ref[:, out] = _dot(z, e_ref[g]) + bf_ref[:, out]

    halves = [dict(cols=slice(j * FFT_WIDTH, (j + 1) * FFT_WIDTH),
                   groups=range(j * FFT_WIDTH // FNET_GROUP_DIM, (j + 1) * FFT_WIDTH // FNET_GROUP_DIM))
              for j in range(FNET_WIDTH // FFT_WIDTH)]
    _run_skewed((block_dfts, across_blocks, channel_map), halves)


def _fourier_call(a_perm, m_tab, e_tab, b_fmix):
    B, n1, _, _ = a_perm.shape
    S = n1 * DFT_LEN
    return pl.pallas_call(
        functools.partial(_fourier_kernel, n1=n1),
        grid=(B,),
        in_specs=[
            pl.BlockSpec((None, n1, DFT_LEN, FNET_WIDTH), lambda b: (b, 0, 0, 0)),
            pl.BlockSpec((n1, 2 * DFT_LEN, DFT_LEN), lambda b: (0, 0, 0)),
            pl.BlockSpec((FNET_GROUPS, 2 * FNET_GROUP_DIM, FNET_GROUP_DIM), lambda b: (0, 0, 0)),
            pl.BlockSpec((1, FNET_WIDTH), lambda b: (0, 0)),
        ],
        out_specs=pl.BlockSpec((None, S, FNET_WIDTH), lambda b: (b, 0, 0)),
        out_shape=jax.ShapeDtypeStruct((B, S, FNET_WIDTH), F32),
        compiler_params=pltpu.CompilerParams(
            dimension_semantics=("arbitrary",),
            vmem_limit_bytes=VMEM_LIMIT_BYTES),
        name="fnet_fft",
    )(a_perm, m_tab, e_tab, b_fmix)


def _main_kernel(x_ref, mod_ref, ya_ref, g_ref, win_ref, lng_ref, lnb_ref, ws_ref, bs_ref,
                 wpa_ref, wpb_ref, wo_ref, fg_ref, o_ref, *, tm):
    def proj(st, rng):
        return _dot(st["h"], win_ref[:, rng[0]:rng[1]])

    def load_norm(st):
        st["x"] = x_ref[st["rows"], :]
        st["h"] = _modulated_norm(st["x"], mod_ref, g_ref).astype(BF16)

    def project(st):
        st["v"] = proj(st, _V)
        st["ga"] = proj(st, _GA)
        st["u"] = proj(st, _U)
        st["gb"] = proj(st, _GB)
        st["ma"] = proj(st, _MA)

    def layernorm(st):
        v = st.pop("v")
        mu = jnp.mean(v, axis=-1, keepdims=True)
        vc = v - mu
        var = jnp.mean(vc * vc, axis=-1, keepdims=True)
        st["vln"] = (vc * lax.rsqrt(var + EPS) * lng_ref[...] + lnb_ref[...]).astype(BF16)

    def spatial_mix(st):
        vln = st.pop("vln")
        rows = []
        for c in range(MAIN_SUB // CHUNK):
            r0 = c * CHUNK
            cols = []
            for hd in range(SGU_HEADS):
                lo, hi = hd * SGU_HEAD_DIM, (hd + 1) * SGU_HEAD_DIM
                cols.append(_dot(ws_ref[hd], vln[r0:r0 + CHUNK, lo:hi]))
            rows.append(jnp.concatenate(cols, axis=1) + bs_ref[...])
        st["mixed"] = jnp.concatenate(rows, axis=0)

    def fourier_path(st):
        y_a = (ya_ref[st["rows"], :] * _silu(st.pop("ga"))).astype(BF16)
        st["merged"] = jax.nn.sigmoid(st.pop("ma")) * _dot(y_a, wpa_ref[...])
        st["mb"] = proj(st, _MB)

    def gating_path(st):
        y_b = (st.pop("u") * st.pop("mixed") * _silu(st.pop("gb"))).astype(BF16)
        st["merged"] = st["merged"] + jax.nn.sigmoid(st.pop("mb")) * _dot(y_b, wpb_ref[...])

    def out_proj(st):
        st["out"] = _dot(st.pop("merged").astype(BF16), wo_ref[...])

    def finish(st):
        xo = st.pop("x") + mod_ref[2:3, :] * st.pop("out")
        ms = jnp.mean(xo * xo, axis=-1, keepdims=True)
        o_ref[st["rows"], :] = xo * lax.rsqrt(ms + EPS) * fg_ref[...]

    stages = (load_norm, project, layernorm, spatial_mix, fourier_path, gating_path, out_proj, finish)
    subs = [dict(rows=slice(i * MAIN_SUB, (i + 1) * MAIN_SUB)) for i in range(tm // MAIN_SUB)]
    _run_skewed(stages, subs)


def _main_call(x, mod, ya, norm_g, w_in, ln_g, ln_b, w_s, bs_full, w_pa, w_pb, w_out,
               final_g, tm):
    B, S, _ = x.shape
    const2 = lambda b, i: (0, 0)
    return pl.pallas_call(
        functools.partial(_main_kernel, tm=tm),
        grid=(B, S // tm),
        in_specs=[
            pl.BlockSpec((None, tm, D_MODEL), lambda b, i: (b, i, 0)),
            pl.BlockSpec((None, 3, D_MODEL), lambda b, i: (b, 0, 0)),
            pl.BlockSpec((None, tm, FNET_WIDTH), lambda b, i: (b, i, 0)),
            pl.BlockSpec((1, D_MODEL), const2),
            pl.BlockSpec((D_MODEL, IN_WIDTH), const2, pipeline_mode=pl.Buffered(1)),
            pl.BlockSpec((1, SGU_WIDTH), const2),
            pl.BlockSpec((1, SGU_WIDTH), const2),
            pl.BlockSpec((SGU_HEADS, CHUNK, CHUNK), lambda b, i: (0, 0, 0)),
            pl.BlockSpec((CHUNK, SGU_WIDTH), const2),
            pl.BlockSpec((FNET_WIDTH, D_MODEL), const2, pipeline_mode=pl.Buffered(1)),
            pl.BlockSpec((SGU_WIDTH, D_MODEL), const2, pipeline_mode=pl.Buffered(1)),
            pl.BlockSpec((D_MODEL, D_MODEL), const2, pipeline_mode=pl.Buffered(1)),
            pl.BlockSpec((1, D_MODEL), const2),
        ],
        out_specs=pl.BlockSpec((None, tm, D_MODEL), lambda b, i: (b, i, 0)),
        out_shape=jax.ShapeDtypeStruct((B, S, D_MODEL), F32),
        compiler_params=pltpu.CompilerParams(
            dimension_semantics=("arbitrary", "arbitrary"),
            vmem_limit_bytes=VMEM_LIMIT_BYTES),
        name="encoder_main",
    )(x, mod, ya, norm_g, w_in, ln_g, ln_b, w_s, bs_full, w_pa, w_pb, w_out, final_g)


def _channel_dft_table():
    c = np.arange(FNET_GROUP_DIM)
    ang = 2.0 * np.pi * ((c[:, None] * c[None, :]) % FNET_GROUP_DIM) / FNET_GROUP_DIM
    scale = 1.0 / math.sqrt(FNET_GROUP_DIM)
    return np.concatenate([np.cos(ang), np.sin(ang)], axis=0) * scale


def _regroup_matrix(n1):
    rows = SEQ_TILE // n1
    p = np.zeros((SEQ_TILE, SEQ_TILE), np.float32)
    for q in range(n1):
        for j in range(rows):
            p[q * rows + j, j * n1 + q] = 1.0
    return p


def _sequence_dft_table(n1):
    S = n1 * DFT_LEN
    k2 = np.arange(DFT_LEN)
    th1 = 2.0 * np.pi * ((k2[None, :] * np.arange(n1)[:, None]) % S) / S
    th2 = 2.0 * np.pi * ((k2[:, None] * k2[None, :]) % DFT_LEN) / DFT_LEN
    scale = 1.0 / math.sqrt(S)
    c1 = jnp.asarray(np.cos(th1) * scale, F32)[:, :, None]
    s1 = jnp.asarray(np.sin(th1) * scale, F32)[:, :, None]
    c2 = jnp.asarray(np.cos(th2), F32)[None]
    s2 = jnp.asarray(np.sin(th2), F32)[None]
    cos = c1 * c2 - s1 * s2
    msin = -(s1 * c2 + c1 * s2)
    return jnp.concatenate([cos, msin], axis=1).astype(BF16)


def _trunk(x, mod, wts):
    B, S, _ = x.shape
    assert S % DFT_LEN == 0
    n1 = S // DFT_LEN
    assert n1 & (n1 - 1) == 0 and SEQ_TILE % n1 == 0 and S % PRE_TILE == 0 and S % MAIN_TILE == 0
    perm = jnp.asarray(_regroup_matrix(n1), F32).astype(BF16)
    a_perm = _pre_call(x, mod, wts["norm_g"], wts["w_in"], perm, n1)
    ya = _fourier_call(a_perm, _sequence_dft_table(n1), wts["e_tab"], wts["b_fmix"])
    return _main_call(x, mod, ya, wts["norm_g"], wts["w_in"], wts["ln_g"], wts["ln_b"],
                      wts["w_s"], wts["bs_full"], wts["w_pa"], wts["w_pb"], wts["w_out"],
                      wts["final_g"], MAIN_TILE)


def kernel(x_prompt, x_sample, c_prompt, c_sample, norm_g, w_ada, b_ada, w_in, w_fmix, b_fmix,
           sgu_ln_g, sgu_ln_b, w_s, b_s, w_pa, w_pb, w_out, final_g):
    depth = norm_g.shape[0]
    assert depth == 1
    l = 0
    nb_p = x_prompt.shape[0]
    mod, e_tab, w_s_bf, w_in_bf, w_pa_bf, w_pb_bf, w_out_bf = _prepare(
        jnp.concatenate([c_prompt, c_sample], axis=0), w_ada[l], b_ada[l],
        jnp.asarray(_channel_dft_table(), F32), w_fmix[l], w_s[l], w_in[l], w_pa[l], w_pb[l], w_out[l])
    mod = mod.reshape(-1, 3, D_MODEL)
    wts = dict(
        norm_g=norm_g[l].reshape(1, -1),
        w_in=w_in_bf,
        e_tab=e_tab,
        b_fmix=b_fmix[l].reshape(1, -1),
        ln_g=sgu_ln_g[l].reshape(1, -1),
        ln_b=sgu_ln_b[l].reshape(1, -1),
        w_s=w_s_bf,
        bs_full=jnp.repeat(b_s[l].T, SGU_HEAD_DIM, axis=1),
        w_pa=w_pa_bf,
        w_pb=w_pb_bf,
        w_out=w_out_bf,
        final_g=final_g.reshape(1, -1),
    )
    y_prompt = _trunk(x_prompt, mod[:nb_p], wts)
    y_sample = _trunk(x_sample, mod[nb_p:], wts)
    return (y_prompt, y_sample)
```

```python
import functools
import math

import numpy as np
import jax
import jax.numpy as jnp
from jax import lax
from jax.experimental import pallas as pl
from jax.experimental.pallas import tpu as pltpu

D_MODEL = 1024
FNET_WIDTH = 512
FNET_GROUPS = 4
FNET_GROUP_DIM = 128
SGU_WIDTH = 512
SGU_HEADS = 4
SGU_HEAD_DIM = 128
CHUNK = 128
EPS = 1e-6
PREP_STEPS = 4
DFT_LEN = 256
SEQ_TILE = 256
PRE_TILE = 2048
MAIN_TILE = 1024
MAIN_SUB = 256
FFT_WIDTH = 256

_GA = (512, 1024)
_U = (1024, 1536)
_V = (1536, 2048)
_GB = (2048, 2560)
_MA = (2560, 3584)
_MB = (3584, 4608)
IN_WIDTH = 4608

BF16 = jnp.bfloat16
F32 = jnp.float32

VMEM_LIMIT_BYTES = 56 * 1024 * 1024


def _dot(a, b):
    return jnp.dot(a, b, preferred_element_type=F32)


def _silu(x):
    return x * jax.nn.sigmoid(x)


def _split_bf16(x):
    hi = x.astype(BF16)
    return hi, (x - hi.astype(F32)).astype(BF16)


def _dot_3pass(a, b):
    a_hi, a_lo = _split_bf16(a)
    b_hi, b_lo = _split_bf16(b)
    return _dot(a_hi, b_hi) + (_dot(a_lo, b_hi) + _dot(a_hi, b_lo))


def _prep_kernel(c_ref, wada_ref, bada_ref, cs_ref, wf_ref, ws_ref, win_ref, wpa_ref, wpb_ref, wo_ref,
                 mod_ref, e_ref, ws_o, win_o, wpa_o, wpb_o, wo_o):
    mod_ref[...] = _dot_3pass(_silu(c_ref[...]), wada_ref[...]) + bada_ref[...]
    win_o[...] = win_ref[...].astype(BF16)
    wpa_o[...] = wpa_ref[...].astype(BF16)
    wpb_o[...] = wpb_ref[...].astype(BF16)
    wo_o[...] = wo_ref[...].astype(BF16)

    @pl.when(pl.program_id(0) == 0)
    def _():
        ws_o[...] = ws_ref[...].astype(BF16)
        for g in range(FNET_GROUPS):
            e_ref[g] = _dot_3pass(cs_ref[...], wf_ref[g]).astype(BF16)


def _prepare(c, w_ada, b_ada, cs_c, w_fmix, w_s, w_in, w_pa, w_pb, w_out):
    n = c.shape[0]
    e_shape = (FNET_GROUPS, 2 * FNET_GROUP_DIM, FNET_GROUP_DIM)
    whole = lambda shape: pl.BlockSpec(shape, lambda j: (0,) * len(shape))
    cols = lambda a: pl.BlockSpec((a.shape[0], a.shape[1] // PREP_STEPS), lambda j: (0, j))
    streamed = (w_in, w_pa, w_pb, w_out)
    assert all(a.shape[1] % (PREP_STEPS * FNET_GROUP_DIM) == 0 for a in (w_ada,) + streamed)
    return pl.pallas_call(
        _prep_kernel,
        grid=(PREP_STEPS,),
        in_specs=[whole(c.shape), cols(w_ada), pl.BlockSpec((1, w_ada.shape[1] // PREP_STEPS), lambda j: (0, j)),
                  whole(cs_c.shape), whole(w_fmix.shape), whole(w_s.shape)] + [cols(a) for a in streamed],
        out_specs=[pl.BlockSpec((n, w_ada.shape[1] // PREP_STEPS), lambda j: (0, j)),
                   whole(e_shape), whole(w_s.shape)] + [cols(a) for a in streamed],
        out_shape=[jax.ShapeDtypeStruct((n, w_ada.shape[1]), F32), jax.ShapeDtypeStruct(e_shape, BF16),
                   jax.ShapeDtypeStruct(w_s.shape, BF16)] + [jax.ShapeDtypeStruct(a.shape, BF16) for a in streamed],
        compiler_params=pltpu.CompilerParams(
            dimension_semantics=("arbitrary",),
            vmem_limit_bytes=VMEM_LIMIT_BYTES),
        name="adaln_prep",
    )(c, w_ada, b_ada.reshape(1, -1), cs_c, w_fmix, w_s, *streamed)


def _modulated_norm(x, mod_ref, g_ref):
    ms = jnp.mean(x * x, axis=-1, keepdims=True)
    xn = x * lax.rsqrt(ms + EPS) * g_ref[...]
    shift = mod_ref[0:1, :]
    scale = mod_ref[1:2, :]
    return xn * (1.0 + scale) + shift


def _pre_kernel(x_ref, mod_ref, g_ref, win_ref, perm_ref, a_ref, *, n1):
    h = _modulated_norm(x_ref[...], mod_ref, g_ref).astype(BF16)
    a = _dot(h, win_ref[...]).astype(BF16)
    rows = SEQ_TILE // n1
    for s in range(PRE_TILE // SEQ_TILE):
        ap = _dot(perm_ref[...], a[s * SEQ_TILE:(s + 1) * SEQ_TILE]).astype(BF16)
        for q in range(n1):
            a_ref[q, s * rows:(s + 1) * rows, :] = ap[q * rows:(q + 1) * rows, :]


def _pre_call(x, mod, norm_g, w_in, perm, n1):
    B, S, _ = x.shape
    rows = PRE_TILE // n1
    return pl.pallas_call(
        functools.partial(_pre_kernel, n1=n1),
        grid=(B, S // PRE_TILE),
        in_specs=[
            pl.BlockSpec((None, PRE_TILE, D_MODEL), lambda b, i: (b, i, 0)),
            pl.BlockSpec((None, 3, D_MODEL), lambda b, i: (b, 0, 0)),
            pl.BlockSpec((1, D_MODEL), lambda b, i: (0, 0)),
            pl.BlockSpec((D_MODEL, FNET_WIDTH), lambda b, i: (0, 0)),
            pl.BlockSpec((SEQ_TILE, SEQ_TILE), lambda b, i: (0, 0)),
        ],
        out_specs=pl.BlockSpec((None, n1, rows, FNET_WIDTH), lambda b, i: (b, 0, i, 0)),
        out_shape=jax.ShapeDtypeStruct((B, n1, DFT_LEN, FNET_WIDTH), BF16),
        compiler_params=pltpu.CompilerParams(
            dimension_semantics=("arbitrary", "arbitrary"),
            vmem_limit_bytes=VMEM_LIMIT_BYTES),
        name="fnet_pre",
    )(x, mod, norm_g, w_in, perm)


def _cmul_const(z, w):
    zr, zi = z
    wr, wi = w.real, w.imag
    tol = 1e-12
    if abs(wi) < tol:
        return (zr, zi) if abs(wr - 1) < tol else (zr * wr, zi * wr)
    if abs(wr) < tol:
        if abs(wi + 1) < tol:
            return (zi, -zr)
        if abs(wi - 1) < tol:
            return (-zi, zr)
    if abs(abs(wr) - abs(wi)) < tol:
        c = abs(wr)
        sr, si = math.copysign(1.0, wr), math.copysign(1.0, wi)
        re = (zr if sr > 0 else -zr) - (zi if si > 0 else -zi)
        im = (zi if sr > 0 else -zi) + (zr if si > 0 else -zr)
        return (re * c, im * c)
    return (zr * wr - zi * wi, zr * wi + zi * wr)


def _block_fft(zs):
    n = len(zs)
    if n == 1:
        return zs
    even = _block_fft(zs[0::2])
    odd = _block_fft(zs[1::2])
    out = [None] * n
    for k in range(n // 2):
        w = complex(math.cos(2 * math.pi * k / n), -math.sin(2 * math.pi * k / n))
        tr, ti = _cmul_const(odd[k], w)
        er, ei = even[k]
        out[k] = (er + tr, ei + ti)
        out[k + n // 2] = (er - tr, ei - ti)
    return out


def _run_skewed(stages, states):
    for t in range(len(stages) + len(states) - 1):
        for i, st in enumerate(states):
            if 0 <= t - i < len(stages):
                stages[t - i](st)


def _fourier_kernel(a_ref, m_ref, e_ref, bf_ref, o_ref, *, n1):
    def block_dfts(st):
        st["g"] = []
        for q in range(n1):
            g = _dot(m_ref[q], a_ref[q, :, st["cols"]])
            st["g"].append((g[:DFT_LEN], g[DFT_LEN:]))

    def across_blocks(st):
        st["u"] = _block_fft(st.pop("g"))

    def channel_map(st):
        us = st.pop("u")
        for g in st["groups"]:
            lo, hi = g * FNET_GROUP_DIM - st["cols"].start, (g + 1) * FNET_GROUP_DIM - st["cols"].start
            z = jnp.concatenate(
                [jnp.concatenate([ur[:, lo:hi], ui[:, lo:hi]], axis=1).astype(BF16) for ur, ui in us], axis=0)
            out = slice(g * FNET_GROUP_DIM, (g + 1) * FNET_GROUP_DIM)
            o_ref[:, out] = _dot(z, e_ref[g]) + bf_ref[:, out]

    halves = [dict(cols=slice(j * FFT_WIDTH, (j + 1) * FFT_WIDTH),
                   groups=range(j * FFT_WIDTH // FNET_GROUP_DIM, (j + 1) * FFT_WIDTH // FNET_GROUP_DIM))
              for j in range(FNET_WIDTH // FFT_WIDTH)]
    _run_skewed((block_dfts, across_blocks, channel_map), halves)


def _fourier_call(a_perm, m_tab, e_tab, b_fmix):
    B, n1, _, _ = a_perm.shape
    S = n1 * DFT_LEN
    return pl.pallas_call(
        functools.partial(_fourier_kernel, n1=n1),
        grid=(B,),
        in_specs=[
            pl.BlockSpec((None, n1, DFT_LEN, FNET_WIDTH), lambda b: (b, 0, 0, 0)),
            pl.BlockSpec((n1, 2 * DFT_LEN, DFT_LEN), lambda b: (0, 0, 0)),
            pl.BlockSpec((FNET_GROUPS, 2 * FNET_GROUP_DIM, FNET_GROUP_DIM), lambda b: (0, 0, 0)),
            pl.BlockSpec((1, FNET_WIDTH), lambda b: (0, 0)),
        ],
        out_specs=pl.BlockSpec((None, S, FNET_WIDTH), lambda b: (b, 0, 0)),
        out_shape=jax.ShapeDtypeStruct((B, S, FNET_WIDTH), F32),
        compiler_params=pltpu.CompilerParams(
            dimension_semantics=("arbitrary",),
            vmem_limit_bytes=VMEM_LIMIT_BYTES),
        name="fnet_fft",
    )(a_perm, m_tab, e_tab, b_fmix)


def _main_kernel(x_ref, mod_ref, ya_ref, g_ref, win_ref, lng_ref, lnb_ref, ws_ref, bs_ref,
                 wpa_ref, wpb_ref, wo_ref, fg_ref, o_ref, *, tm):
    def proj(st, rng):
        return _dot(st["h"], win_ref[:, rng[0]:rng[1]])

    def load_norm(st):
        st["x"] = x_ref[st["rows"], :]
        st["h"] = _modulated_norm(st["x"], mod_ref, g_ref).astype(BF16)

    def project(st):
        st["v"] = proj(st, _V)
        st["ga"] = proj(st, _GA)
        st["ma"] = proj(st, _MA)

    def project2(st):
        st["u"] = proj(st, _U)
        st["gb"] = proj(st, _GB)

    def layernorm(st):
        v = st.pop("v")
        mu = jnp.mean(v, axis=-1, keepdims=True)
        vc = v - mu
        var = jnp.mean(vc * vc, axis=-1, keepdims=True)
        st["vln"] = (vc * lax.rsqrt(var + EPS) * lng_ref[...] + lnb_ref[...]).astype(BF16)

    def spatial_mix(st):
        vln = st.pop("vln")
        rows = []
        for c in range(MAIN_SUB // CHUNK):
            r0 = c * CHUNK
            cols = []
            for hd in range(SGU_HEADS):
                lo, hi = hd * SGU_HEAD_DIM, (hd + 1) * SGU_HEAD_DIM
                cols.append(_dot(ws_ref[hd], vln[r0:r0 + CHUNK, lo:hi]))
            rows.append(jnp.concatenate(cols, axis=1) + bs_ref[...])
        st["mixed"] = jnp.concatenate(rows, axis=0)

    def fourier_path(st):
        y_a = (ya_ref[st["rows"], :] * _silu(st.pop("ga"))).astype(BF16)
        st["merged"] = jax.nn.sigmoid(st.pop("ma")) * _dot(y_a, wpa_ref[...])
        st["mb"] = proj(st, _MB)

    def gating_path(st):
        y_b = (st.pop("u") * st.pop("mixed") * _silu(st.pop("gb"))).astype(BF16)
        st["merged"] = st["merged"] + jax.nn.sigmoid(st.pop("mb")) * _dot(y_b, wpb_ref[...])

    def out_proj(st):
        st["out"] = _dot(st.pop("merged").astype(BF16), wo_ref[...])

    def residual(st):
        st["xo"] = st.pop("x") + mod_ref[2:3, :] * st.pop("out")
        st["ms"] = jnp.mean(st["xo"] * st["xo"], axis=-1, keepdims=True)

    def finish(st):
        o_ref[st["rows"], :] = st.pop("xo") * lax.rsqrt(st.pop("ms") + EPS) * fg_ref[...]

    stages = (load_norm, project, project2, layernorm, spatial_mix, fourier_path, gating_path, out_proj, residual, finish)
    subs = [dict(rows=slice(i * MAIN_SUB, (i + 1) * MAIN_SUB)) for i in range(tm // MAIN_SUB)]
    _run_skewed(stages, subs)


def _main_call(x, mod, ya, norm_g, w_in, ln_g, ln_b, w_s, bs_full, w_pa, w_pb, w_out,
               final_g, tm):
    B, S, _ = x.shape
    const2 = lambda b, i: (0, 0)
    return pl.pallas_call(
        functools.partial(_main_kernel, tm=tm),
        grid=(B, S // tm),
        in_specs=[
            pl.BlockSpec((None, tm, D_MODEL), lambda b, i: (b, i, 0)),
            pl.BlockSpec((None, 3, D_MODEL), lambda b, i: (b, 0, 0)),
            pl.BlockSpec((None, tm, FNET_WIDTH), lambda b, i: (b, i, 0)),
            pl.BlockSpec((1, D_MODEL), const2),
            pl.BlockSpec((D_MODEL, IN_WIDTH), const2, pipeline_mode=pl.Buffered(1)),
            pl.BlockSpec((1, SGU_WIDTH), const2),
            pl.BlockSpec((1, SGU_WIDTH), const2),
            pl.BlockSpec((SGU_HEADS, CHUNK, CHUNK), lambda b, i: (0, 0, 0)),
            pl.BlockSpec((CHUNK, SGU_WIDTH), const2),
            pl.BlockSpec((FNET_WIDTH, D_MODEL), const2, pipeline_mode=pl.Buffered(1)),
            pl.BlockSpec((SGU_WIDTH, D_MODEL), const2, pipeline_mode=pl.Buffered(1)),
            pl.BlockSpec((D_MODEL, D_MODEL), const2, pipeline_mode=pl.Buffered(1)),
            pl.BlockSpec((1, D_MODEL), const2),
        ],
        out_specs=pl.BlockSpec((None, tm, D_MODEL), lambda b, i: (b, i, 0)),
        out_shape=jax.ShapeDtypeStruct((B, S, D_MODEL), F32),
        compiler_params=pltpu.CompilerParams(
            dimension_semantics=("arbitrary", "arbitrary"),
            vmem_limit_bytes=VMEM_LIMIT_BYTES),
        name="encoder_main",
    )(x, mod, ya, norm_g, w_in, ln_g, ln_b, w_s, bs_full, w_pa, w_pb, w_out, final_g)


def _channel_dft_table():
    c = np.arange(FNET_GROUP_DIM)
    ang = 2.0 * np.pi * ((c[:, None] * c[None, :]) % FNET_GROUP_DIM) / FNET_GROUP_DIM
    scale = 1.0 / math.sqrt(FNET_GROUP_DIM)
    return np.concatenate([np.cos(ang), np.sin(ang)], axis=0) * scale


def _regroup_matrix(n1):
    rows = SEQ_TILE // n1
    p = np.zeros((SEQ_TILE, SEQ_TILE), np.float32)
    for q in range(n1):
        for j in range(rows):
            p[q * rows + j, j * n1 + q] = 1.0
    return p


def _sequence_dft_table(n1):
    S = n1 * DFT_LEN
    k2 = np.arange(DFT_LEN)
    th1 = 2.0 * np.pi * ((k2[None, :] * np.arange(n1)[:, None]) % S) / S
    th2 = 2.0 * np.pi * ((k2[:, None] * k2[None, :]) % DFT_LEN) / DFT_LEN
    scale = 1.0 / math.sqrt(S)
    c1 = jnp.asarray(np.cos(th1) * scale, F32)[:, :, None]
    s1 = jnp.asarray(np.sin(th1) * scale, F32)[:, :, None]
    c2 = jnp.asarray(np.cos(th2), F32)[None]
    s2 = jnp.asarray(np.sin(th2), F32)[None]
    cos = c1 * c2 - s1 * s2
    msin = -(s1 * c2 + c1 * s2)
    return jnp.concatenate([cos, msin], axis=1).astype(BF16)


def _trunk(x, mod, wts):
    B, S, _ = x.shape
    assert S % DFT_LEN == 0
    n1 = S // DFT_LEN
    assert n1 & (n1 - 1) == 0 and SEQ_TILE % n1 == 0 and S % PRE_TILE == 0 and S % MAIN_TILE == 0
    perm = jnp.asarray(_regroup_matrix(n1), F32).astype(BF16)
    a_perm = _pre_call(x, mod, wts["norm_g"], wts["w_in"], perm, n1)
    ya = _fourier_call(a_perm, _sequence_dft_table(n1), wts["e_tab"], wts["b_fmix"])
    return _main_call(x, mod, ya, wts["norm_g"], wts["w_in"], wts["ln_g"], wts["ln_b"],
                      wts["w_s"], wts["bs_full"], wts["w_pa"], wts["w_pb"], wts["w_out"],
                      wts["final_g"], MAIN_TILE)


def kernel(x_prompt, x_sample, c_prompt, c_sample, norm_g, w_ada, b_ada, w_in, w_fmix, b_fmix,
           sgu_ln_g, sgu_ln_b, w_s, b_s, w_pa, w_pb, w_out, final_g):
    depth = norm_g.shape[0]
    assert depth == 1
    l = 0
    nb_p = x_prompt.shape[0]
    mod, e_tab, w_s_bf, w_in_bf, w_pa_bf, w_pb_bf, w_out_bf = _prepare(
        jnp.concatenate([c_prompt, c_sample], axis=0), w_ada[l], b_ada[l],
        jnp.asarray(_channel_dft_table(), F32), w_fmix[l], w_s[l], w_in[l], w_pa[l], w_pb[l], w_out[l])
    mod = mod.reshape(-1, 3, D_MODEL)
    wts = dict(
        norm_g=norm_g[l].reshape(1, -1),
        w_in=w_in_bf,
        e_tab=e_tab,
        b_fmix=b_fmix[l].reshape(1, -1),
        ln_g=sgu_ln_g[l].reshape(1, -1),
        ln_b=sgu_ln_b[l].reshape(1, -1),
        w_s=w_s_bf,
        bs_full=jnp.repeat(b_s[l].T, SGU_HEAD_DIM, axis=1),
        w_pa=w_pa_bf,
        w_pb=w_pb_bf,
        w_out=w_out_bf,
        final_g=final_g.reshape(1, -1),
    )
    y_prompt = _trunk(x_prompt, mod[:nb_p], wts)
    y_sample = _trunk(x_sample, mod[nb_p:], wts)
    return (y_prompt, y_sample)
```

```python
import functools
import math

import numpy as np
import jax
import jax.numpy as jnp
from jax import lax
from jax.experimental import pallas as pl
from jax.experimental.pallas import tpu as pltpu

D_MODEL = 1024
FNET_WIDTH = 512
FNET_GROUPS = 4
FNET_GROUP_DIM = 128
SGU_WIDTH = 512
SGU_HEADS = 4
SGU_HEAD_DIM = 128
CHUNK = 128
EPS = 1e-6
PREP_STEPS = 4
DFT_LEN = 256
SEQ_TILE = 256
PRE_TILE = 2048
MAIN_TILE = 1024
MAIN_SUB = 256
FFT_WIDTH = 256

_GA = (512, 1024)
_U = (1024, 1536)
_V = (1536, 2048)
_GB = (2048, 2560)
_MA = (2560, 3584)
_MB = (3584, 4608)
IN_WIDTH = 4608

BF16 = jnp.bfloat16
F32 = jnp.float32

VMEM_LIMIT_BYTES = 56 * 1024 * 1024


def _dot(a, b):
    return jnp.dot(a, b, preferred_element_type=F32)


def _silu(x):
    return x * jax.nn.sigmoid(x)


def _split_bf16(x):
    hi = x.astype(BF16)
    return hi, (x - hi.astype(F32)).astype(BF16)


def _dot_3pass(a, b):
    a_hi, a_lo = _split_bf16(a)
    b_hi, b_lo = _split_bf16(b)
    return _dot(a_hi, b_hi) + (_dot(a_lo, b_hi) + _dot(a_hi, b_lo))


def _prep_kernel(c_ref, wada_ref, bada_ref, cs_ref, wf_ref, ws_ref, win_ref, wpa_ref, wpb_ref, wo_ref,
                 mod_ref, e_ref, ws_o, win_o, wpa_o, wpb_o, wo_o):
    mod_ref[...] = _dot_3pass(_silu(c_ref[...]), wada_ref[...]) + bada_ref[...]
    win_o[...] = win_ref[...].astype(BF16)
    wpa_o[...] = wpa_ref[...].astype(BF16)
    wpb_o[...] = wpb_ref[...].astype(BF16)
    wo_o[...] = wo_ref[...].astype(BF16)

    @pl.when(pl.program_id(0) == 0)
    def _():
        ws_o[...] = ws_ref[...].astype(BF16)
        for g in range(FNET_GROUPS):
            e_ref[g] = _dot_3pass(cs_ref[...], wf_ref[g]).astype(BF16)


def _prepare(c, w_ada, b_ada, cs_c, w_fmix, w_s, w_in, w_pa, w_pb, w_out):
    n = c.shape[0]
    e_shape = (FNET_GROUPS, 2 * FNET_GROUP_DIM, FNET_GROUP_DIM)
    whole = lambda shape: pl.BlockSpec(shape, lambda j: (0,) * len(shape))
    cols = lambda a: pl.BlockSpec((a.shape[0], a.shape[1] // PREP_STEPS), lambda j: (0, j))
    streamed = (w_in, w_pa, w_pb, w_out)
    assert all(a.shape[1] % (PREP_STEPS * FNET_GROUP_DIM) == 0 for a in (w_ada,) + streamed)
    return pl.pallas_call(
        _prep_kernel,
        grid=(PREP_STEPS,),
        in_specs=[whole(c.shape), cols(w_ada), pl.BlockSpec((1, w_ada.shape[1] // PREP_STEPS), lambda j: (0, j)),
                  whole(cs_c.shape), whole(w_fmix.shape), whole(w_s.shape)] + [cols(a) for a in streamed],
        out_specs=[pl.BlockSpec((n, w_ada.shape[1] // PREP_STEPS), lambda j: (0, j)),
                   whole(e_shape), whole(w_s.shape)] + [cols(a) for a in streamed],
        out_shape=[jax.ShapeDtypeStruct((n, w_ada.shape[1]), F32), jax.ShapeDtypeStruct(e_shape, BF16),
                   jax.ShapeDtypeStruct(w_s.shape, BF16)] + [jax.ShapeDtypeStruct(a.shape, BF16) for a in streamed],
        compiler_params=pltpu.CompilerParams(
            dimension_semantics=("arbitrary",),
            vmem_limit_bytes=VMEM_LIMIT_BYTES),
        name="adaln_prep",
    )(c, w_ada, b_ada.reshape(1, -1), cs_c, w_fmix, w_s, *streamed)


def _modulated_norm(x, mod_ref, g_ref):
    ms = jnp.mean(x * x, axis=-1, keepdims=True)
    xn = x * lax.rsqrt(ms + EPS) * g_ref[...]
    shift = mod_ref[0:1, :]
    scale = mod_ref[1:2, :]
    return xn * (1.0 + scale) + shift


def _regrouped_blocks(a, perm_ref, n1):
    rows = SEQ_TILE // n1
    aps = [_dot(perm_ref[...], a[s * SEQ_TILE:(s + 1) * SEQ_TILE]).astype(BF16)
           for s in range(a.shape[0] // SEQ_TILE)]
    return [jnp.concatenate([ap[q * rows:(q + 1) * rows] for ap in aps], axis=0) for q in range(n1)]


def _fnet_in(x_ref, mod_ref, g_ref, win_ref):
    h = _modulated_norm(x_ref[...], mod_ref, g_ref).astype(BF16)
    return _dot(h, win_ref[...]).astype(BF16)


def _pre_kernel(x_ref, mod_ref, g_ref, win_ref, perm_ref, a_ref, *, n1):
    for q, block in enumerate(_regrouped_blocks(_fnet_in(x_ref, mod_ref, g_ref, win_ref), perm_ref, n1)):
        a_ref[q] = block


def _pre_call(x, mod, norm_g, w_in, perm, n1):
    B, S, _ = x.shape
    rows = PRE_TILE // n1
    return pl.pallas_call(
        functools.partial(_pre_kernel, n1=n1),
        grid=(B, S // PRE_TILE),
        in_specs=[
            pl.BlockSpec((None, PRE_TILE, D_MODEL), lambda b, i: (b, i, 0)),
            pl.BlockSpec((None, 3, D_MODEL), lambda b, i: (b, 0, 0)),
            pl.BlockSpec((1, D_MODEL), lambda b, i: (0, 0)),
            pl.BlockSpec((D_MODEL, FNET_WIDTH), lambda b, i: (0, 0)),
            pl.BlockSpec((SEQ_TILE, SEQ_TILE), lambda b, i: (0, 0)),
        ],
        out_specs=pl.BlockSpec((None, n1, rows, FNET_WIDTH), lambda b, i: (b, 0, i, 0)),
        out_shape=jax.ShapeDtypeStruct((B, n1, DFT_LEN, FNET_WIDTH), BF16),
        compiler_params=pltpu.CompilerParams(
            dimension_semantics=("arbitrary", "arbitrary"),
            vmem_limit_bytes=VMEM_LIMIT_BYTES),
        name="fnet_pre",
    )(x, mod, norm_g, w_in, perm)


def _cmul_const(z, w):
    zr, zi = z
    wr, wi = w.real, w.imag
    tol = 1e-12
    if abs(wi) < tol:
        return (zr, zi) if abs(wr - 1) < tol else (zr * wr, zi * wr)
    if abs(wr) < tol:
        if abs(wi + 1) < tol:
            return (zi, -zr)
        if abs(wi - 1) < tol:
            return (-zi, zr)
    if abs(abs(wr) - abs(wi)) < tol:
        c = abs(wr)
        sr, si = math.copysign(1.0, wr), math.copysign(1.0, wi)
        re = (zr if sr > 0 else -zr) - (zi if si > 0 else -zi)
        im = (zi if sr > 0 else -zi) + (zr if si > 0 else -zr)
        return (re * c, im * c)
    return (zr * wr - zi * wi, zr * wi + zi * wr)


def _block_fft(zs):
    n = len(zs)
    if n == 1:
        return zs
    even = _block_fft(zs[0::2])
    odd = _block_fft(zs[1::2])
    out = [None] * n
    for k in range(n // 2):
        w = complex(math.cos(2 * math.pi * k / n), -math.sin(2 * math.pi * k / n))
        tr, ti = _cmul_const(odd[k], w)
        er, ei = even[k]
        out[k] = (er + tr, ei + ti)
        out[k + n // 2] = (er - tr, ei - ti)
    return out


def _run_skewed(stages, states):
    for t in range(len(stages) + len(states) - 1):
        for i, st in enumerate(states):
            if 0 <= t - i < len(stages):
                stages[t - i](st)


def _fourier_body(a_block, m_ref, e_ref, bf_ref, o_ref, n1):
    def block_dfts(st):
        st["g"] = []
        for q in range(n1):
            g = _dot(m_ref[q], a_block(q, st["cols"]))
            st["g"].append((g[:DFT_LEN], g[DFT_LEN:]))

    def across_blocks(st):
        st["u"] = _block_fft(st.pop("g"))

    def channel_map(st):
        us = st.pop("u")
        for g in st["groups"]:
            lo, hi = g * FNET_GROUP_DIM - st["cols"].start, (g + 1) * FNET_GROUP_DIM - st["cols"].start
            z = jnp.concatenate(
                [jnp.concatenate([ur[:, lo:hi], ui[:, lo:hi]], axis=1).astype(BF16) for ur, ui in us], axis=0)
            out = slice(g * FNET_GROUP_DIM, (g + 1) * FNET_GROUP_DIM)
            o_ref[:, out] = _dot(z, e_ref[g]) + bf_ref[:, out]

    halves = [dict(cols=slice(j * FFT_WIDTH, (j + 1) * FFT_WIDTH),
                   groups=range(j * FFT_WIDTH // FNET_GROUP_DIM, (j + 1) * FFT_WIDTH // FNET_GROUP_DIM))
              for j in range(FNET_WIDTH // FFT_WIDTH)]
    _run_skewed((block_dfts, across_blocks, channel_map), halves)


def _fourier_kernel(a_ref, m_ref, e_ref, bf_ref, o_ref, *, n1):
    _fourier_body(lambda q, cols: a_ref[q, :, cols], m_ref, e_ref, bf_ref, o_ref, n1)


def _front_kernel(x_ref, mod_ref, g_ref, win_ref, perm_ref, m_ref, e_ref, bf_ref, o_ref, *, n1):
    blocks = _regrouped_blocks(_fnet_in(x_ref, mod_ref, g_ref, win_ref), perm_ref, n1)
    _fourier_body(lambda q, cols: blocks[q][:, cols], m_ref, e_ref, bf_ref, o_ref, n1)


def _front_call(x, mod, norm_g, w_in, perm, m_tab, e_tab, b_fmix, n1):
    B, S, _ = x.shape
    const2 = lambda b: (0, 0)
    const3 = lambda b: (0, 0, 0)
    return pl.pallas_call(
        functools.partial(_front_kernel, n1=n1),
        grid=(B,),
        in_specs=[
            pl.BlockSpec((None, S, D_MODEL), lambda b: (b, 0, 0)),
            pl.BlockSpec((None, 3, D_MODEL), lambda b: (b, 0, 0)),
            pl.BlockSpec((1, D_MODEL), const2),
            pl.BlockSpec((D_MODEL, FNET_WIDTH), const2),
            pl.BlockSpec((SEQ_TILE, SEQ_TILE), const2),
            pl.BlockSpec((n1, 2 * DFT_LEN, DFT_LEN), const3),
            pl.BlockSpec((FNET_GROUPS, 2 * FNET_GROUP_DIM, FNET_GROUP_DIM), const3),
            pl.BlockSpec((1, FNET_WIDTH), const2),
        ],
        out_specs=pl.BlockSpec((None, S, FNET_WIDTH), lambda b: (b, 0, 0)),
        out_shape=jax.ShapeDtypeStruct((B, S, FNET_WIDTH), F32),
        compiler_params=pltpu.CompilerParams(
            dimension_semantics=("arbitrary",),
            vmem_limit_bytes=VMEM_LIMIT_BYTES),
        name="fnet_front",
    )(x, mod, norm_g, w_in, perm, m_tab, e_tab, b_fmix)


def _fourier_call(a_perm, m_tab, e_tab, b_fmix):
    B, n1, _, _ = a_perm.shape
    S = n1 * DFT_LEN
    return pl.pallas_call(
        functools.partial(_fourier_kernel, n1=n1),
        grid=(B,),
        in_specs=[
            pl.BlockSpec((None, n1, DFT_LEN, FNET_WIDTH), lambda b: (b, 0, 0, 0)),
            pl.BlockSpec((n1, 2 * DFT_LEN, DFT_LEN), lambda b: (0, 0, 0)),
            pl.BlockSpec((FNET_GROUPS, 2 * FNET_GROUP_DIM, FNET_GROUP_DIM), lambda b: (0, 0, 0)),
            pl.BlockSpec((1, FNET_WIDTH), lambda b: (0, 0)),
        ],
        out_specs=pl.BlockSpec((None, S, FNET_WIDTH), lambda b: (b, 0, 0)),
        out_shape=jax.ShapeDtypeStruct((B, S, FNET_WIDTH), F32),
        compiler_params=pltpu.CompilerParams(
            dimension_semantics=("arbitrary",),
            vmem_limit_bytes=VMEM_LIMIT_BYTES),
        name="fnet_fft",
    )(a_perm, m_tab, e_tab, b_fmix)


def _main_kernel(x_ref, mod_ref, ya_ref, g_ref, win_ref, lng_ref, lnb_ref, ws_ref, bs_ref,
                 wpa_ref, wpb_ref, wo_ref, fg_ref, o_ref, *, tm):
    def proj(st, rng):
        return _dot(st["h"], win_ref[:, rng[0]:rng[1]])

    def load_norm(st):
        st["x"] = x_ref[st["rows"], :]
        st["h"] = _modulated_norm(st["x"], mod_ref, g_ref).astype(BF16)

    def project(st):
        st["v"] = proj(st, _V)
        st["ga"] = proj(st, _GA)
        st["ma"] = proj(st, _MA)

    def project2(st):
        st["u"] = proj(st, _U)
        st["gb"] = proj(st, _GB)

    def layernorm(st):
        v = st.pop("v")
        mu = jnp.mean(v, axis=-1, keepdims=True)
        vc = v - mu
        var = jnp.mean(vc * vc, axis=-1, keepdims=True)
        st["vln"] = (vc * lax.rsqrt(var + EPS) * lng_ref[...] + lnb_ref[...]).astype(BF16)

    def spatial_mix(st):
        vln = st.pop("vln")
        rows = []
        for c in range(MAIN_SUB // CHUNK):
            r0 = c * CHUNK
            cols = []
            for hd in range(SGU_HEADS):
                lo, hi = hd * SGU_HEAD_DIM, (hd + 1) * SGU_HEAD_DIM
                cols.append(_dot(ws_ref[hd], vln[r0:r0 + CHUNK, lo:hi]))
            rows.append(jnp.concatenate(cols, axis=1) + bs_ref[...])
        st["mixed"] = jnp.concatenate(rows, axis=0)

    def fourier_path(st):
        y_a = (ya_ref[st["rows"], :] * _silu(st.pop("ga"))).astype(BF16)
        st["merged"] = jax.nn.sigmoid(st.pop("ma")) * _dot(y_a, wpa_ref[...])
        st["mb"] = proj(st, _MB)

    def gating_path(st):
        y_b = (st.pop("u") * st.pop("mixed") * _silu(st.pop("gb"))).astype(BF16)
        st["merged"] = st["merged"] + jax.nn.sigmoid(st.pop("mb")) * _dot(y_b, wpb_ref[...])

    def out_proj(st):
        st["out"] = _dot(st.pop("merged").astype(BF16), wo_ref[...])

    def residual(st):
        st["xo"] = st.pop("x") + mod_ref[2:3, :] * st.pop("out")
        st["ms"] = jnp.mean(st["xo"] * st["xo"], axis=-1, keepdims=True)

    def finish(st):
        o_ref[st["rows"], :] = st.pop("xo") * lax.rsqrt(st.pop("ms") + EPS) * fg_ref[...]

    stages = (load_norm, project, project2, layernorm, spatial_mix, fourier_path, gating_path, out_proj, residual, finish)
    subs = [dict(rows=slice(i * MAIN_SUB, (i + 1) * MAIN_SUB)) for i in range(tm // MAIN_SUB)]
    _run_skewed(stages, subs)


def _main_call(x, mod, ya, norm_g, w_in, ln_g, ln_b, w_s, bs_full, w_pa, w_pb, w_out,
               final_g, tm):
    B, S, _ = x.shape
    const2 = lambda b, i: (0, 0)
    return pl.pallas_call(
        functools.partial(_main_kernel, tm=tm),
        grid=(B, S // tm),
        in_specs=[
            pl.BlockSpec((None, tm, D_MODEL), lambda b, i: (b, i, 0)),
            pl.BlockSpec((None, 3, D_MODEL), lambda b, i: (b, 0, 0)),
            pl.BlockSpec((None, tm, FNET_WIDTH), lambda b, i: (b, i, 0)),
            pl.BlockSpec((1, D_MODEL), const2),
            pl.BlockSpec((D_MODEL, IN_WIDTH), const2, pipeline_mode=pl.Buffered(1)),
            pl.BlockSpec((1, SGU_WIDTH), const2),
            pl.BlockSpec((1, SGU_WIDTH), const2),
            pl.BlockSpec((SGU_HEADS, CHUNK, CHUNK), lambda b, i: (0, 0, 0)),
            pl.BlockSpec((CHUNK, SGU_WIDTH), const2),
            pl.BlockSpec((FNET_WIDTH, D_MODEL), const2, pipeline_mode=pl.Buffered(1)),
            pl.BlockSpec((SGU_WIDTH, D_MODEL), const2, pipeline_mode=pl.Buffered(1)),
            pl.BlockSpec((D_MODEL, D_MODEL), const2, pipeline_mode=pl.Buffered(1)),
            pl.BlockSpec((1, D_MODEL), const2),
        ],
        out_specs=pl.BlockSpec((None, tm, D_MODEL), lambda b, i: (b, i, 0)),
        out_shape=jax.ShapeDtypeStruct((B, S, D_MODEL), F32),
        compiler_params=pltpu.CompilerParams(
            dimension_semantics=("arbitrary", "arbitrary"),
            vmem_limit_bytes=VMEM_LIMIT_BYTES),
        name="encoder_main",
    )(x, mod, ya, norm_g, w_in, ln_g, ln_b, w_s, bs_full, w_pa, w_pb, w_out, final_g)


def _channel_dft_table():
    c = np.arange(FNET_GROUP_DIM)
    ang = 2.0 * np.pi * ((c[:, None] * c[None, :]) % FNET_GROUP_DIM) / FNET_GROUP_DIM
    scale = 1.0 / math.sqrt(FNET_GROUP_DIM)
    return np.concatenate([np.cos(ang), np.sin(ang)], axis=0) * scale


def _regroup_matrix(n1):
    rows = SEQ_TILE // n1
    p = np.zeros((SEQ_TILE, SEQ_TILE), np.float32)
    for q in range(n1):
        for j in range(rows):
            p[q * rows + j, j * n1 + q] = 1.0
    return p


def _sequence_dft_table(n1):
    S = n1 * DFT_LEN
    k2 = np.arange(DFT_LEN)
    th1 = 2.0 * np.pi * ((k2[None, :] * np.arange(n1)[:, None]) % S) / S
    th2 = 2.0 * np.pi * ((k2[:, None] * k2[None, :]) % DFT_LEN) / DFT_LEN
    scale = 1.0 / math.sqrt(S)
    c1 = jnp.asarray(np.cos(th1) * scale, F32)[:, :, None]
    s1 = jnp.asarray(np.sin(th1) * scale, F32)[:, :, None]
    c2 = jnp.asarray(np.cos(th2), F32)[None]
    s2 = jnp.asarray(np.sin(th2), F32)[None]
    cos = c1 * c2 - s1 * s2
    msin = -(s1 * c2 + c1 * s2)
    return jnp.concatenate([cos, msin], axis=1).astype(BF16)


def _trunk(x, mod, wts):
    B, S, _ = x.shape
    assert S % DFT_LEN == 0
    n1 = S // DFT_LEN
    assert n1 & (n1 - 1) == 0 and SEQ_TILE % n1 == 0 and S % PRE_TILE == 0 and S % MAIN_TILE == 0
    perm = jnp.asarray(_regroup_matrix(n1), F32).astype(BF16)
    m_tab = _sequence_dft_table(n1)
    if S == PRE_TILE:
        ya = _front_call(x, mod, wts["norm_g"], wts["w_in"], perm, m_tab, wts["e_tab"], wts["b_fmix"], n1)
    else:
        a_perm = _pre_call(x, mod, wts["norm_g"], wts["w_in"], perm, n1)
        ya = _fourier_call(a_perm, m_tab, wts["e_tab"], wts["b_fmix"])
    return _main_call(x, mod, ya, wts["norm_g"], wts["w_in"], wts["ln_g"], wts["ln_b"],
                      wts["w_s"], wts["bs_full"], wts["w_pa"], wts["w_pb"], wts["w_out"],
                      wts["final_g"], MAIN_TILE)


def kernel(x_prompt, x_sample, c_prompt, c_sample, norm_g, w_ada, b_ada, w_in, w_fmix, b_fmix,
           sgu_ln_g, sgu_ln_b, w_s, b_s, w_pa, w_pb, w_out, final_g):
    depth = norm_g.shape[0]
    assert depth == 1
    l = 0
    nb_p = x_prompt.shape[0]
    mod, e_tab, w_s_bf, w_in_bf, w_pa_bf, w_pb_bf, w_out_bf = _prepare(
        jnp.concatenate([c_prompt, c_sample], axis=0), w_ada[l], b_ada[l],
        jnp.asarray(_channel_dft_table(), F32), w_fmix[l], w_s[l], w_in[l], w_pa[l], w_pb[l], w_out[l])
    mod = mod.reshape(-1, 3, D_MODEL)
    wts = dict(
        norm_g=norm_g[l].reshape(1, -1),
        w_in=w_in_bf,
        e_tab=e_tab,
        b_fmix=b_fmix[l].reshape(1, -1),
        ln_g=sgu_ln_g[l].reshape(1, -1),
        ln_b=sgu_ln_b[l].reshape(1, -1),
        w_s=w_s_bf,
        bs_full=jnp.repeat(b_s[l].T, SGU_HEAD_DIM, axis=1),
        w_pa=w_pa_bf,
        w_pb=w_pb_bf,
        w_out=w_out_bf,
        final_g=final_g.reshape(1, -1),
    )
    y_prompt = _trunk(x_prompt, mod[:nb_p], wts)
    y_sample = _trunk(x_sample, mod[nb_p:], wts)
    return (y_prompt, y_sample)
```

```python
import functools
import math

import numpy as np
import jax
import jax.numpy as jnp
from jax import lax
from jax.experimental import pallas as pl
from jax.experimental.pallas import tpu as pltpu

D_MODEL = 1024
FNET_WIDTH = 512
FNET_GROUPS = 4
FNET_GROUP_DIM = 128
SGU_WIDTH = 512
SGU_HEADS = 4
SGU_HEAD_DIM = 128
CHUNK = 128
EPS = 1e-6
PREP_STEPS = 4
DFT_LEN = 256
SEQ_TILE = 256
PRE_TILE = 2048
MAIN_TILE = 1024
MAIN_SUB = 256
FFT_WIDTH = 256

_GA = (512, 1024)
_U = (1024, 1536)
_V = (1536, 2048)
_GB = (2048, 2560)
_MA = (2560, 3584)
_MB = (3584, 4608)
IN_WIDTH = 4608

BF16 = jnp.bfloat16
F32 = jnp.float32

VMEM_LIMIT_BYTES = 56 * 1024 * 1024


def _dot(a, b):
    return jnp.dot(a, b, preferred_element_type=F32)


def _silu(x):
    return x * jax.nn.sigmoid(x)


def _split_bf16(x):
    hi = x.astype(BF16)
    return hi, (x - hi.astype(F32)).astype(BF16)


def _dot_3pass(a, b):
    a_hi, a_lo = _split_bf16(a)
    b_hi, b_lo = _split_bf16(b)
    return _dot(a_hi, b_hi) + (_dot(a_lo, b_hi) + _dot(a_hi, b_lo))


def _prep_kernel(*refs, table_blocks):
    nt = len(table_blocks)
    (c_ref, wada_ref, bada_ref, cs_ref, wf_ref, ws_ref, win_ref, wpa_ref, wpb_ref, wo_ref,
     c2_ref, s2_ref) = refs[:12]
    factor_refs = refs[12:12 + 2 * nt]
    mod_ref, e_ref, ws_o, win_o, wpa_o, wpb_o, wo_o = refs[12 + 2 * nt:19 + 2 * nt]
    table_refs = refs[19 + 2 * nt:]

    mod_ref[...] = _dot_3pass(_silu(c_ref[...]), wada_ref[...]) + bada_ref[...]
    win_o[...] = win_ref[...].astype(BF16)
    wpa_o[...] = wpa_ref[...].astype(BF16)
    wpb_o[...] = wpb_ref[...].astype(BF16)
    wo_o[...] = wo_ref[...].astype(BF16)

    c2, s2 = c2_ref[...], s2_ref[...]
    for t in range(nt):
        c1_ref, s1_ref = factor_refs[2 * t], factor_refs[2 * t + 1]
        for q in range(table_blocks[t]):
            c1, s1 = c1_ref[q], s1_ref[q]
            table_refs[t][q, 0:DFT_LEN, :] = (c1 * c2 - s1 * s2).astype(BF16)
            table_refs[t][q, DFT_LEN:, :] = (-(s1 * c2 + c1 * s2)).astype(BF16)

    @pl.when(pl.program_id(0) == 0)
    def _():
        ws_o[...] = ws_ref[...].astype(BF16)
        for g in range(FNET_GROUPS):
            e_ref[g] = _dot_3pass(cs_ref[...], wf_ref[g]).astype(BF16)


def _prepare(c, w_ada, b_ada, cs_c, w_fmix, w_s, w_in, w_pa, w_pb, w_out, seq_factors):
    n = c.shape[0]
    e_shape = (FNET_GROUPS, 2 * FNET_GROUP_DIM, FNET_GROUP_DIM)
    whole = lambda shape: pl.BlockSpec(shape, lambda j: (0,) * len(shape))
    cols = lambda a: pl.BlockSpec((a.shape[0], a.shape[1] // PREP_STEPS), lambda j: (0, j))
    lead = lambda shape: pl.BlockSpec((shape[0] // PREP_STEPS,) + shape[1:], lambda j: (j, 0, 0))
    streamed = (w_in, w_pa, w_pb, w_out)
    assert all(a.shape[1] % (PREP_STEPS * FNET_GROUP_DIM) == 0 for a in (w_ada,) + streamed)
    n1s = sorted(seq_factors)
    assert all(n1 % PREP_STEPS == 0 for n1 in n1s)
    c2, s2 = _shared_dft_factors()
    factors = [f for n1 in n1s for f in _block_dft_factors(n1)]
    t_shapes = [(n1, 2 * DFT_LEN, DFT_LEN) for n1 in n1s]
    outs = pl.pallas_call(
        functools.partial(_prep_kernel, table_blocks=tuple(n1 // PREP_STEPS for n1 in n1s)),
        grid=(PREP_STEPS,),
        in_specs=[whole(c.shape), cols(w_ada), pl.BlockSpec((1, w_ada.shape[1] // PREP_STEPS), lambda j: (0, j)),
                  whole(cs_c.shape), whole(w_fmix.shape), whole(w_s.shape)] + [cols(a) for a in streamed]
                 + [whole(c2.shape), whole(s2.shape)] + [lead(f.shape) for f in factors],
        out_specs=[pl.BlockSpec((n, w_ada.shape[1] // PREP_STEPS), lambda j: (0, j)),
                   whole(e_shape), whole(w_s.shape)] + [cols(a) for a in streamed] + [lead(t) for t in t_shapes],
        out_shape=[jax.ShapeDtypeStruct((n, w_ada.shape[1]), F32), jax.ShapeDtypeStruct(e_shape, BF16),
                   jax.ShapeDtypeStruct(w_s.shape, BF16)] + [jax.ShapeDtypeStruct(a.shape, BF16) for a in streamed]
                  + [jax.ShapeDtypeStruct(t, BF16) for t in t_shapes],
        compiler_params=pltpu.CompilerParams(
            dimension_semantics=("arbitrary",),
            vmem_limit_bytes=VMEM_LIMIT_BYTES),
        name="adaln_prep",
    )(c, w_ada, b_ada.reshape(1, -1), cs_c, w_fmix, w_s, *streamed, c2, s2, *factors)
    return outs[:7], dict(zip(n1s, outs[7:]))


def _modulated_norm(x, mod_ref, g_ref):
    ms = jnp.mean(x * x, axis=-1, keepdims=True)
    xn = x * lax.rsqrt(ms + EPS) * g_ref[...]
    shift = mod_ref[0:1, :]
    scale = mod_ref[1:2, :]
    return xn * (1.0 + scale) + shift


def _regrouped_blocks(a, perm_ref, n1):
    rows = SEQ_TILE // n1
    aps = [_dot(perm_ref[...], a[s * SEQ_TILE:(s + 1) * SEQ_TILE]).astype(BF16)
           for s in range(a.shape[0] // SEQ_TILE)]
    return [jnp.concatenate([ap[q * rows:(q + 1) * rows] for ap in aps], axis=0) for q in range(n1)]


def _fnet_in(x_ref, mod_ref, g_ref, win_ref):
    h = _modulated_norm(x_ref[...], mod_ref, g_ref).astype(BF16)
    return _dot(h, win_ref[...]).astype(BF16)


def _pre_kernel(x_ref, mod_ref, g_ref, win_ref, perm_ref, a_ref, *, n1):
    for q, block in enumerate(_regrouped_blocks(_fnet_in(x_ref, mod_ref, g_ref, win_ref), perm_ref, n1)):
        a_ref[q] = block


def _pre_call(x, mod, norm_g, w_in, perm, n1):
    B, S, _ = x.shape
    rows = PRE_TILE // n1
    return pl.pallas_call(
        functools.partial(_pre_kernel, n1=n1),
        grid=(B, S // PRE_TILE),
        in_specs=[
            pl.BlockSpec((None, PRE_TILE, D_MODEL), lambda b, i: (b, i, 0)),
            pl.BlockSpec((None, 3, D_MODEL), lambda b, i: (b, 0, 0)),
            pl.BlockSpec((1, D_MODEL), lambda b, i: (0, 0)),
            pl.BlockSpec((D_MODEL, FNET_WIDTH), lambda b, i: (0, 0)),
            pl.BlockSpec((SEQ_TILE, SEQ_TILE), lambda b, i: (0, 0)),
        ],
        out_specs=pl.BlockSpec((None, n1, rows, FNET_WIDTH), lambda b, i: (b, 0, i, 0)),
        out_shape=jax.ShapeDtypeStruct((B, n1, DFT_LEN, FNET_WIDTH), BF16),
        compiler_params=pltpu.CompilerParams(
            dimension_semantics=("arbitrary", "arbitrary"),
            vmem_limit_bytes=VMEM_LIMIT_BYTES),
        name="fnet_pre",
    )(x, mod, norm_g, w_in, perm)


def _cmul_const(z, w):
    zr, zi = z
    wr, wi = w.real, w.imag
    tol = 1e-12
    if abs(wi) < tol:
        return (zr, zi) if abs(wr - 1) < tol else (zr * wr, zi * wr)
    if abs(wr) < tol:
        if abs(wi + 1) < tol:
            return (zi, -zr)
        if abs(wi - 1) < tol:
            return (-zi, zr)
    if abs(abs(wr) - abs(wi)) < tol:
        c = abs(wr)
        sr, si = math.copysign(1.0, wr), math.copysign(1.0, wi)
        re = (zr if sr > 0 else -zr) - (zi if si > 0 else -zi)
        im = (zi if sr > 0 else -zi) + (zr if si > 0 else -zr)
        return (re * c, im * c)
    return (zr * wr - zi * wi, zr * wi + zi * wr)


def _block_fft(zs):
    n = len(zs)
    if n == 1:
        return zs
    even = _block_fft(zs[0::2])
    odd = _block_fft(zs[1::2])
    out = [None] * n
    for k in range(n // 2):
        w = complex(math.cos(2 * math.pi * k / n), -math.sin(2 * math.pi * k / n))
        tr, ti = _cmul_const(odd[k], w)
        er, ei = even[k]
        out[k] = (er + tr, ei + ti)
        out[k + n // 2] = (er - tr, ei - ti)
    return out


def _run_skewed(stages, states):
    for t in range(len(stages) + len(states) - 1):
        for i, st in enumerate(states):
            if 0 <= t - i < len(stages):
                stages[t - i](st)


def _fourier_body(a_block, m_ref, e_ref, bf_ref, o_ref, n1):
    def block_dfts(st):
        st["g"] = []
        for q in range(n1):
            g = _dot(m_ref[q], a_block(q, st["cols"]))
            st["g"].append((g[:DFT_LEN], g[DFT_LEN:]))

    def across_blocks(st):
        st["u"] = _block_fft(st.pop("g"))

    def channel_map(st):
        us = st.pop("u")
        for g in st["groups"]:
            lo, hi = g * FNET_GROUP_DIM - st["cols"].start, (g + 1) * FNET_GROUP_DIM - st["cols"].start
            z = jnp.concatenate(
                [jnp.concatenate([ur[:, lo:hi], ui[:, lo:hi]], axis=1).astype(BF16) for ur, ui in us], axis=0)
            out = slice(g * FNET_GROUP_DIM, (g + 1) * FNET_GROUP_DIM)
            o_ref[:, out] = _dot(z, e_ref[g]) + bf_ref[:, out]

    halves = [dict(cols=slice(j * FFT_WIDTH, (j + 1) * FFT_WIDTH),
                   groups=range(j * FFT_WIDTH // FNET_GROUP_DIM, (j + 1) * FFT_WIDTH // FNET_GROUP_DIM))
              for j in range(FNET_WIDTH // FFT_WIDTH)]
    _run_skewed((block_dfts, across_blocks, channel_map), halves)


def _fourier_kernel(a_ref, m_ref, e_ref, bf_ref, o_ref, *, n1):
    _fourier_body(lambda q, cols: a_ref[q, :, cols], m_ref, e_ref, bf_ref, o_ref, n1)


def _front_kernel(x_ref, mod_ref, g_ref, win_ref, perm_ref, m_ref, e_ref, bf_ref, o_ref, *, n1):
    blocks = _regrouped_blocks(_fnet_in(x_ref, mod_ref, g_ref, win_ref), perm_ref, n1)
    _fourier_body(lambda q, cols: blocks[q][:, cols], m_ref, e_ref, bf_ref, o_ref, n1)


def _front_call(x, mod, norm_g, w_in, perm, m_tab, e_tab, b_fmix, n1):
    B, S, _ = x.shape
    const2 = lambda b: (0, 0)
    const3 = lambda b: (0, 0, 0)
    return pl.pallas_call(
        functools.partial(_front_kernel, n1=n1),
        grid=(B,),
        in_specs=[
            pl.BlockSpec((None, S, D_MODEL), lambda b: (b, 0, 0)),
            pl.BlockSpec((None, 3, D_MODEL), lambda b: (b, 0, 0)),
            pl.BlockSpec((1, D_MODEL), const2),
            pl.BlockSpec((D_MODEL, FNET_WIDTH), const2),
            pl.BlockSpec((SEQ_TILE, SEQ_TILE), const2),
            pl.BlockSpec((n1, 2 * DFT_LEN, DFT_LEN), const3),
            pl.BlockSpec((FNET_GROUPS, 2 * FNET_GROUP_DIM, FNET_GROUP_DIM), const3),
            pl.BlockSpec((1, FNET_WIDTH), const2),
        ],
        out_specs=pl.BlockSpec((None, S, FNET_WIDTH), lambda b: (b, 0, 0)),
        out_shape=jax.ShapeDtypeStruct((B, S, FNET_WIDTH), F32),
        compiler_params=pltpu.CompilerParams(
            dimension_semantics=("arbitrary",),
            vmem_limit_bytes=VMEM_LIMIT_BYTES),
        name="fnet_front",
    )(x, mod, norm_g, w_in, perm, m_tab, e_tab, b_fmix)


def _fourier_call(a_perm, m_tab, e_tab, b_fmix):
    B, n1, _, _ = a_perm.shape
    S = n1 * DFT_LEN
    return pl.pallas_call(
        functools.partial(_fourier_kernel, n1=n1),
        grid=(B,),
        in_specs=[
            pl.BlockSpec((None, n1, DFT_LEN, FNET_WIDTH), lambda b: (b, 0, 0, 0)),
            pl.BlockSpec((n1, 2 * DFT_LEN, DFT_LEN), lambda b: (0, 0, 0)),
            pl.BlockSpec((FNET_GROUPS, 2 * FNET_GROUP_DIM, FNET_GROUP_DIM), lambda b: (0, 0, 0)),
            pl.BlockSpec((1, FNET_WIDTH), lambda b: (0, 0)),
        ],
        out_specs=pl.BlockSpec((None, S, FNET_WIDTH), lambda b: (b, 0, 0)),
        out_shape=jax.ShapeDtypeStruct((B, S, FNET_WIDTH), F32),
        compiler_params=pltpu.CompilerParams(
            dimension_semantics=("arbitrary",),
            vmem_limit_bytes=VMEM_LIMIT_BYTES),
        name="fnet_fft",
    )(a_perm, m_tab, e_tab, b_fmix)


def _main_kernel(x_ref, mod_ref, ya_ref, g_ref, win_ref, lng_ref, lnb_ref, ws_ref, bs_ref,
                 wpa_ref, wpb_ref, wo_ref, fg_ref, o_ref, *, tm):
    def proj(st, rng):
        return _dot(st["h"], win_ref[:, rng[0]:rng[1]])

    def load_norm(st):
        st["x"] = x_ref[st["rows"], :]
        st["h"] = _modulated_norm(st["x"], mod_ref, g_ref).astype(BF16)

    def project(st):
        st["v"] = proj(st, _V)
        st["ga"] = proj(st, _GA)
        st["ma"] = proj(st, _MA)

    def project2(st):
        st["u"] = proj(st, _U)
        st["gb"] = proj(st, _GB)

    def layernorm(st):
        v = st.pop("v")
        mu = jnp.mean(v, axis=-1, keepdims=True)
        vc = v - mu
        var = jnp.mean(vc * vc, axis=-1, keepdims=True)
        st["vln"] = (vc * lax.rsqrt(var + EPS) * lng_ref[...] + lnb_ref[...]).astype(BF16)

    def spatial_mix(st):
        vln = st.pop("vln")
        rows = []
        for c in range(MAIN_SUB // CHUNK):
            r0 = c * CHUNK
            cols = []
            for hd in range(SGU_HEADS):
                lo, hi = hd * SGU_HEAD_DIM, (hd + 1) * SGU_HEAD_DIM
                cols.append(_dot(ws_ref[hd], vln[r0:r0 + CHUNK, lo:hi]))
            rows.append(jnp.concatenate(cols, axis=1) + bs_ref[...])
        st["mixed"] = jnp.concatenate(rows, axis=0)

    def fourier_path(st):
        y_a = (ya_ref[st["rows"], :] * _silu(st.pop("ga"))).astype(BF16)
        st["merged"] = jax.nn.sigmoid(st.pop("ma")) * _dot(y_a, wpa_ref[...])
        st["mb"] = proj(st, _MB)

    def gating_path(st):
        y_b = (st.pop("u") * st.pop("mixed") * _silu(st.pop("gb"))).astype(BF16)
        st["merged"] = st["merged"] + jax.nn.sigmoid(st.pop("mb")) * _dot(y_b, wpb_ref[...])

    def out_proj(st):
        st["out"] = _dot(st.pop("merged").astype(BF16), wo_ref[...])

    def residual(st):
        st["xo"] = st.pop("x") + mod_ref[2:3, :] * st.pop("out")
        st["ms"] = jnp.mean(st["xo"] * st["xo"], axis=-1, keepdims=True)

    def finish(st):
        o_ref[st["rows"], :] = st.pop("xo") * lax.rsqrt(st.pop("ms") + EPS) * fg_ref[...]

    stages = (load_norm, project, project2, layernorm, spatial_mix, fourier_path, gating_path, out_proj,
              residual, finish)
    subs = [dict(rows=slice(i * MAIN_SUB, (i + 1) * MAIN_SUB)) for i in range(tm // MAIN_SUB)]
    _run_skewed(stages, subs)


def _main_call(x, mod, ya, norm_g, w_in, ln_g, ln_b, w_s, bs_full, w_pa, w_pb, w_out,
               final_g, tm):
    B, S, _ = x.shape
    const2 = lambda b, i: (0, 0)
    return pl.pallas_call(
        functools.partial(_main_kernel, tm=tm),
        grid=(B, S // tm),
        in_specs=[
            pl.BlockSpec((None, tm, D_MODEL), lambda b, i: (b, i, 0)),
            pl.BlockSpec((None, 3, D_MODEL), lambda b, i: (b, 0, 0)),
            pl.BlockSpec((None, tm, FNET_WIDTH), lambda b, i: (b, i, 0)),
            pl.BlockSpec((1, D_MODEL), const2),
            pl.BlockSpec((D_MODEL, IN_WIDTH), const2),
            pl.BlockSpec((1, SGU_WIDTH), const2),
            pl.BlockSpec((1, SGU_WIDTH), const2),
            pl.BlockSpec((SGU_HEADS, CHUNK, CHUNK), lambda b, i: (0, 0, 0)),
            pl.BlockSpec((CHUNK, SGU_WIDTH), const2),
            pl.BlockSpec((FNET_WIDTH, D_MODEL), const2),
            pl.BlockSpec((SGU_WIDTH, D_MODEL), const2),
            pl.BlockSpec((D_MODEL, D_MODEL), const2),
            pl.BlockSpec((1, D_MODEL), const2),
        ],
        out_specs=pl.BlockSpec((None, tm, D_MODEL), lambda b, i: (b, i, 0)),
        out_shape=jax.ShapeDtypeStruct((B, S, D_MODEL), F32),
        compiler_params=pltpu.CompilerParams(
            dimension_semantics=("arbitrary", "arbitrary"),
            vmem_limit_bytes=VMEM_LIMIT_BYTES),
        name="encoder_main",
    )(x, mod, ya, norm_g, w_in, ln_g, ln_b, w_s, bs_full, w_pa, w_pb, w_out, final_g)


def _channel_dft_table():
    c = np.arange(FNET_GROUP_DIM)
    ang = 2.0 * np.pi * ((c[:, None] * c[None, :]) % FNET_GROUP_DIM) / FNET_GROUP_DIM
    scale = 1.0 / math.sqrt(FNET_GROUP_DIM)
    return np.concatenate([np.cos(ang), np.sin(ang)], axis=0) * scale


def _regroup_matrix(n1):
    rows = SEQ_TILE // n1
    p = np.zeros((SEQ_TILE, SEQ_TILE), np.float32)
    for q in range(n1):
        for j in range(rows):
            p[q * rows + j, j * n1 + q] = 1.0
    return p


def _shared_dft_factors():
    k2 = np.arange(DFT_LEN)
    th2 = 2.0 * np.pi * ((k2[:, None] * k2[None, :]) % DFT_LEN) / DFT_LEN
    return jnp.asarray(np.cos(th2), F32), jnp.asarray(np.sin(th2), F32)


def _block_dft_factors(n1):
    S = n1 * DFT_LEN
    th1 = 2.0 * np.pi * ((np.arange(DFT_LEN)[None, :] * np.arange(n1)[:, None]) % S) / S
    scale = 1.0 / math.sqrt(S)
    return (jnp.asarray(np.cos(th1) * scale, F32)[:, :, None], jnp.asarray(np.sin(th1) * scale, F32)[:, :, None])


def _trunk(x, mod, wts):
    B, S, _ = x.shape
    assert S % DFT_LEN == 0
    n1 = S // DFT_LEN
    assert n1 & (n1 - 1) == 0 and SEQ_TILE % n1 == 0 and S % PRE_TILE == 0 and S % MAIN_TILE == 0
    perm = jnp.asarray(_regroup_matrix(n1), F32).astype(BF16)
    m_tab = wts["m_tab"][n1]
    if S == PRE_TILE:
        ya = _front_call(x, mod, wts["norm_g"], wts["w_in"], perm, m_tab, wts["e_tab"], wts["b_fmix"], n1)
    else:
        a_perm = _pre_call(x, mod, wts["norm_g"], wts["w_in"], perm, n1)
        ya = _fourier_call(a_perm, m_tab, wts["e_tab"], wts["b_fmix"])
    return _main_call(x, mod, ya, wts["norm_g"], wts["w_in"], wts["ln_g"], wts["ln_b"],
                      wts["w_s"], wts["bs_full"], wts["w_pa"], wts["w_pb"], wts["w_out"],
                      wts["final_g"], MAIN_TILE)


def kernel(x_prompt, x_sample, c_prompt, c_sample, norm_g, w_ada, b_ada, w_in, w_fmix, b_fmix,
           sgu_ln_g, sgu_ln_b, w_s, b_s, w_pa, w_pb, w_out, final_g):
    depth = norm_g.shape[0]
    assert depth == 1
    l = 0
    nb_p = x_prompt.shape[0]
    (mod, e_tab, w_s_bf, w_in_bf, w_pa_bf, w_pb_bf, w_out_bf), m_tabs = _prepare(
        jnp.concatenate([c_prompt, c_sample], axis=0), w_ada[l], b_ada[l],
        jnp.asarray(_channel_dft_table(), F32), w_fmix[l], w_s[l], w_in[l], w_pa[l], w_pb[l], w_out[l],
        seq_factors={x.shape[1] // DFT_LEN for x in (x_prompt, x_sample)})
    mod = mod.reshape(-1, 3, D_MODEL)
    wts = dict(
        norm_g=norm_g[l].reshape(1, -1),
        w_in=w_in_bf,
        e_tab=e_tab,
        m_tab=m_tabs,
        b_fmix=b_fmix[l].reshape(1, -1),
        ln_g=sgu_ln_g[l].reshape(1, -1),
        ln_b=sgu_ln_b[l].reshape(1, -1),
        w_s=w_s_bf,
        bs_full=jnp.repeat(b_s[l].T, SGU_HEAD_DIM, axis=1),
        w_pa=w_pa_bf,
        w_pb=w_pb_bf,
        w_out=w_out_bf,
        final_g=final_g.reshape(1, -1),
    )
    y_prompt = _trunk(x_prompt, mod[:nb_p], wts)
    y_sample = _trunk(x_sample, mod[nb_p:], wts)
    return (y_prompt, y_sample)
```

```python
import functools
import math

import numpy as np
import jax
import jax.numpy as jnp
from jax import lax
from jax.experimental import pallas as pl
from jax.experimental.pallas import tpu as pltpu

D_MODEL = 1024
FNET_WIDTH = 512
FNET_GROUPS = 4
FNET_GROUP_DIM = 128
SGU_WIDTH = 512
SGU_HEADS = 4
SGU_HEAD_DIM = 128
CHUNK = 128
EPS = 1e-6
PREP_STEPS = 4
DFT_LEN = 256
SEQ_TILE = 256
PRE_TILE = 2048
MAIN_TILE = 1024
MAIN_SUB = 256
FFT_WIDTH = 256

_IN_SIZES = (FNET_WIDTH, FNET_WIDTH, SGU_WIDTH, SGU_WIDTH, SGU_WIDTH, D_MODEL, D_MODEL)
_IN_ENDS = tuple(int(v) for v in np.cumsum(_IN_SIZES))
_A, _GA, _U, _V, _GB, _MA, _MB = ((end - size, end) for size, end in zip(_IN_SIZES, _IN_ENDS))
IN_WIDTH = _IN_ENDS[-1]

BF16 = jnp.bfloat16
F32 = jnp.float32

VMEM_LIMIT_BYTES = 56 * 1024 * 1024


def _dot(a, b):
    return jnp.dot(a, b, preferred_element_type=F32)


def _silu(x):
    return x * jax.nn.sigmoid(x)


def _split_bf16(x):
    hi = x.astype(BF16)
    return hi, (x - hi.astype(F32)).astype(BF16)


def _dot_3pass(a, b):
    a_hi, a_lo = _split_bf16(a)
    b_hi, b_lo = _split_bf16(b)
    return _dot(a_hi, b_hi) + (_dot(a_lo, b_hi) + _dot(a_hi, b_lo))


def _prep_kernel(*refs, table_blocks):
    refs = list(refs)
    take = lambda k: [refs.pop(0) for _ in range(k)]
    c_ref, wada_ref, bada_ref, cs_ref, wf_ref, ws_ref, win_ref, wpa_ref, wpb_ref, wo_ref, c2_ref, s2_ref = take(12)
    factor_refs = take(2 * len(table_blocks))
    mod_ref, e_ref, ws_o, win_o, wpa_o, wpb_o, wo_o = take(7)
    table_refs = refs

    mod_ref[...] = _dot_3pass(_silu(c_ref[...]), wada_ref[...]) + bada_ref[...]
    win_o[...] = win_ref[...].astype(BF16)
    wpa_o[...] = wpa_ref[...].astype(BF16)
    wpb_o[...] = wpb_ref[...].astype(BF16)
    wo_o[...] = wo_ref[...].astype(BF16)

    c2, s2 = c2_ref[...], s2_ref[...]
    for t, table_ref in enumerate(table_refs):
        c1_ref, s1_ref = factor_refs[2 * t], factor_refs[2 * t + 1]
        for q in range(table_blocks[t]):
            c1, s1 = c1_ref[q], s1_ref[q]
            table_ref[q, 0:DFT_LEN, :] = (c1 * c2 - s1 * s2).astype(BF16)
            table_ref[q, DFT_LEN:, :] = (-(s1 * c2 + c1 * s2)).astype(BF16)

    @pl.when(pl.program_id(0) == 0)
    def _():
        ws_o[...] = ws_ref[...].astype(BF16)
        for g in range(FNET_GROUPS):
            e_ref[g] = _dot_3pass(cs_ref[...], wf_ref[g]).astype(BF16)


def _prepare(c, w_ada, b_ada, cs_c, w_fmix, w_s, w_in, w_pa, w_pb, w_out, seq_factors):
    n = c.shape[0]
    e_shape = (FNET_GROUPS, 2 * FNET_GROUP_DIM, FNET_GROUP_DIM)
    whole = lambda shape: pl.BlockSpec(shape, lambda j: (0,) * len(shape))
    cols = lambda a: pl.BlockSpec((a.shape[0], a.shape[1] // PREP_STEPS), lambda j: (0, j))
    lead = lambda shape: pl.BlockSpec((shape[0] // PREP_STEPS,) + shape[1:], lambda j: (j, 0, 0))
    streamed = (w_in, w_pa, w_pb, w_out)
    assert all(a.shape[1] % (PREP_STEPS * FNET_GROUP_DIM) == 0 for a in (w_ada,) + streamed)
    n1s = sorted(seq_factors)
    assert all(n1 % PREP_STEPS == 0 for n1 in n1s)
    c2, s2 = _shared_dft_factors()
    factors = [f for n1 in n1s for f in _block_dft_factors(n1)]
    t_shapes = [(n1, 2 * DFT_LEN, DFT_LEN) for n1 in n1s]
    outs = pl.pallas_call(
        functools.partial(_prep_kernel, table_blocks=tuple(n1 // PREP_STEPS for n1 in n1s)),
        grid=(PREP_STEPS,),
        in_specs=[whole(c.shape), cols(w_ada), pl.BlockSpec((1, w_ada.shape[1] // PREP_STEPS), lambda j: (0, j)),
                  whole(cs_c.shape), whole(w_fmix.shape), whole(w_s.shape)] + [cols(a) for a in streamed]
                 + [whole(c2.shape), whole(s2.shape)] + [lead(f.shape) for f in factors],
        out_specs=[pl.BlockSpec((n, w_ada.shape[1] // PREP_STEPS), lambda j: (0, j)),
                   whole(e_shape), whole(w_s.shape)] + [cols(a) for a in streamed] + [lead(t) for t in t_shapes],
        out_shape=[jax.ShapeDtypeStruct((n, w_ada.shape[1]), F32), jax.ShapeDtypeStruct(e_shape, BF16),
                   jax.ShapeDtypeStruct(w_s.shape, BF16)] + [jax.ShapeDtypeStruct(a.shape, BF16) for a in streamed]
                  + [jax.ShapeDtypeStruct(t, BF16) for t in t_shapes],
        compiler_params=pltpu.CompilerParams(
            dimension_semantics=("arbitrary",),
            vmem_limit_bytes=VMEM_LIMIT_BYTES),
        name="adaln_prep",
    )(c, w_ada, b_ada.reshape(1, -1), cs_c, w_fmix, w_s, *streamed, c2, s2, *factors)
    return outs[:7], dict(zip(n1s, outs[7:]))


def _modulated_norm(x, mod_ref, g_ref):
    ms = jnp.mean(x * x, axis=-1, keepdims=True)
    xn = x * lax.rsqrt(ms + EPS) * g_ref[...]
    shift = mod_ref[0:1, :]
    scale = mod_ref[1:2, :]
    return xn * (1.0 + scale) + shift


def _regrouped_blocks(a, perm_ref, n1):
    rows = SEQ_TILE // n1
    aps = [_dot(perm_ref[...], a[s * SEQ_TILE:(s + 1) * SEQ_TILE]).astype(BF16)
           for s in range(a.shape[0] // SEQ_TILE)]
    return [jnp.concatenate([ap[q * rows:(q + 1) * rows] for ap in aps], axis=0) for q in range(n1)]


def _fnet_in(x_ref, mod_ref, g_ref, wa_ref):
    h = _modulated_norm(x_ref[...], mod_ref, g_ref).astype(BF16)
    return _dot(h, wa_ref[...]).astype(BF16)


def _pre_kernel(x_ref, mod_ref, g_ref, win_ref, perm_ref, a_ref, *, n1):
    for q, block in enumerate(_regrouped_blocks(_fnet_in(x_ref, mod_ref, g_ref, win_ref), perm_ref, n1)):
        a_ref[q] = block


def _pre_call(x, mod, norm_g, w_in, perm, n1):
    B, S, _ = x.shape
    rows = PRE_TILE // n1
    return pl.pallas_call(
        functools.partial(_pre_kernel, n1=n1),
        grid=(B, S // PRE_TILE),
        in_specs=[
            pl.BlockSpec((None, PRE_TILE, D_MODEL), lambda b, i: (b, i, 0)),
            pl.BlockSpec((None, 3, D_MODEL), lambda b, i: (b, 0, 0)),
            pl.BlockSpec((1, D_MODEL), lambda b, i: (0, 0)),
            pl.BlockSpec((D_MODEL, FNET_WIDTH), lambda b, i: (0, _A[0] // FNET_WIDTH)),
            pl.BlockSpec((SEQ_TILE, SEQ_TILE), lambda b, i: (0, 0)),
        ],
        out_specs=pl.BlockSpec((None, n1, rows, FNET_WIDTH), lambda b, i: (b, 0, i, 0)),
        out_shape=jax.ShapeDtypeStruct((B, n1, DFT_LEN, FNET_WIDTH), BF16),
        compiler_params=pltpu.CompilerParams(
            dimension_semantics=("arbitrary", "arbitrary"),
            vmem_limit_bytes=VMEM_LIMIT_BYTES),
        name="fnet_pre",
    )(x, mod, norm_g, w_in, perm)


def _cmul_const(z, w):
    zr, zi = z
    wr, wi = w.real, w.imag
    tol = 1e-12
    if abs(wi) < tol:
        return (zr, zi) if abs(wr - 1) < tol else (zr * wr, zi * wr)
    if abs(wr) < tol:
        if abs(wi + 1) < tol:
            return (zi, -zr)
        if abs(wi - 1) < tol:
            return (-zi, zr)
    if abs(abs(wr) - abs(wi)) < tol:
        c = abs(wr)
        sr, si = math.copysign(1.0, wr), math.copysign(1.0, wi)
        re = (zr if sr > 0 else -zr) - (zi if si > 0 else -zi)
        im = (zi if sr > 0 else -zi) + (zr if si > 0 else -zr)
        return (re * c, im * c)
    return (zr * wr - zi * wi, zr * wi + zi * wr)


def _block_fft(zs):
    n = len(zs)
    if n == 1:
        return zs
    even = _block_fft(zs[0::2])
    odd = _block_fft(zs[1::2])
    out = [None] * n
    for k in range(n // 2):
        w = complex(math.cos(2 * math.pi * k / n), -math.sin(2 * math.pi * k / n))
        tr, ti = _cmul_const(odd[k], w)
        er, ei = even[k]
        out[k] = (er + tr, ei + ti)
        out[k + n // 2] = (er - tr, ei - ti)
    return out


def _run_skewed(stages, states):
    for t in range(len(stages) + len(states) - 1):
        for i, st in enumerate(states):
            if 0 <= t - i < len(stages):
                stages[t - i](st)


def _fourier_body(a_block, m_ref, e_ref, bf_ref, o_ref, n1):
    def block_dfts(st):
        st["g"] = []
        for q in range(n1):
            g = _dot(m_ref[q], a_block(q, st["cols"]))
            st["g"].append((g[:DFT_LEN], g[DFT_LEN:]))

    def across_blocks(st):
        st["u"] = _block_fft(st.pop("g"))

    def channel_map(st):
        us = st.pop("u")
        for g in st["groups"]:
            lo, hi = g * FNET_GROUP_DIM - st["cols"].start, (g + 1) * FNET_GROUP_DIM - st["cols"].start
            z = jnp.concatenate(
                [jnp.concatenate([ur[:, lo:hi], ui[:, lo:hi]], axis=1).astype(BF16) for ur, ui in us], axis=0)
            out = slice(g * FNET_GROUP_DIM, (g + 1) * FNET_GROUP_DIM)
            o_ref[:, out] = _dot(z, e_ref[g]) + bf_ref[:, out]

    halves = [dict(cols=slice(j * FFT_WIDTH, (j + 1) * FFT_WIDTH),
                   groups=range(j * FFT_WIDTH // FNET_GROUP_DIM, (j + 1) * FFT_WIDTH // FNET_GROUP_DIM))
              for j in range(FNET_WIDTH // FFT_WIDTH)]
    _run_skewed((block_dfts, across_blocks, channel_map), halves)


def _fourier_kernel(a_ref, m_ref, e_ref, bf_ref, o_ref, *, n1):
    _fourier_body(lambda q, cols: a_ref[q, :, cols], m_ref, e_ref, bf_ref, o_ref, n1)


def _front_kernel(x_ref, mod_ref, g_ref, win_ref, perm_ref, m_ref, e_ref, bf_ref, o_ref, *, n1):
    blocks = _regrouped_blocks(_fnet_in(x_ref, mod_ref, g_ref, win_ref), perm_ref, n1)
    _fourier_body(lambda q, cols: blocks[q][:, cols], m_ref, e_ref, bf_ref, o_ref, n1)


def _front_call(x, mod, norm_g, w_in, perm, m_tab, e_tab, b_fmix, n1):
    B, S, _ = x.shape
    const2 = lambda b: (0, 0)
    const3 = lambda b: (0, 0, 0)
    return pl.pallas_call(
        functools.partial(_front_kernel, n1=n1),
        grid=(B,),
        in_specs=[
            pl.BlockSpec((None, S, D_MODEL), lambda b: (b, 0, 0)),
            pl.BlockSpec((None, 3, D_MODEL), lambda b: (b, 0, 0)),
            pl.BlockSpec((1, D_MODEL), const2),
            pl.BlockSpec((D_MODEL, FNET_WIDTH), lambda b: (0, _A[0] // FNET_WIDTH)),
            pl.BlockSpec((SEQ_TILE, SEQ_TILE), const2),
            pl.BlockSpec((n1, 2 * DFT_LEN, DFT_LEN), const3),
            pl.BlockSpec((FNET_GROUPS, 2 * FNET_GROUP_DIM, FNET_GROUP_DIM), const3),
            pl.BlockSpec((1, FNET_WIDTH), const2),
        ],
        out_specs=pl.BlockSpec((None, S, FNET_WIDTH), lambda b: (b, 0, 0)),
        out_shape=jax.ShapeDtypeStruct((B, S, FNET_WIDTH), F32),
        compiler_params=pltpu.CompilerParams(
            dimension_semantics=("arbitrary",),
            vmem_limit_bytes=VMEM_LIMIT_BYTES),
        name="fnet_front",
    )(x, mod, norm_g, w_in, perm, m_tab, e_tab, b_fmix)


def _fourier_call(a_perm, m_tab, e_tab, b_fmix):
    B, n1, _, _ = a_perm.shape
    S = n1 * DFT_LEN
    return pl.pallas_call(
        functools.partial(_fourier_kernel, n1=n1),
        grid=(B,),
        in_specs=[
            pl.BlockSpec((None, n1, DFT_LEN, FNET_WIDTH), lambda b: (b, 0, 0, 0)),
            pl.BlockSpec((n1, 2 * DFT_LEN, DFT_LEN), lambda b: (0, 0, 0)),
            pl.BlockSpec((FNET_GROUPS, 2 * FNET_GROUP_DIM, FNET_GROUP_DIM), lambda b: (0, 0, 0)),
            pl.BlockSpec((1, FNET_WIDTH), lambda b: (0, 0)),
        ],
        out_specs=pl.BlockSpec((None, S, FNET_WIDTH), lambda b: (b, 0, 0)),
        out_shape=jax.ShapeDtypeStruct((B, S, FNET_WIDTH), F32),
        compiler_params=pltpu.CompilerParams(
            dimension_semantics=("arbitrary",),
            vmem_limit_bytes=VMEM_LIMIT_BYTES),
        name="fnet_fft",
    )(a_perm, m_tab, e_tab, b_fmix)


def _main_kernel(x_ref, mod_ref, ya_ref, g_ref, win_ref, lng_ref, lnb_ref, ws_ref, bs_ref,
                 wpa_ref, wpb_ref, wo_ref, fg_ref, o_ref, *, tm):
    def proj(st, rng):
        return _dot(st["h"], win_ref[:, rng[0]:rng[1]])

    def load_norm(st):
        st["x"] = x_ref[st["rows"], :]
        st["h"] = _modulated_norm(st["x"], mod_ref, g_ref).astype(BF16)

    def project(st):
        st["v"] = proj(st, _V)
        st["ga"] = proj(st, _GA)
        st["ma"] = proj(st, _MA)

    def project2(st):
        st["u"] = proj(st, _U)
        st["gb"] = proj(st, _GB)

    def layernorm(st):
        v = st.pop("v")
        mu = jnp.mean(v, axis=-1, keepdims=True)
        vc = v - mu
        var = jnp.mean(vc * vc, axis=-1, keepdims=True)
        st["vln"] = (vc * lax.rsqrt(var + EPS) * lng_ref[...] + lnb_ref[...]).astype(BF16)

    def spatial_mix(st):
        vln = st.pop("vln")
        rows = []
        for c in range(MAIN_SUB // CHUNK):
            r0 = c * CHUNK
            cols = []
            for hd in range(SGU_HEADS):
                lo, hi = hd * SGU_HEAD_DIM, (hd + 1) * SGU_HEAD_DIM
                cols.append(_dot(ws_ref[hd], vln[r0:r0 + CHUNK, lo:hi]))
            rows.append(jnp.concatenate(cols, axis=1) + bs_ref[...])
        st["mixed"] = jnp.concatenate(rows, axis=0)

    def fourier_path(st):
        y_a = (ya_ref[st["rows"], :] * _silu(st.pop("ga"))).astype(BF16)
        st["merged"] = jax.nn.sigmoid(st.pop("ma")) * _dot(y_a, wpa_ref[...])
        st["mb"] = proj(st, _MB)

    def gating_path(st):
        y_b = (st.pop("u") * st.pop("mixed") * _silu(st.pop("gb"))).astype(BF16)
        st["merged"] = st["merged"] + jax.nn.sigmoid(st.pop("mb")) * _dot(y_b, wpb_ref[...])

    def out_proj(st):
        st["out"] = _dot(st.pop("merged").astype(BF16), wo_ref[...])

    def residual(st):
        st["xo"] = st.pop("x") + mod_ref[2:3, :] * st.pop("out")
        st["ms"] = jnp.mean(st["xo"] * st["xo"], axis=-1, keepdims=True)

    def finish(st):
        o_ref[st["rows"], :] = st.pop("xo") * lax.rsqrt(st.pop("ms") + EPS) * fg_ref[...]

    stages = (load_norm, project, project2, layernorm, spatial_mix, fourier_path, gating_path, out_proj,
              residual, finish)
    subs = [dict(rows=slice(i * MAIN_SUB, (i + 1) * MAIN_SUB)) for i in range(tm // MAIN_SUB)]
    _run_skewed(stages, subs)


def _main_call(x, mod, ya, norm_g, w_in, ln_g, ln_b, w_s, bs_full, w_pa, w_pb, w_out,
               final_g, tm):
    B, S, _ = x.shape
    const2 = lambda b, i: (0, 0)
    return pl.pallas_call(
        functools.partial(_main_kernel, tm=tm),
        grid=(B, S // tm),
        in_specs=[
            pl.BlockSpec((None, tm, D_MODEL), lambda b, i: (b, i, 0)),
            pl.BlockSpec((None, 3, D_MODEL), lambda b, i: (b, 0, 0)),
            pl.BlockSpec((None, tm, FNET_WIDTH), lambda b, i: (b, i, 0)),
            pl.BlockSpec((1, D_MODEL), const2),
            pl.BlockSpec((D_MODEL, IN_WIDTH), const2),
            pl.BlockSpec((1, SGU_WIDTH), const2),
            pl.BlockSpec((1, SGU_WIDTH), const2),
            pl.BlockSpec((SGU_HEADS, CHUNK, CHUNK), lambda b, i: (0, 0, 0)),
            pl.BlockSpec((CHUNK, SGU_WIDTH), const2),
            pl.BlockSpec((FNET_WIDTH, D_MODEL), const2),
            pl.BlockSpec((SGU_WIDTH, D_MODEL), const2),
            pl.BlockSpec((D_MODEL, D_MODEL), const2),
            pl.BlockSpec((1, D_MODEL), const2),
        ],
        out_specs=pl.BlockSpec((None, tm, D_MODEL), lambda b, i: (b, i, 0)),
        out_shape=jax.ShapeDtypeStruct((B, S, D_MODEL), F32),
        compiler_params=pltpu.CompilerParams(
            dimension_semantics=("arbitrary", "arbitrary"),
            vmem_limit_bytes=VMEM_LIMIT_BYTES),
        name="encoder_main",
    )(x, mod, ya, norm_g, w_in, ln_g, ln_b, w_s, bs_full, w_pa, w_pb, w_out, final_g)


def _channel_dft_table():
    c = np.arange(FNET_GROUP_DIM)
    ang = 2.0 * np.pi * ((c[:, None] * c[None, :]) % FNET_GROUP_DIM) / FNET_GROUP_DIM
    scale = 1.0 / math.sqrt(FNET_GROUP_DIM)
    return np.concatenate([np.cos(ang), np.sin(ang)], axis=0) * scale


def _regroup_matrix(n1):
    rows = SEQ_TILE // n1
    p = np.zeros((SEQ_TILE, SEQ_TILE), np.float32)
    for q in range(n1):
        for j in range(rows):
            p[q * rows + j, j * n1 + q] = 1.0
    return p


def _shared_dft_factors():
    k2 = np.arange(DFT_LEN)
    th2 = 2.0 * np.pi * ((k2[:, None] * k2[None, :]) % DFT_LEN) / DFT_LEN
    return jnp.asarray(np.cos(th2), F32), jnp.asarray(np.sin(th2), F32)


def _block_dft_factors(n1):
    S = n1 * DFT_LEN
    th1 = 2.0 * np.pi * ((np.arange(DFT_LEN)[None, :] * np.arange(n1)[:, None]) % S) / S
    scale = 1.0 / math.sqrt(S)
    return (jnp.asarray(np.cos(th1) * scale, F32)[:, :, None], jnp.asarray(np.sin(th1) * scale, F32)[:, :, None])


def _trunk(x, mod, wts):
    B, S, _ = x.shape
    assert S % DFT_LEN == 0
    n1 = S // DFT_LEN
    assert n1 & (n1 - 1) == 0 and SEQ_TILE % n1 == 0 and S % PRE_TILE == 0 and S % MAIN_TILE == 0
    perm = jnp.asarray(_regroup_matrix(n1), F32).astype(BF16)
    m_tab = wts["m_tab"][n1]
    if S == PRE_TILE:
        ya = _front_call(x, mod, wts["norm_g"], wts["w_in"], perm, m_tab, wts["e_tab"], wts["b_fmix"], n1)
    else:
        a_perm = _pre_call(x, mod, wts["norm_g"], wts["w_in"], perm, n1)
        ya = _fourier_call(a_perm, m_tab, wts["e_tab"], wts["b_fmix"])
    return _main_call(x, mod, ya, wts["norm_g"], wts["w_in"], wts["ln_g"], wts["ln_b"],
                      wts["w_s"], wts["bs_full"], wts["w_pa"], wts["w_pb"], wts["w_out"],
                      wts["final_g"], MAIN_TILE)


def kernel(x_prompt, x_sample, c_prompt, c_sample, norm_g, w_ada, b_ada, w_in, w_fmix, b_fmix,
           sgu_ln_g, sgu_ln_b, w_s, b_s, w_pa, w_pb, w_out, final_g):
    depth = norm_g.shape[0]
    assert depth == 1
    l = 0
    nb_p = x_prompt.shape[0]
    (mod, e_tab, w_s_bf, w_in_bf, w_pa_bf, w_pb_bf, w_out_bf), m_tabs = _prepare(
        jnp.concatenate([c_prompt, c_sample], axis=0), w_ada[l], b_ada[l],
        jnp.asarray(_channel_dft_table(), F32), w_fmix[l], w_s[l], w_in[l], w_pa[l], w_pb[l], w_out[l],
        seq_factors={x.shape[1] // DFT_LEN for x in (x_prompt, x_sample)})
    mod = mod.reshape(-1, 3, D_MODEL)
    wts = dict(
        norm_g=norm_g[l].reshape(1, -1),
        w_in=w_in_bf,
        e_tab=e_tab,
        m_tab=m_tabs,
        b_fmix=b_fmix[l].reshape(1, -1),
        ln_g=sgu_ln_g[l].reshape(1, -1),
        ln_b=sgu_ln_b[l].reshape(1, -1),
        w_s=w_s_bf,
        bs_full=jnp.repeat(b_s[l].T, SGU_HEAD_DIM, axis=1),
        w_pa=w_pa_bf,
        w_pb=w_pb_bf,
        w_out=w_out_bf,
        final_g=final_g.reshape(1, -1),
    )
    y_prompt = _trunk(x_prompt, mod[:nb_p], wts)
    y_sample = _trunk(x_sample, mod[nb_p:], wts)
    return (y_prompt, y_sample)
```

```python
import functools
import math

import numpy as np
import jax
import jax.numpy as jnp
from jax import lax
from jax.experimental import pallas as pl
from jax.experimental.pallas import tpu as pltpu

D_MODEL = 1024
FNET_WIDTH = 512
FNET_GROUPS = 4
FNET_GROUP_DIM = 128
SGU_WIDTH = 512
SGU_HEADS = 4
SGU_HEAD_DIM = 128
CHUNK = 128
EPS = 1e-6
PREP_STEPS = 4
DFT_LEN = 256
SEQ_TILE = 256
PRE_TILE = 2048
MAIN_TILE = 1024
MAIN_SUB = 256
FFT_WIDTH = 256

_IN_SIZES = (FNET_WIDTH, FNET_WIDTH, SGU_WIDTH, SGU_WIDTH, SGU_WIDTH, D_MODEL, D_MODEL)
_IN_ENDS = tuple(int(v) for v in np.cumsum(_IN_SIZES))
_A, _GA, _U, _V, _GB, _MA, _MB = ((end - size, end) for size, end in zip(_IN_SIZES, _IN_ENDS))
IN_WIDTH = _IN_ENDS[-1]

BF16 = jnp.bfloat16
F32 = jnp.float32

VMEM_LIMIT_BYTES = 56 * 1024 * 1024


def _dot(a, b):
    return jnp.dot(a, b, preferred_element_type=F32)


def _silu(x):
    return x * jax.nn.sigmoid(x)


def _split_bf16(x):
    hi = x.astype(BF16)
    return hi, (x - hi.astype(F32)).astype(BF16)


def _dot_3pass(a, b_split):
    a_hi, a_lo = _split_bf16(a)
    b_hi, b_lo = b_split
    return _dot(a_hi, b_hi) + (_dot(a_lo, b_hi) + _dot(a_hi, b_lo))


def _prep_kernel(*refs, n_cond, table_blocks):
    refs = list(refs)
    take = lambda k: [refs.pop(0) for _ in range(k)]
    c_refs = take(n_cond)
    wada_ref, bada_ref, cs_ref, wf_ref, ws_ref, win_ref, wpa_ref, wpb_ref, wo_ref, c2_ref, s2_ref = take(11)
    factor_refs = take(2 * len(table_blocks))
    mod_refs = take(n_cond)
    e_ref, ws_o, win_o, wpa_o, wpb_o, wo_o = take(6)
    table_refs = refs

    cond = jnp.concatenate([c_ref[...] for c_ref in c_refs], axis=0)
    mod = _dot_3pass(_silu(cond), _split_bf16(wada_ref[...])) + bada_ref[...]
    row = 0
    for mod_ref in mod_refs:
        mod_ref[...] = mod[row:row + mod_ref.shape[0]]
        row += mod_ref.shape[0]
    win_o[...] = win_ref[...].astype(BF16)
    wpa_o[...] = wpa_ref[...].astype(BF16)
    wpb_o[...] = wpb_ref[...].astype(BF16)
    wo_o[...] = wo_ref[...].astype(BF16)

    c2, s2 = c2_ref[...], s2_ref[...]
    for t, table_ref in enumerate(table_refs):
        c1_ref, s1_ref = factor_refs[2 * t], factor_refs[2 * t + 1]
        for q in range(table_blocks[t]):
            c1, s1 = c1_ref[q], s1_ref[q]
            table_ref[q, 0:DFT_LEN, :] = (c1 * c2 - s1 * s2).astype(BF16)
            table_ref[q, DFT_LEN:, :] = (-(s1 * c2 + c1 * s2)).astype(BF16)

    @pl.when(pl.program_id(0) == 0)
    def _():
        ws_o[...] = ws_ref[...].astype(BF16)
        for g in range(FNET_GROUPS):
            e_ref[g] = _dot_3pass(cs_ref[...], _split_bf16(wf_ref[g])).astype(BF16)


def _prepare(conds, w_ada, b_ada, cs_c, w_fmix, w_s, w_in, w_pa, w_pb, w_out, seq_factors):
    e_shape = (FNET_GROUPS, 2 * FNET_GROUP_DIM, FNET_GROUP_DIM)
    whole = lambda shape: pl.BlockSpec(shape, lambda j: (0,) * len(shape))
    cols = lambda a: pl.BlockSpec((a.shape[0], a.shape[1] // PREP_STEPS), lambda j: (0, j))
    lead = lambda shape: pl.BlockSpec((shape[0] // PREP_STEPS,) + shape[1:], lambda j: (j, 0, 0))
    streamed = (w_in, w_pa, w_pb, w_out)
    assert all(a.shape[1] % (PREP_STEPS * FNET_GROUP_DIM) == 0 for a in (w_ada,) + streamed)
    n1s = sorted(seq_factors)
    assert all(n1 % PREP_STEPS == 0 for n1 in n1s)
    c2, s2 = _shared_dft_factors()
    factors = [f for n1 in n1s for f in _block_dft_factors(n1)]
    t_shapes = [(n1, 2 * DFT_LEN, DFT_LEN) for n1 in n1s]
    mod_cols = w_ada.shape[1] // PREP_STEPS
    outs = pl.pallas_call(
        functools.partial(_prep_kernel, n_cond=len(conds), table_blocks=tuple(n1 // PREP_STEPS for n1 in n1s)),
        grid=(PREP_STEPS,),
        in_specs=[whole(c.shape) for c in conds]
                 + [cols(w_ada), pl.BlockSpec((1, mod_cols), lambda j: (0, j)),
                    whole(cs_c.shape), whole(w_fmix.shape), whole(w_s.shape)] + [cols(a) for a in streamed]
                 + [whole(c2.shape), whole(s2.shape)] + [lead(f.shape) for f in factors],
        out_specs=[pl.BlockSpec((c.shape[0], mod_cols), lambda j: (0, j)) for c in conds]
                  + [whole(e_shape), whole(w_s.shape)] + [cols(a) for a in streamed] + [lead(t) for t in t_shapes],
        out_shape=[jax.ShapeDtypeStruct((c.shape[0], w_ada.shape[1]), F32) for c in conds]
                  + [jax.ShapeDtypeStruct(e_shape, BF16), jax.ShapeDtypeStruct(w_s.shape, BF16)]
                  + [jax.ShapeDtypeStruct(a.shape, BF16) for a in streamed]
                  + [jax.ShapeDtypeStruct(t, BF16) for t in t_shapes],
        compiler_params=pltpu.CompilerParams(
            dimension_semantics=("arbitrary",),
            vmem_limit_bytes=VMEM_LIMIT_BYTES),
        name="adaln_prep",
    )(*conds, w_ada, b_ada.reshape(1, -1), cs_c, w_fmix, w_s, *streamed, c2, s2, *factors)
    k = len(conds)
    return outs[:k], outs[k:k + 6], dict(zip(n1s, outs[k + 6:]))


def _mod_row(mod_ref, part):
    return mod_ref[pl.ds(pl.program_id(0), 1), part * D_MODEL:(part + 1) * D_MODEL]


def _modulated_norm(x, mod_ref, g_ref):
    ms = jnp.mean(x * x, axis=-1, keepdims=True)
    xn = x * lax.rsqrt(ms + EPS) * g_ref[...]
    return xn * (1.0 + _mod_row(mod_ref, 1)) + _mod_row(mod_ref, 0)


def _regrouped_blocks(a, perm_ref, n1):
    rows = SEQ_TILE // n1
    aps = [_dot(perm_ref[...], a[s * SEQ_TILE:(s + 1) * SEQ_TILE]).astype(BF16)
           for s in range(a.shape[0] // SEQ_TILE)]
    return [jnp.concatenate([ap[q * rows:(q + 1) * rows] for ap in aps], axis=0) for q in range(n1)]


def _fnet_in(x_ref, mod_ref, g_ref, wa_ref):
    h = _modulated_norm(x_ref[...], mod_ref, g_ref).astype(BF16)
    return _dot(h, wa_ref[...]).astype(BF16)


def _pre_kernel(x_ref, mod_ref, g_ref, win_ref, perm_ref, a_ref, *, n1):
    for q, block in enumerate(_regrouped_blocks(_fnet_in(x_ref, mod_ref, g_ref, win_ref), perm_ref, n1)):
        a_ref[q] = block


def _pre_call(x, mod, norm_g, w_in, perm, n1):
    B, S, _ = x.shape
    rows = PRE_TILE // n1
    return pl.pallas_call(
        functools.partial(_pre_kernel, n1=n1),
        grid=(B, S // PRE_TILE),
        in_specs=[
            pl.BlockSpec((None, PRE_TILE, D_MODEL), lambda b, i: (b, i, 0)),
            pl.BlockSpec(mod.shape, lambda b, i: (0, 0)),
            pl.BlockSpec((1, D_MODEL), lambda b, i: (0, 0)),
            pl.BlockSpec((D_MODEL, FNET_WIDTH), lambda b, i: (0, _A[0] // FNET_WIDTH)),
            pl.BlockSpec((SEQ_TILE, SEQ_TILE), lambda b, i: (0, 0)),
        ],
        out_specs=pl.BlockSpec((None, n1, rows, FNET_WIDTH), lambda b, i: (b, 0, i, 0)),
        out_shape=jax.ShapeDtypeStruct((B, n1, DFT_LEN, FNET_WIDTH), BF16),
        compiler_params=pltpu.CompilerParams(
            dimension_semantics=("arbitrary", "arbitrary"),
            vmem_limit_bytes=VMEM_LIMIT_BYTES),
        name="fnet_pre",
    )(x, mod, norm_g, w_in, perm)


def _cmul_const(z, w):
    zr, zi = z
    wr, wi = w.real, w.imag
    tol = 1e-12
    if abs(wi) < tol:
        return (zr, zi) if abs(wr - 1) < tol else (zr * wr, zi * wr)
    if abs(wr) < tol:
        if abs(wi + 1) < tol:
            return (zi, -zr)
        if abs(wi - 1) < tol:
            return (-zi, zr)
    if abs(abs(wr) - abs(wi)) < tol:
        c = abs(wr)
        sr, si = math.copysign(1.0, wr), math.copysign(1.0, wi)
        re = (zr if sr > 0 else -zr) - (zi if si > 0 else -zi)
        im = (zi if sr > 0 else -zi) + (zr if si > 0 else -zr)
        return (re * c, im * c)
    return (zr * wr - zi * wi, zr * wi + zi * wr)


def _block_fft(zs):
    n = len(zs)
    if n == 1:
        return zs
    even = _block_fft(zs[0::2])
    odd = _block_fft(zs[1::2])
    out = [None] * n
    for k in range(n // 2):
        w = complex(math.cos(2 * math.pi * k / n), -math.sin(2 * math.pi * k / n))
        tr, ti = _cmul_const(odd[k], w)
        er, ei = even[k]
        out[k] = (er + tr, ei + ti)
        out[k + n // 2] = (er - tr, ei - ti)
    return out


def _run_skewed(stages, states):
    for t in range(len(stages) + len(states) - 1):
        for i, st in enumerate(states):
            if 0 <= t - i < len(stages):
                stages[t - i](st)


def _fourier_body(a_block, m_ref, e_ref, bf_ref, o_ref, n1):
    def block_dfts(st):
        st["g"] = []
        for q in range(n1):
            g = _dot(m_ref[q], a_block(q, st["cols"]))
            st["g"].append((g[:DFT_LEN], g[DFT_LEN:]))

    def across_blocks(st):
        st["u"] = _block_fft(st.pop("g"))

    def channel_map(st):
        us = st.pop("u")
        for g in st["groups"]:
            lo, hi = g * FNET_GROUP_DIM - st["cols"].start, (g + 1) * FNET_GROUP_DIM - st["cols"].start
            z = jnp.concatenate(
                [jnp.concatenate([ur[:, lo:hi], ui[:, lo:hi]], axis=1).astype(BF16) for ur, ui in us], axis=0)
            out = slice(g * FNET_GROUP_DIM, (g + 1) * FNET_GROUP_DIM)
            o_ref[:, out] = _dot(z, e_ref[g]) + bf_ref[:, out]

    halves = [dict(cols=slice(j * FFT_WIDTH, (j + 1) * FFT_WIDTH),
                   groups=range(j * FFT_WIDTH // FNET_GROUP_DIM, (j + 1) * FFT_WIDTH // FNET_GROUP_DIM))
              for j in range(FNET_WIDTH // FFT_WIDTH)]
    _run_skewed((block_dfts, across_blocks, channel_map), halves)


def _fourier_kernel(a_ref, m_ref, e_ref, bf_ref, o_ref, *, n1):
    _fourier_body(lambda q, cols: a_ref[q, :, cols], m_ref, e_ref, bf_ref, o_ref, n1)


def _front_kernel(x_ref, mod_ref, g_ref, win_ref, perm_ref, m_ref, e_ref, bf_ref, o_ref, *, n1):
    blocks = _regrouped_blocks(_fnet_in(x_ref, mod_ref, g_ref, win_ref), perm_ref, n1)
    _fourier_body(lambda q, cols: blocks[q][:, cols], m_ref, e_ref, bf_ref, o_ref, n1)


def _front_call(x, mod, norm_g, w_in, perm, m_tab, e_tab, b_fmix, n1):
    B, S, _ = x.shape
    const2 = lambda b: (0, 0)
    const3 = lambda b: (0, 0, 0)
    return pl.pallas_call(
        functools.partial(_front_kernel, n1=n1),
        grid=(B,),
        in_specs=[
            pl.BlockSpec((None, S, D_MODEL), lambda b: (b, 0, 0)),
            pl.BlockSpec(mod.shape, lambda b: (0, 0)),
            pl.BlockSpec((1, D_MODEL), const2),
            pl.BlockSpec((D_MODEL, FNET_WIDTH), lambda b: (0, _A[0] // FNET_WIDTH)),
            pl.BlockSpec((SEQ_TILE, SEQ_TILE), const2),
            pl.BlockSpec((n1, 2 * DFT_LEN, DFT_LEN), const3),
            pl.BlockSpec((FNET_GROUPS, 2 * FNET_GROUP_DIM, FNET_GROUP_DIM), const3),
            pl.BlockSpec((1, FNET_WIDTH), const2),
        ],
        out_specs=pl.BlockSpec((None, S, FNET_WIDTH), lambda b: (b, 0, 0)),
        out_shape=jax.ShapeDtypeStruct((B, S, FNET_WIDTH), F32),
        compiler_params=pltpu.CompilerParams(
            dimension_semantics=("arbitrary",),
            vmem_limit_bytes=VMEM_LIMIT_BYTES),
        name="fnet_front",
    )(x, mod, norm_g, w_in, perm, m_tab, e_tab, b_fmix)


def _fourier_call(a_perm, m_tab, e_tab, b_fmix):
    B, n1, _, _ = a_perm.shape
    S = n1 * DFT_LEN
    return pl.pallas_call(
        functools.partial(_fourier_kernel, n1=n1),
        grid=(B,),
        in_specs=[
            pl.BlockSpec((None, n1, DFT_LEN, FNET_WIDTH), lambda b: (b, 0, 0, 0)),
            pl.BlockSpec((n1, 2 * DFT_LEN, DFT_LEN), lambda b: (0, 0, 0)),
            pl.BlockSpec((FNET_GROUPS, 2 * FNET_GROUP_DIM, FNET_GROUP_DIM), lambda b: (0, 0, 0)),
            pl.BlockSpec((1, FNET_WIDTH), lambda b: (0, 0)),
        ],
        out_specs=pl.BlockSpec((None, S, FNET_WIDTH), lambda b: (b, 0, 0)),
        out_shape=jax.ShapeDtypeStruct((B, S, FNET_WIDTH), F32),
        compiler_params=pltpu.CompilerParams(
            dimension_semantics=("arbitrary",),
            vmem_limit_bytes=VMEM_LIMIT_BYTES),
        name="fnet_fft",
    )(a_perm, m_tab, e_tab, b_fmix)


def _main_kernel(x_ref, mod_ref, ya_ref, g_ref, win_ref, lng_ref, lnb_ref, ws_ref, bs_ref,
                 wpa_ref, wpb_ref, wo_ref, fg_ref, o_ref, *, tm):
    def proj(st, rng):
        return _dot(st["h"], win_ref[:, rng[0]:rng[1]])

    def load_norm(st):
        st["x"] = x_ref[st["rows"], :]
        st["h"] = _modulated_norm(st["x"], mod_ref, g_ref).astype(BF16)

    def project(st):
        st["v"] = proj(st, _V)
        st["ga"] = proj(st, _GA)
        st["ma"] = proj(st, _MA)

    def project2(st):
        st["u"] = proj(st, _U)
        st["gb"] = proj(st, _GB)

    def layernorm(st):
        v = st.pop("v")
        mu = jnp.mean(v, axis=-1, keepdims=True)
        vc = v - mu
        var = jnp.mean(vc * vc, axis=-1, keepdims=True)
        st["vln"] = (vc * lax.rsqrt(var + EPS) * lng_ref[...] + lnb_ref[...]).astype(BF16)

    def spatial_mix(st):
        vln = st.pop("vln")
        rows = []
        for c in range(MAIN_SUB // CHUNK):
            r0 = c * CHUNK
            cols = []
            for hd in range(SGU_HEADS):
                lo, hi = hd * SGU_HEAD_DIM, (hd + 1) * SGU_HEAD_DIM
                cols.append(_dot(ws_ref[hd], vln[r0:r0 + CHUNK, lo:hi]))
            rows.append(jnp.concatenate(cols, axis=1) + bs_ref[...])
        st["mixed"] = jnp.concatenate(rows, axis=0)

    def fourier_path(st):
        y_a = (ya_ref[st["rows"], :] * _silu(st.pop("ga"))).astype(BF16)
        st["merged"] = jax.nn.sigmoid(st.pop("ma")) * _dot(y_a, wpa_ref[...])
        st["mb"] = proj(st, _MB)

    def gating_path(st):
        y_b = (st.pop("u") * st.pop("mixed") * _silu(st.pop("gb"))).astype(BF16)
        st["merged"] = st["merged"] + jax.nn.sigmoid(st.pop("mb")) * _dot(y_b, wpb_ref[...])

    def out_proj(st):
        st["mg"] = st.pop("merged").astype(BF16)
        st["out0"] = _dot(st["mg"], wo_ref[:, 0:D_MODEL // 2])

    def out_proj2(st):
        st["out"] = jnp.concatenate([st.pop("out0"), _dot(st.pop("mg"), wo_ref[:, D_MODEL // 2:])], axis=1)

    def residual(st):
        st["xo"] = st.pop("x") + _mod_row(mod_ref, 2) * st.pop("out")
        st["ms"] = jnp.mean(st["xo"] * st["xo"], axis=-1, keepdims=True)

    def finish(st):
        o_ref[st["rows"], :] = st.pop("xo") * lax.rsqrt(st.pop("ms") + EPS) * fg_ref[...]

    stages = (load_norm, project, layernorm, project2, spatial_mix, fourier_path, gating_path, out_proj, out_proj2,
              residual, finish)
    subs = [dict(rows=slice(i * MAIN_SUB, (i + 1) * MAIN_SUB)) for i in range(tm // MAIN_SUB)]
    _run_skewed(stages, subs)


def _main_call(x, mod, ya, norm_g, w_in, ln_g, ln_b, w_s, bs_full, w_pa, w_pb, w_out,
               final_g, tm):
    B, S, _ = x.shape
    const2 = lambda b, i: (0, 0)
    return pl.pallas_call(
        functools.partial(_main_kernel, tm=tm),
        grid=(B, S // tm),
        in_specs=[
            pl.BlockSpec((None, tm, D_MODEL), lambda b, i: (b, i, 0)),
            pl.BlockSpec(mod.shape, lambda b, i: (0, 0)),
            pl.BlockSpec((None, tm, FNET_WIDTH), lambda b, i: (b, i, 0)),
            pl.BlockSpec((1, D_MODEL), const2),
            pl.BlockSpec((D_MODEL, IN_WIDTH), const2),
            pl.BlockSpec((1, SGU_WIDTH), const2),
            pl.BlockSpec((1, SGU_WIDTH), const2),
            pl.BlockSpec((SGU_HEADS, CHUNK, CHUNK), lambda b, i: (0, 0, 0)),
            pl.BlockSpec((CHUNK, SGU_WIDTH), const2),
            pl.BlockSpec((FNET_WIDTH, D_MODEL), const2),
            pl.BlockSpec((SGU_WIDTH, D_MODEL), const2),
            pl.BlockSpec((D_MODEL, D_MODEL), const2),
            pl.BlockSpec((1, D_MODEL), const2),
        ],
        out_specs=pl.BlockSpec((None, tm, D_MODEL), lambda b, i: (b, i, 0)),
        out_shape=jax.ShapeDtypeStruct((B, S, D_MODEL), F32),
        compiler_params=pltpu.CompilerParams(
            dimension_semantics=("arbitrary", "arbitrary"),
            vmem_limit_bytes=VMEM_LIMIT_BYTES),
        name="encoder_main",
    )(x, mod, ya, norm_g, w_in, ln_g, ln_b, w_s, bs_full, w_pa, w_pb, w_out, final_g)


def _channel_dft_table():
    c = np.arange(FNET_GROUP_DIM)
    ang = 2.0 * np.pi * ((c[:, None] * c[None, :]) % FNET_GROUP_DIM) / FNET_GROUP_DIM
    scale = 1.0 / math.sqrt(FNET_GROUP_DIM)
    return np.concatenate([np.cos(ang), np.sin(ang)], axis=0) * scale


def _regroup_matrix(n1):
    rows = SEQ_TILE // n1
    p = np.zeros((SEQ_TILE, SEQ_TILE), np.float32)
    for q in range(n1):
        for j in range(rows):
            p[q * rows + j, j * n1 + q] = 1.0
    return p


def _shared_dft_factors():
    k2 = np.arange(DFT_LEN)
    th2 = 2.0 * np.pi * ((k2[:, None] * k2[None, :]) % DFT_LEN) / DFT_LEN
    return jnp.asarray(np.cos(th2), F32), jnp.asarray(np.sin(th2), F32)


def _block_dft_factors(n1):
    S = n1 * DFT_LEN
    th1 = 2.0 * np.pi * ((np.arange(DFT_LEN)[None, :] * np.arange(n1)[:, None]) % S) / S
    scale = 1.0 / math.sqrt(S)
    return (jnp.asarray(np.cos(th1) * scale, F32)[:, :, None], jnp.asarray(np.sin(th1) * scale, F32)[:, :, None])


def _trunk(x, mod, wts):
    B, S, _ = x.shape
    assert S % DFT_LEN == 0
    n1 = S // DFT_LEN
    assert n1 & (n1 - 1) == 0 and SEQ_TILE % n1 == 0 and S % PRE_TILE == 0 and S % MAIN_TILE == 0
    perm = jnp.asarray(_regroup_matrix(n1), F32).astype(BF16)
    m_tab = wts["m_tab"][n1]
    if S == PRE_TILE:
        ya = _front_call(x, mod, wts["norm_g"], wts["w_in"], perm, m_tab, wts["e_tab"], wts["b_fmix"], n1)
    else:
        a_perm = _pre_call(x, mod, wts["norm_g"], wts["w_in"], perm, n1)
        ya = _fourier_call(a_perm, m_tab, wts["e_tab"], wts["b_fmix"])
    return _main_call(x, mod, ya, wts["norm_g"], wts["w_in"], wts["ln_g"], wts["ln_b"],
                      wts["w_s"], wts["bs_full"], wts["w_pa"], wts["w_pb"], wts["w_out"],
                      wts["final_g"], MAIN_TILE)


def kernel(x_prompt, x_sample, c_prompt, c_sample, norm_g, w_ada, b_ada, w_in, w_fmix, b_fmix,
           sgu_ln_g, sgu_ln_b, w_s, b_s, w_pa, w_pb, w_out, final_g):
    depth = norm_g.shape[0]
    assert depth == 1
    l = 0
    (mod_p, mod_s), (e_tab, w_s_bf, w_in_bf, w_pa_bf, w_pb_bf, w_out_bf), m_tabs = _prepare(
        (c_prompt, c_sample), w_ada[l], b_ada[l],
        jnp.asarray(_channel_dft_table(), F32), w_fmix[l], w_s[l], w_in[l], w_pa[l], w_pb[l], w_out[l],
        seq_factors={x.shape[1] // DFT_LEN for x in (x_prompt, x_sample)})
    wts = dict(
        norm_g=norm_g[l].reshape(1, -1),
        w_in=w_in_bf,
        e_tab=e_tab,
        m_tab=m_tabs,
        b_fmix=b_fmix[l].reshape(1, -1),
        ln_g=sgu_ln_g[l].reshape(1, -1),
        ln_b=sgu_ln_b[l].reshape(1, -1),
        w_s=w_s_bf,
        bs_full=jnp.repeat(b_s[l].T, SGU_HEAD_DIM, axis=1),
        w_pa=w_pa_bf,
        w_pb=w_pb_bf,
        w_out=w_out_bf,
        final_g=final_g.reshape(1, -1),
    )
    y_prompt = _trunk(x_prompt, mod_p, wts)
    y_sample = _trunk(x_sample, mod_s, wts)
    return (y_prompt, y_sample)
```

```python
import functools
import math

import numpy as np
import jax
import jax.numpy as jnp
from jax import lax
from jax.experimental import pallas as pl
from jax.experimental.pallas import tpu as pltpu

D_MODEL = 1024
FNET_WIDTH = 512
FNET_GROUPS = 4
FNET_GROUP_DIM = 128
SGU_WIDTH = 512
SGU_HEADS = 4
SGU_HEAD_DIM = 128
CHUNK = 128
EPS = 1e-6
PREP_STEPS = 4
DFT_LEN = 256
SEQ_TILE = 256
PRE_TILE = 2048
MAIN_TILE = 1024
MAIN_SUB = 256
FFT_WIDTH = 256

_IN_SIZES = (FNET_WIDTH, FNET_WIDTH, SGU_WIDTH, SGU_WIDTH, SGU_WIDTH, D_MODEL, D_MODEL)
_IN_ENDS = tuple(int(v) for v in np.cumsum(_IN_SIZES))
_A, _GA, _U, _V, _GB, _MA, _MB = ((end - size, end) for size, end in zip(_IN_SIZES, _IN_ENDS))
IN_WIDTH = _IN_ENDS[-1]

BF16 = jnp.bfloat16
F32 = jnp.float32

VMEM_LIMIT_BYTES = 56 * 1024 * 1024
FRONT_VMEM_LIMIT_BYTES = 58 * 1024 * 1024


def _dot(a, b):
    return jnp.dot(a, b, preferred_element_type=F32)


def _silu(x):
    return x * jax.nn.sigmoid(x)


def _split_bf16(x):
    hi = x.astype(BF16)
    return hi, (x - hi.astype(F32)).astype(BF16)


def _dot_3pass(a, b_split):
    a_hi, a_lo = _split_bf16(a)
    b_hi, b_lo = b_split
    return _dot(a_hi, b_hi) + (_dot(a_lo, b_hi) + _dot(a_hi, b_lo))


def _prep_kernel(*refs, n_cond, table_blocks):
    refs = list(refs)
    take = lambda k: [refs.pop(0) for _ in range(k)]
    c_refs = take(n_cond)
    wada_ref, bada_ref, cs_ref, wf_ref, ws_ref, win_ref, wpa_ref, wpb_ref, wo_ref, c2_ref, s2_ref = take(11)
    factor_refs = take(2 * len(table_blocks))
    mod_refs = take(n_cond)
    e_ref, ws_o, win_o, wpa_o, wpb_o, wo_o = take(6)
    table_refs = refs

    cond = jnp.concatenate([c_ref[...] for c_ref in c_refs], axis=0)
    mod = _dot_3pass(_silu(cond), _split_bf16(wada_ref[...])) + bada_ref[...]
    row = 0
    for mod_ref in mod_refs:
        mod_ref[...] = mod[row:row + mod_ref.shape[0]]
        row += mod_ref.shape[0]
    win_o[...] = win_ref[...].astype(BF16)
    wpa_o[...] = wpa_ref[...].astype(BF16)
    wpb_o[...] = wpb_ref[...].astype(BF16)
    wo_o[...] = wo_ref[...].astype(BF16)

    c2, s2 = c2_ref[...], s2_ref[...]
    for t, table_ref in enumerate(table_refs):
        c1_ref, s1_ref = factor_refs[2 * t], factor_refs[2 * t + 1]
        for q in range(table_blocks[t]):
            c1, s1 = c1_ref[q], s1_ref[q]
            table_ref[q, 0:DFT_LEN, :] = (c1 * c2 - s1 * s2).astype(BF16)
            table_ref[q, DFT_LEN:, :] = (-(s1 * c2 + c1 * s2)).astype(BF16)

    @pl.when(pl.program_id(0) == 0)
    def _():
        ws_o[...] = ws_ref[...].astype(BF16)
        for g in range(FNET_GROUPS):
            e_ref[g] = _dot_3pass(cs_ref[...], _split_bf16(wf_ref[g])).astype(BF16)


def _prepare(conds, w_ada, b_ada, cs_c, w_fmix, w_s, w_in, w_pa, w_pb, w_out, seq_factors):
    e_shape = (FNET_GROUPS, 2 * FNET_GROUP_DIM, FNET_GROUP_DIM)
    whole = lambda shape: pl.BlockSpec(shape, lambda j: (0,) * len(shape))
    cols = lambda a: pl.BlockSpec((a.shape[0], a.shape[1] // PREP_STEPS), lambda j: (0, j))
    lead = lambda shape: pl.BlockSpec((shape[0] // PREP_STEPS,) + shape[1:], lambda j: (j, 0, 0))
    streamed = (w_in, w_pa, w_pb, w_out)
    assert all(a.shape[1] % (PREP_STEPS * FNET_GROUP_DIM) == 0 for a in (w_ada,) + streamed)
    n1s = sorted(seq_factors)
    assert all(n1 % PREP_STEPS == 0 for n1 in n1s)
    c2, s2 = _shared_dft_factors()
    factors = [f for n1 in n1s for f in _block_dft_factors(n1)]
    t_shapes = [(n1, 2 * DFT_LEN, DFT_LEN) for n1 in n1s]
    mod_cols = w_ada.shape[1] // PREP_STEPS
    outs = pl.pallas_call(
        functools.partial(_prep_kernel, n_cond=len(conds), table_blocks=tuple(n1 // PREP_STEPS for n1 in n1s)),
        grid=(PREP_STEPS,),
        in_specs=[whole(c.shape) for c in conds]
                 + [cols(w_ada), pl.BlockSpec((1, mod_cols), lambda j: (0, j)),
                    whole(cs_c.shape), whole(w_fmix.shape), whole(w_s.shape)] + [cols(a) for a in streamed]
                 + [whole(c2.shape), whole(s2.shape)] + [lead(f.shape) for f in factors],
        out_specs=[pl.BlockSpec((c.shape[0], mod_cols), lambda j: (0, j)) for c in conds]
                  + [whole(e_shape), whole(w_s.shape)] + [cols(a) for a in streamed] + [lead(t) for t in t_shapes],
        out_shape=[jax.ShapeDtypeStruct((c.shape[0], w_ada.shape[1]), F32) for c in conds]
                  + [jax.ShapeDtypeStruct(e_shape, BF16), jax.ShapeDtypeStruct(w_s.shape, BF16)]
                  + [jax.ShapeDtypeStruct(a.shape, BF16) for a in streamed]
                  + [jax.ShapeDtypeStruct(t, BF16) for t in t_shapes],
        compiler_params=pltpu.CompilerParams(
            dimension_semantics=("arbitrary",),
            vmem_limit_bytes=VMEM_LIMIT_BYTES),
        name="adaln_prep",
    )(*conds, w_ada, b_ada.reshape(1, -1), cs_c, w_fmix, w_s, *streamed, c2, s2, *factors)
    k = len(conds)
    return outs[:k], outs[k:k + 6], dict(zip(n1s, outs[k + 6:]))


def _mod_row(mod_ref, part):
    return mod_ref[pl.ds(pl.program_id(0), 1), part * D_MODEL:(part + 1) * D_MODEL]


def _modulated_norm(x, mod_ref, g_ref):
    ms = jnp.mean(x * x, axis=-1, keepdims=True)
    xn = x * lax.rsqrt(ms + EPS) * g_ref[...]
    return xn * (1.0 + _mod_row(mod_ref, 1)) + _mod_row(mod_ref, 0)


def _regrouped_blocks(a, perm_ref, n1):
    rows = SEQ_TILE // n1
    aps = [_dot(perm_ref[...], a[s * SEQ_TILE:(s + 1) * SEQ_TILE]).astype(BF16)
           for s in range(a.shape[0] // SEQ_TILE)]
    return [jnp.concatenate([ap[q * rows:(q + 1) * rows] for ap in aps], axis=0) for q in range(n1)]


def _fnet_in(x_ref, mod_ref, g_ref, wa_ref):
    h = _modulated_norm(x_ref[...], mod_ref, g_ref).astype(BF16)
    return _dot(h, wa_ref[...]).astype(BF16)


def _cmul_const(z, w):
    zr, zi = z
    wr, wi = w.real, w.imag
    tol = 1e-12
    if abs(wi) < tol:
        return (zr, zi) if abs(wr - 1) < tol else (zr * wr, zi * wr)
    if abs(wr) < tol:
        if abs(wi + 1) < tol:
            return (zi, -zr)
        if abs(wi - 1) < tol:
            return (-zi, zr)
    if abs(abs(wr) - abs(wi)) < tol:
        c = abs(wr)
        sr, si = math.copysign(1.0, wr), math.copysign(1.0, wi)
        re = (zr if sr > 0 else -zr) - (zi if si > 0 else -zi)
        im = (zi if sr > 0 else -zi) + (zr if si > 0 else -zr)
        return (re * c, im * c)
    return (zr * wr - zi * wi, zr * wi + zi * wr)


def _block_fft(zs):
    n = len(zs)
    if n == 1:
        return zs
    even = _block_fft(zs[0::2])
    odd = _block_fft(zs[1::2])
    out = [None] * n
    for k in range(n // 2):
        w = complex(math.cos(2 * math.pi * k / n), -math.sin(2 * math.pi * k / n))
        tr, ti = _cmul_const(odd[k], w)
        er, ei = even[k]
        out[k] = (er + tr, ei + ti)
        out[k + n // 2] = (er - tr, ei - ti)
    return out


def _run_skewed(stages, states):
    for t in range(len(stages) + len(states) - 1):
        for i, st in enumerate(states):
            if 0 <= t - i < len(stages):
                stages[t - i](st)


def _fourier_body(a_block, m_ref, e_ref, bf_ref, o_ref, n1, skew=True):
    def block_dfts(st):
        st["g"] = []
        for q in range(n1):
            g = _dot(m_ref[q], a_block(q, st["cols"]))
            st["g"].append((g[:DFT_LEN], g[DFT_LEN:]))

    def across_blocks(st):
        st["u"] = _block_fft(st.pop("g"))

    def channel_map(st):
        us = st.pop("u")
        for g in st["groups"]:
            lo, hi = g * FNET_GROUP_DIM - st["cols"].start, (g + 1) * FNET_GROUP_DIM - st["cols"].start
            z = jnp.concatenate(
                [jnp.concatenate([ur[:, lo:hi], ui[:, lo:hi]], axis=1).astype(BF16) for ur, ui in us], axis=0)
            out = slice(g * FNET_GROUP_DIM, (g + 1) * FNET_GROUP_DIM)
            o_ref[:, out] = _dot(z, e_ref[g]) + bf_ref[:, out]

    halves = [dict(cols=slice(j * FFT_WIDTH, (j + 1) * FFT_WIDTH),
                   groups=range(j * FFT_WIDTH // FNET_GROUP_DIM, (j + 1) * FFT_WIDTH // FNET_GROUP_DIM))
              for j in range(FNET_WIDTH // FFT_WIDTH)]
    if skew:
        _run_skewed((block_dfts, across_blocks, channel_map), halves)
    else:
        for st in halves:
            block_dfts(st)
            across_blocks(st)
            channel_map(st)


def _front_kernel(x_ref, mod_ref, g_ref, wa_ref, perm_ref, m_ref, e_ref, bf_ref, o_ref, *scratch, n1):
    blocks = _regrouped_blocks(_fnet_in(x_ref, mod_ref, g_ref, wa_ref), perm_ref, n1)
    if not scratch:
        _fourier_body(lambda q, cols: blocks[q][:, cols], m_ref, e_ref, bf_ref, o_ref, n1)
        return
    (a_sc,) = scratch
    t = pl.program_id(1)
    rows = PRE_TILE // n1
    for q, block in enumerate(blocks):
        a_sc[q, pl.ds(pl.multiple_of(t * rows, rows), rows), :] = block

    @pl.when(t == pl.num_programs(1) - 1)
    def _():
        _fourier_body(lambda q, cols: a_sc[q, :, cols], m_ref, e_ref, bf_ref, o_ref, n1, skew=False)


def _front_call(x, mod, norm_g, w_in, perm, m_tab, e_tab, b_fmix, n1):
    B, S, _ = x.shape
    tiles = S // PRE_TILE
    const2 = lambda b, t: (0, 0)
    const3 = lambda b, t: (0, 0, 0)
    return pl.pallas_call(
        functools.partial(_front_kernel, n1=n1),
        grid=(B, tiles),
        in_specs=[
            pl.BlockSpec((None, PRE_TILE, D_MODEL), lambda b, t: (b, t, 0)),
            pl.BlockSpec(mod.shape, const2),
            pl.BlockSpec((1, D_MODEL), const2),
            pl.BlockSpec((D_MODEL, FNET_WIDTH), lambda b, t: (0, _A[0] // FNET_WIDTH)),
            pl.BlockSpec((SEQ_TILE, SEQ_TILE), const2),
            pl.BlockSpec((n1, 2 * DFT_LEN, DFT_LEN), const3),
            pl.BlockSpec((FNET_GROUPS, 2 * FNET_GROUP_DIM, FNET_GROUP_DIM), const3),
            pl.BlockSpec((1, FNET_WIDTH), const2),
        ],
        out_specs=pl.BlockSpec((None, S, FNET_WIDTH), lambda b, t: (b, 0, 0)),
        out_shape=jax.ShapeDtypeStruct((B, S, FNET_WIDTH), F32),
        scratch_shapes=[pltpu.VMEM((n1, DFT_LEN, FNET_WIDTH), BF16)] if tiles > 1 else [],
        compiler_params=pltpu.CompilerParams(
            dimension_semantics=("arbitrary", "arbitrary"),
            vmem_limit_bytes=FRONT_VMEM_LIMIT_BYTES),
        name="fnet_front",
    )(x, mod, norm_g, w_in, perm, m_tab, e_tab, b_fmix)


def _main_kernel(x_ref, mod_ref, ya_ref, g_ref, win_ref, lng_ref, lnb_ref, ws_ref, bs_ref,
                 wpa_ref, wpb_ref, wo_ref, fg_ref, o_ref, *, tm):
    def proj(st, rng):
        return _dot(st["h"], win_ref[:, rng[0]:rng[1]])

    def load_norm(st):
        st["x"] = x_ref[st["rows"], :]
        st["h"] = _modulated_norm(st["x"], mod_ref, g_ref).astype(BF16)

    def project(st):
        st["v"] = proj(st, _V)
        st["ga"] = proj(st, _GA)
        st["ma"] = proj(st, _MA)

    def project2(st):
        st["u"] = proj(st, _U)
        st["gb"] = proj(st, _GB)

    def layernorm(st):
        v = st.pop("v")
        mu = jnp.mean(v, axis=-1, keepdims=True)
        vc = v - mu
        var = jnp.mean(vc * vc, axis=-1, keepdims=True)
        st["vln"] = (vc * lax.rsqrt(var + EPS) * lng_ref[...] + lnb_ref[...]).astype(BF16)

    def spatial_mix(st):
        vln = st.pop("vln")
        rows = []
        for c in range(MAIN_SUB // CHUNK):
            r0 = c * CHUNK
            cols = []
            for hd in range(SGU_HEADS):
                lo, hi = hd * SGU_HEAD_DIM, (hd + 1) * SGU_HEAD_DIM
                cols.append(_dot(ws_ref[hd], vln[r0:r0 + CHUNK, lo:hi]))
            rows.append(jnp.concatenate(cols, axis=1) + bs_ref[...])
        st["mixed"] = jnp.concatenate(rows, axis=0)

    def fourier_path(st):
        y_a = (ya_ref[st["rows"], :] * _silu(st.pop("ga"))).astype(BF16)
        st["merged"] = jax.nn.sigmoid(st.pop("ma")) * _dot(y_a, wpa_ref[...])
        st["mb"] = proj(st, _MB)

    def gating_path(st):
        y_b = (st.pop("u") * st.pop("mixed") * _silu(st.pop("gb"))).astype(BF16)
        st["merged"] = st["merged"] + jax.nn.sigmoid(st.pop("mb")) * _dot(y_b, wpb_ref[...])

    def out_proj(st):
        st["mg"] = st.pop("merged").astype(BF16)
        st["out0"] = _dot(st["mg"], wo_ref[:, 0:D_MODEL // 2])

    def out_proj2(st):
        st["out"] = jnp.concatenate([st.pop("out0"), _dot(st.pop("mg"), wo_ref[:, D_MODEL // 2:])], axis=1)

    def residual(st):
        st["xo"] = st.pop("x") + _mod_row(mod_ref, 2) * st.pop("out")
        st["ms"] = jnp.mean(st["xo"] * st["xo"], axis=-1, keepdims=True)

    def finish(st):
        o_ref[st["rows"], :] = st.pop("xo") * lax.rsqrt(st.pop("ms") + EPS) * fg_ref[...]

    stages = (load_norm, project, layernorm, project2, spatial_mix, fourier_path, gating_path, out_proj, out_proj2,
              residual, finish)
    subs = [dict(rows=slice(i * MAIN_SUB, (i + 1) * MAIN_SUB)) for i in range(tm // MAIN_SUB)]
    _run_skewed(stages, subs)


def _main_call(x, mod, ya, norm_g, w_in, ln_g, ln_b, w_s, bs_full, w_pa, w_pb, w_out,
               final_g, tm):
    B, S, _ = x.shape
    const2 = lambda b, i: (0, 0)
    return pl.pallas_call(
        functools.partial(_main_kernel, tm=tm),
        grid=(B, S // tm),
        in_specs=[
            pl.BlockSpec((None, tm, D_MODEL), lambda b, i: (b, i, 0)),
            pl.BlockSpec(mod.shape, lambda b, i: (0, 0)),
            pl.BlockSpec((None, tm, FNET_WIDTH), lambda b, i: (b, i, 0)),
            pl.BlockSpec((1, D_MODEL), const2),
            pl.BlockSpec((D_MODEL, IN_WIDTH), const2),
            pl.BlockSpec((1, SGU_WIDTH), const2),
            pl.BlockSpec((1, SGU_WIDTH), const2),
            pl.BlockSpec((SGU_HEADS, CHUNK, CHUNK), lambda b, i: (0, 0, 0)),
            pl.BlockSpec((CHUNK, SGU_WIDTH), const2),
            pl.BlockSpec((FNET_WIDTH, D_MODEL), const2),
            pl.BlockSpec((SGU_WIDTH, D_MODEL), const2),
            pl.BlockSpec((D_MODEL, D_MODEL), const2),
            pl.BlockSpec((1, D_MODEL), const2),
        ],
        out_specs=pl.BlockSpec((None, tm, D_MODEL), lambda b, i: (b, i, 0)),
        out_shape=jax.ShapeDtypeStruct((B, S, D_MODEL), F32),
        compiler_params=pltpu.CompilerParams(
            dimension_semantics=("arbitrary", "arbitrary"),
            vmem_limit_bytes=VMEM_LIMIT_BYTES),
        name="encoder_main",
    )(x, mod, ya, norm_g, w_in, ln_g, ln_b, w_s, bs_full, w_pa, w_pb, w_out, final_g)


def _channel_dft_table():
    c = np.arange(FNET_GROUP_DIM)
    ang = 2.0 * np.pi * ((c[:, None] * c[None, :]) % FNET_GROUP_DIM) / FNET_GROUP_DIM
    scale = 1.0 / math.sqrt(FNET_GROUP_DIM)
    return np.concatenate([np.cos(ang), np.sin(ang)], axis=0) * scale


def _regroup_matrix(n1):
    rows = SEQ_TILE // n1
    p = np.zeros((SEQ_TILE, SEQ_TILE), np.float32)
    for q in range(n1):
        for j in range(rows):
            p[q * rows + j, j * n1 + q] = 1.0
    return p


def _shared_dft_factors():
    k2 = np.arange(DFT_LEN)
    th2 = 2.0 * np.pi * ((k2[:, None] * k2[None, :]) % DFT_LEN) / DFT_LEN
    return jnp.asarray(np.cos(th2), F32), jnp.asarray(np.sin(th2), F32)


def _block_dft_factors(n1):
    S = n1 * DFT_LEN
    th1 = 2.0 * np.pi * ((np.arange(DFT_LEN)[None, :] * np.arange(n1)[:, None]) % S) / S
    scale = 1.0 / math.sqrt(S)
    return (jnp.asarray(np.cos(th1) * scale, F32)[:, :, None], jnp.asarray(np.sin(th1) * scale, F32)[:, :, None])


def _trunk(x, mod, wts):
    B, S, _ = x.shape
    assert S % DFT_LEN == 0
    n1 = S // DFT_LEN
    assert n1 & (n1 - 1) == 0 and SEQ_TILE % n1 == 0 and S % PRE_TILE == 0 and S % MAIN_TILE == 0
    perm = jnp.asarray(_regroup_matrix(n1), F32).astype(BF16)
    m_tab = wts["m_tab"][n1]
    ya = _front_call(x, mod, wts["norm_g"], wts["w_in"], perm, m_tab, wts["e_tab"], wts["b_fmix"], n1)
    return _main_call(x, mod, ya, wts["norm_g"], wts["w_in"], wts["ln_g"], wts["ln_b"],
                      wts["w_s"], wts["bs_full"], wts["w_pa"], wts["w_pb"], wts["w_out"],
                      wts["final_g"], MAIN_TILE)


def kernel(x_prompt, x_sample, c_prompt, c_sample, norm_g, w_ada, b_ada, w_in, w_fmix, b_fmix,
           sgu_ln_g, sgu_ln_b, w_s, b_s, w_pa, w_pb, w_out, final_g):
    depth = norm_g.shape[0]
    assert depth == 1
    l = 0
    (mod_p, mod_s), (e_tab, w_s_bf, w_in_bf, w_pa_bf, w_pb_bf, w_out_bf), m_tabs = _prepare(
        (c_prompt, c_sample), w_ada[l], b_ada[l],
        jnp.asarray(_channel_dft_table(), F32), w_fmix[l], w_s[l], w_in[l], w_pa[l], w_pb[l], w_out[l],
        seq_factors={x.shape[1] // DFT_LEN for x in (x_prompt, x_sample)})
    wts = dict(
        norm_g=norm_g[l].reshape(1, -1),
        w_in=w_in_bf,
        e_tab=e_tab,
        m_tab=m_tabs,
        b_fmix=b_fmix[l].reshape(1, -1),
        ln_g=sgu_ln_g[l].reshape(1, -1),
        ln_b=sgu_ln_b[l].reshape(1, -1),
        w_s=w_s_bf,
        bs_full=jnp.repeat(b_s[l].T, SGU_HEAD_DIM, axis=1),
        w_pa=w_pa_bf,
        w_pb=w_pb_bf,
        w_out=w_out_bf,
        final_g=final_g.reshape(1, -1),
    )
    y_prompt = _trunk(x_prompt, mod_p, wts)
    y_sample = _trunk(x_sample, mod_s, wts)
    return (y_prompt, y_sample)
```

```python
import functools
import math

import numpy as np
import jax
import jax.numpy as jnp
from jax import lax
from jax.experimental import pallas as pl
from jax.experimental.pallas import tpu as pltpu

D_MODEL = 1024
FNET_WIDTH = 512
FNET_GROUPS = 4
FNET_GROUP_DIM = 128
SGU_WIDTH = 512
SGU_HEADS = 4
SGU_HEAD_DIM = 128
CHUNK = 128
EPS = 1e-6
PREP_STEPS = 4
DFT_LEN = 256
SEQ_TILE = 256
PRE_TILE = 2048
MAIN_TILE = 1024
MAIN_SUB = 256
FFT_WIDTH = 256

_IN_SIZES = (FNET_WIDTH, FNET_WIDTH, SGU_WIDTH, SGU_WIDTH, SGU_WIDTH, D_MODEL, D_MODEL)
_IN_ENDS = tuple(int(v) for v in np.cumsum(_IN_SIZES))
_A, _GA, _U, _V, _GB, _MA, _MB = ((end - size, end) for size, end in zip(_IN_SIZES, _IN_ENDS))
IN_WIDTH = _IN_ENDS[-1]

BF16 = jnp.bfloat16
F32 = jnp.float32

VMEM_LIMIT_BYTES = 56 * 1024 * 1024


def _dot(a, b):
    return jnp.dot(a, b, preferred_element_type=F32)


def _silu(x):
    return x * jax.nn.sigmoid(x)


def _split_bf16(x):
    hi = x.astype(BF16)
    return hi, (x - hi.astype(F32)).astype(BF16)


def _dot_3pass(a, b_split):
    a_hi, a_lo = _split_bf16(a)
    b_hi, b_lo = b_split
    return _dot(a_hi, b_hi) + (_dot(a_lo, b_hi) + _dot(a_hi, b_lo))


def _prep_kernel(*refs, n_cond, table_blocks):
    refs = list(refs)
    take = lambda k: [refs.pop(0) for _ in range(k)]
    c_refs = take(n_cond)
    wada_ref, bada_ref, cs_ref, wf_ref, ws_ref, win_ref, wpa_ref, wpb_ref, wo_ref, c2_ref, s2_ref = take(11)
    factor_refs = take(2 * len(table_blocks))
    mod_refs = take(n_cond)
    e_ref, ws_o, win_o, wpa_o, wpb_o, wo_o = take(6)
    table_refs = refs

    cond = jnp.concatenate([c_ref[...] for c_ref in c_refs], axis=0)
    mod = _dot_3pass(_silu(cond), _split_bf16(wada_ref[...])) + bada_ref[...]
    row = 0
    for mod_ref in mod_refs:
        mod_ref[...] = mod[row:row + mod_ref.shape[0]]
        row += mod_ref.shape[0]
    win_o[...] = win_ref[...].astype(BF16)
    wpa_o[...] = wpa_ref[...].astype(BF16)
    wpb_o[...] = wpb_ref[...].astype(BF16)
    wo_o[...] = wo_ref[...].astype(BF16)

    c2, s2 = c2_ref[...], s2_ref[...]
    for t, table_ref in enumerate(table_refs):
        c1_ref, s1_ref = factor_refs[2 * t], factor_refs[2 * t + 1]
        for q in range(table_blocks[t]):
            c1, s1 = c1_ref[q], s1_ref[q]
            table_ref[q, 0:DFT_LEN, :] = (c1 * c2 - s1 * s2).astype(BF16)
            table_ref[q, DFT_LEN:, :] = (-(s1 * c2 + c1 * s2)).astype(BF16)

    @pl.when(pl.program_id(0) == 0)
    def _():
        ws_o[...] = ws_ref[...].astype(BF16)
        for g in range(FNET_GROUPS):
            e_ref[g] = _dot_3pass(cs_ref[...], _split_bf16(wf_ref[g])).astype(BF16)


def _prepare(conds, w_ada, b_ada, cs_c, w_fmix, w_s, w_in, w_pa, w_pb, w_out, seq_factors):
    e_shape = (FNET_GROUPS, 2 * FNET_GROUP_DIM, FNET_GROUP_DIM)
    whole = lambda shape: pl.BlockSpec(shape, lambda j: (0,) * len(shape))
    cols = lambda a: pl.BlockSpec((a.shape[0], a.shape[1] // PREP_STEPS), lambda j: (0, j))
    lead = lambda shape: pl.BlockSpec((shape[0] // PREP_STEPS,) + shape[1:], lambda j: (j, 0, 0))
    streamed = (w_in, w_pa, w_pb, w_out)
    assert all(a.shape[1] % (PREP_STEPS * FNET_GROUP_DIM) == 0 for a in (w_ada,) + streamed)
    n1s = sorted(seq_factors)
    assert all(n1 % PREP_STEPS == 0 for n1 in n1s)
    c2, s2 = _shared_dft_factors()
    factors = [f for n1 in n1s for f in _block_dft_factors(n1)]
    t_shapes = [(n1, 2 * DFT_LEN, DFT_LEN) for n1 in n1s]
    mod_cols = w_ada.shape[1] // PREP_STEPS
    outs = pl.pallas_call(
        functools.partial(_prep_kernel, n_cond=len(conds), table_blocks=tuple(n1 // PREP_STEPS for n1 in n1s)),
        grid=(PREP_STEPS,),
        in_specs=[whole(c.shape) for c in conds]
                 + [cols(w_ada), pl.BlockSpec((1, mod_cols), lambda j: (0, j)),
                    whole(cs_c.shape), whole(w_fmix.shape), whole(w_s.shape)] + [cols(a) for a in streamed]
                 + [whole(c2.shape), whole(s2.shape)] + [lead(f.shape) for f in factors],
        out_specs=[pl.BlockSpec((c.shape[0], mod_cols), lambda j: (0, j)) for c in conds]
                  + [whole(e_shape), whole(w_s.shape)] + [cols(a) for a in streamed] + [lead(t) for t in t_shapes],
        out_shape=[jax.ShapeDtypeStruct((c.shape[0], w_ada.shape[1]), F32) for c in conds]
                  + [jax.ShapeDtypeStruct(e_shape, BF16), jax.ShapeDtypeStruct(w_s.shape, BF16)]
                  + [jax.ShapeDtypeStruct(a.shape, BF16) for a in streamed]
                  + [jax.ShapeDtypeStruct(t, BF16) for t in t_shapes],
        compiler_params=pltpu.CompilerParams(
            dimension_semantics=("arbitrary",),
            vmem_limit_bytes=VMEM_LIMIT_BYTES),
        name="adaln_prep",
    )(*conds, w_ada, b_ada.reshape(1, -1), cs_c, w_fmix, w_s, *streamed, c2, s2, *factors)
    k = len(conds)
    return outs[:k], outs[k:k + 6], dict(zip(n1s, outs[k + 6:]))


def _mod_row(mod_ref, part):
    return mod_ref[pl.ds(pl.program_id(0), 1), part * D_MODEL:(part + 1) * D_MODEL]


def _modulated_norm(x, mod_ref, g_ref):
    ms = jnp.mean(x * x, axis=-1, keepdims=True)
    xn = x * lax.rsqrt(ms + EPS) * g_ref[...]
    return xn * (1.0 + _mod_row(mod_ref, 1)) + _mod_row(mod_ref, 0)


def _regrouped_blocks(a, perm_ref, n1):
    rows = SEQ_TILE // n1
    aps = [_dot(perm_ref[...], a[s * SEQ_TILE:(s + 1) * SEQ_TILE]).astype(BF16)
           for s in range(a.shape[0] // SEQ_TILE)]
    return [jnp.concatenate([ap[q * rows:(q + 1) * rows] for ap in aps], axis=0) for q in range(n1)]


def _fnet_in(x_ref, mod_ref, g_ref, wa_ref):
    h = _modulated_norm(x_ref[...], mod_ref, g_ref).astype(BF16)
    return _dot(h, wa_ref[...]).astype(BF16)


def _pre_kernel(x_ref, mod_ref, g_ref, win_ref, perm_ref, a_ref, *, n1):
    for q, block in enumerate(_regrouped_blocks(_fnet_in(x_ref, mod_ref, g_ref, win_ref), perm_ref, n1)):
        a_ref[q] = block


def _pre_call(x, mod, norm_g, w_in, perm, n1):
    B, S, _ = x.shape
    rows = PRE_TILE // n1
    return pl.pallas_call(
        functools.partial(_pre_kernel, n1=n1),
        grid=(B, S // PRE_TILE),
        in_specs=[
            pl.BlockSpec((None, PRE_TILE, D_MODEL), lambda b, i: (b, i, 0)),
            pl.BlockSpec(mod.shape, lambda b, i: (0, 0)),
            pl.BlockSpec((1, D_MODEL), lambda b, i: (0, 0)),
            pl.BlockSpec((D_MODEL, FNET_WIDTH), lambda b, i: (0, _A[0] // FNET_WIDTH)),
            pl.BlockSpec((SEQ_TILE, SEQ_TILE), lambda b, i: (0, 0)),
        ],
        out_specs=pl.BlockSpec((None, n1, rows, FNET_WIDTH), lambda b, i: (b, 0, i, 0)),
        out_shape=jax.ShapeDtypeStruct((B, n1, DFT_LEN, FNET_WIDTH), BF16),
        compiler_params=pltpu.CompilerParams(
            dimension_semantics=("arbitrary", "arbitrary"),
            vmem_limit_bytes=VMEM_LIMIT_BYTES),
        name="fnet_pre",
    )(x, mod, norm_g, w_in, perm)


def _cmul_const(z, w):
    zr, zi = z
    wr, wi = w.real, w.imag
    tol = 1e-12
    if abs(wi) < tol:
        return (zr, zi) if abs(wr - 1) < tol else (zr * wr, zi * wr)
    if abs(wr) < tol:
        if abs(wi + 1) < tol:
            return (zi, -zr)
        if abs(wi - 1) < tol:
            return (-zi, zr)
    if abs(abs(wr) - abs(wi)) < tol:
        c = abs(wr)
        sr, si = math.copysign(1.0, wr), math.copysign(1.0, wi)
        re = (zr if sr > 0 else -zr) - (zi if si > 0 else -zi)
        im = (zi if sr > 0 else -zi) + (zr if si > 0 else -zr)
        return (re * c, im * c)
    return (zr * wr - zi * wi, zr * wi + zi * wr)


def _block_fft(zs):
    n = len(zs)
    if n == 1:
        return zs
    even = _block_fft(zs[0::2])
    odd = _block_fft(zs[1::2])
    out = [None] * n
    for k in range(n // 2):
        w = complex(math.cos(2 * math.pi * k / n), -math.sin(2 * math.pi * k / n))
        tr, ti = _cmul_const(odd[k], w)
        er, ei = even[k]
        out[k] = (er + tr, ei + ti)
        out[k + n // 2] = (er - tr, ei - ti)
    return out


def _run_skewed(stages, states):
    for t in range(len(stages) + len(states) - 1):
        for i, st in enumerate(states):
            if 0 <= t - i < len(stages):
                stages[t - i](st)


def _fourier_body(a_block, m_ref, e_ref, bf_ref, o_ref, n1):
    def block_dfts(st):
        st["g"] = []
        for q in range(n1):
            g = _dot(m_ref[q], a_block(q, st["cols"]))
            st["g"].append((g[:DFT_LEN], g[DFT_LEN:]))

    def across_blocks(st):
        st["u"] = _block_fft(st.pop("g"))

    def channel_map(st):
        us = st.pop("u")
        for g in st["groups"]:
            lo, hi = g * FNET_GROUP_DIM - st["cols"].start, (g + 1) * FNET_GROUP_DIM - st["cols"].start
            z = jnp.concatenate(
                [jnp.concatenate([ur[:, lo:hi], ui[:, lo:hi]], axis=1).astype(BF16) for ur, ui in us], axis=0)
            out = slice(g * FNET_GROUP_DIM, (g + 1) * FNET_GROUP_DIM)
            o_ref[:, out] = _dot(z, e_ref[g]) + bf_ref[:, out]

    halves = [dict(cols=slice(j * FFT_WIDTH, (j + 1) * FFT_WIDTH),
                   groups=range(j * FFT_WIDTH // FNET_GROUP_DIM, (j + 1) * FFT_WIDTH // FNET_GROUP_DIM))
              for j in range(FNET_WIDTH // FFT_WIDTH)]
    _run_skewed((block_dfts, across_blocks, channel_map), halves)


def _fourier_kernel(a_ref, m_ref, e_ref, bf_ref, o_ref, *, n1):
    _fourier_body(lambda q, cols: a_ref[q, :, cols], m_ref, e_ref, bf_ref, o_ref, n1)


def _front_kernel(x_ref, mod_ref, g_ref, win_ref, perm_ref, m_ref, e_ref, bf_ref, o_ref, *, n1):
    blocks = _regrouped_blocks(_fnet_in(x_ref, mod_ref, g_ref, win_ref), perm_ref, n1)
    _fourier_body(lambda q, cols: blocks[q][:, cols], m_ref, e_ref, bf_ref, o_ref, n1)


def _front_call(x, mod, norm_g, w_in, perm, m_tab, e_tab, b_fmix, n1):
    B, S, _ = x.shape
    const2 = lambda b: (0, 0)
    const3 = lambda b: (0, 0, 0)
    return pl.pallas_call(
        functools.partial(_front_kernel, n1=n1),
        grid=(B,),
        in_specs=[
            pl.BlockSpec((None, S, D_MODEL), lambda b: (b, 0, 0)),
            pl.BlockSpec(mod.shape, lambda b: (0, 0)),
            pl.BlockSpec((1, D_MODEL), const2),
            pl.BlockSpec((D_MODEL, FNET_WIDTH), lambda b: (0, _A[0] // FNET_WIDTH)),
            pl.BlockSpec((SEQ_TILE, SEQ_TILE), const2),
            pl.BlockSpec((n1, 2 * DFT_LEN, DFT_LEN), const3),
            pl.BlockSpec((FNET_GROUPS, 2 * FNET_GROUP_DIM, FNET_GROUP_DIM), const3),
            pl.BlockSpec((1, FNET_WIDTH), const2),
        ],
        out_specs=pl.BlockSpec((None, S, FNET_WIDTH), lambda b: (b, 0, 0)),
        out_shape=jax.ShapeDtypeStruct((B, S, FNET_WIDTH), F32),
        compiler_params=pltpu.CompilerParams(
            dimension_semantics=("arbitrary",),
            vmem_limit_bytes=VMEM_LIMIT_BYTES),
        name="fnet_front",
    )(x, mod, norm_g, w_in, perm, m_tab, e_tab, b_fmix)


def _fourier_call(a_perm, m_tab, e_tab, b_fmix):
    B, n1, _, _ = a_perm.shape
    S = n1 * DFT_LEN
    return pl.pallas_call(
        functools.partial(_fourier_kernel, n1=n1),
        grid=(B,),
        in_specs=[
            pl.BlockSpec((None, n1, DFT_LEN, FNET_WIDTH), lambda b: (b, 0, 0, 0)),
            pl.BlockSpec((n1, 2 * DFT_LEN, DFT_LEN), lambda b: (0, 0, 0)),
            pl.BlockSpec((FNET_GROUPS, 2 * FNET_GROUP_DIM, FNET_GROUP_DIM), lambda b: (0, 0, 0)),
            pl.BlockSpec((1, FNET_WIDTH), lambda b: (0, 0)),
        ],
        out_specs=pl.BlockSpec((None, S, FNET_WIDTH), lambda b: (b, 0, 0)),
        out_shape=jax.ShapeDtypeStruct((B, S, FNET_WIDTH), F32),
        compiler_params=pltpu.CompilerParams(
            dimension_semantics=("arbitrary",),
            vmem_limit_bytes=VMEM_LIMIT_BYTES),
        name="fnet_fft",
    )(a_perm, m_tab, e_tab, b_fmix)


def _main_kernel(x_ref, mod_ref, ya_ref, g_ref, win_ref, lng_ref, lnb_ref, ws_ref, bs_ref,
                 wpa_ref, wpb_ref, wo_ref, fg_ref, o_ref, *, tm):
    def proj(st, rng):
        return _dot(st["h"], win_ref[:, rng[0]:rng[1]])

    def load_norm(st):
        st["x"] = x_ref[st["rows"], :]
        st["h"] = _modulated_norm(st["x"], mod_ref, g_ref).astype(BF16)

    def project(st):
        st["v"] = proj(st, _V)
        st["ga"] = proj(st, _GA)
        st["ma"] = proj(st, _MA)

    def project2(st):
        st["u"] = proj(st, _U)
        st["gb"] = proj(st, _GB)

    def layernorm(st):
        v = st.pop("v")
        mu = jnp.mean(v, axis=-1, keepdims=True)
        vc = v - mu
        var = jnp.mean(vc * vc, axis=-1, keepdims=True)
        st["vln"] = (vc * lax.rsqrt(var + EPS) * lng_ref[...] + lnb_ref[...]).astype(BF16)

    def spatial_mix(st):
        vln = st.pop("vln")
        rows = []
        for c in range(MAIN_SUB // CHUNK):
            r0 = c * CHUNK
            cols = []
            for hd in range(SGU_HEADS):
                lo, hi = hd * SGU_HEAD_DIM, (hd + 1) * SGU_HEAD_DIM
                cols.append(_dot(ws_ref[hd], vln[r0:r0 + CHUNK, lo:hi]))
            rows.append(jnp.concatenate(cols, axis=1) + bs_ref[...])
        st["mixed"] = jnp.concatenate(rows, axis=0)

    def fourier_path(st):
        y_a = (ya_ref[st["rows"], :] * _silu(st.pop("ga"))).astype(BF16)
        st["merged"] = jax.nn.sigmoid(st.pop("ma")) * _dot(y_a, wpa_ref[...])
        st["mb"] = proj(st, _MB)

    def gating_path(st):
        y_b = (st.pop("u") * st.pop("mixed") * _silu(st.pop("gb"))).astype(BF16)
        st["merged"] = st["merged"] + jax.nn.sigmoid(st.pop("mb")) * _dot(y_b, wpb_ref[...])

    def out_proj(st):
        st["mg"] = st.pop("merged").astype(BF16)
        st["out0"] = _dot(st["mg"], wo_ref[:, 0:D_MODEL // 2])

    def out_proj2(st):
        st["out"] = jnp.concatenate([st.pop("out0"), _dot(st.pop("mg"), wo_ref[:, D_MODEL // 2:])], axis=1)

    def residual(st):
        st["xo"] = st.pop("x") + _mod_row(mod_ref, 2) * st.pop("out")
        st["ms"] = jnp.mean(st["xo"] * st["xo"], axis=-1, keepdims=True)

    def finish(st):
        o_ref[st["rows"], :] = st.pop("xo") * lax.rsqrt(st.pop("ms") + EPS) * fg_ref[...]

    stages = (load_norm, project, layernorm, project2, spatial_mix, fourier_path, gating_path, out_proj, out_proj2,
              residual, finish)
    subs = [dict(rows=slice(i * MAIN_SUB, (i + 1) * MAIN_SUB)) for i in range(tm // MAIN_SUB)]
    _run_skewed(stages, subs)


def _main_call(x, mod, ya, norm_g, w_in, ln_g, ln_b, w_s, bs_full, w_pa, w_pb, w_out,
               final_g, tm):
    B, S, _ = x.shape
    const2 = lambda b, i: (0, 0)
    return pl.pallas_call(
        functools.partial(_main_kernel, tm=tm),
        grid=(B, S // tm),
        in_specs=[
            pl.BlockSpec((None, tm, D_MODEL), lambda b, i: (b, i, 0)),
            pl.BlockSpec(mod.shape, lambda b, i: (0, 0)),
            pl.BlockSpec((None, tm, FNET_WIDTH), lambda b, i: (b, i, 0)),
            pl.BlockSpec((1, D_MODEL), const2),
            pl.BlockSpec((D_MODEL, IN_WIDTH), const2),
            pl.BlockSpec((1, SGU_WIDTH), const2),
            pl.BlockSpec((1, SGU_WIDTH), const2),
            pl.BlockSpec((SGU_HEADS, CHUNK, CHUNK), lambda b, i: (0, 0, 0)),
            pl.BlockSpec((CHUNK, SGU_WIDTH), const2),
            pl.BlockSpec((FNET_WIDTH, D_MODEL), const2),
            pl.BlockSpec((SGU_WIDTH, D_MODEL), const2),
            pl.BlockSpec((D_MODEL, D_MODEL), const2),
            pl.BlockSpec((1, D_MODEL), const2),
        ],
        out_specs=pl.BlockSpec((None, tm, D_MODEL), lambda b, i: (b, i, 0)),
        out_shape=jax.ShapeDtypeStruct((B, S, D_MODEL), F32),
        compiler_params=pltpu.CompilerParams(
            dimension_semantics=("arbitrary", "arbitrary"),
            vmem_limit_bytes=VMEM_LIMIT_BYTES),
        name="encoder_main",
    )(x, mod, ya, norm_g, w_in, ln_g, ln_b, w_s, bs_full, w_pa, w_pb, w_out, final_g)


def _channel_dft_table():
    c = np.arange(FNET_GROUP_DIM)
    ang = 2.0 * np.pi * ((c[:, None] * c[None, :]) % FNET_GROUP_DIM) / FNET_GROUP_DIM
    scale = 1.0 / math.sqrt(FNET_GROUP_DIM)
    return np.concatenate([np.cos(ang), np.sin(ang)], axis=0) * scale


def _regroup_matrix(n1):
    rows = SEQ_TILE // n1
    p = np.zeros((SEQ_TILE, SEQ_TILE), np.float32)
    for q in range(n1):
        for j in range(rows):
            p[q * rows + j, j * n1 + q] = 1.0
    return p


def _shared_dft_factors():
    k2 = np.arange(DFT_LEN)
    th2 = 2.0 * np.pi * ((k2[:, None] * k2[None, :]) % DFT_LEN) / DFT_LEN
    return jnp.asarray(np.cos(th2), F32), jnp.asarray(np.sin(th2), F32)


def _block_dft_factors(n1):
    S = n1 * DFT_LEN
    th1 = 2.0 * np.pi * ((np.arange(DFT_LEN)[None, :] * np.arange(n1)[:, None]) % S) / S
    scale = 1.0 / math.sqrt(S)
    return (jnp.asarray(np.cos(th1) * scale, F32)[:, :, None], jnp.asarray(np.sin(th1) * scale, F32)[:, :, None])


def _trunk(x, mod, wts):
    B, S, _ = x.shape
    assert S % DFT_LEN == 0
    n1 = S // DFT_LEN
    assert n1 & (n1 - 1) == 0 and SEQ_TILE % n1 == 0 and S % PRE_TILE == 0 and S % MAIN_TILE == 0
    perm = jnp.asarray(_regroup_matrix(n1), F32).astype(BF16)
    m_tab = wts["m_tab"][n1]
    if S == PRE_TILE:
        ya = _front_call(x, mod, wts["norm_g"], wts["w_in"], perm, m_tab, wts["e_tab"], wts["b_fmix"], n1)
    else:
        a_perm = _pre_call(x, mod, wts["norm_g"], wts["w_in"], perm, n1)
        ya = _fourier_call(a_perm, m_tab, wts["e_tab"], wts["b_fmix"])
    return _main_call(x, mod, ya, wts["norm_g"], wts["w_in"], wts["ln_g"], wts["ln_b"],
                      wts["w_s"], wts["bs_full"], wts["w_pa"], wts["w_pb"], wts["w_out"],
                      wts["final_g"], MAIN_TILE)


def kernel(x_prompt, x_sample, c_prompt, c_sample, norm_g, w_ada, b_ada, w_in, w_fmix, b_fmix,
           sgu_ln_g, sgu_ln_b, w_s, b_s, w_pa, w_pb, w_out, final_g):
    depth = norm_g.shape[0]
    assert depth == 1
    l = 0
    (mod_p, mod_s), (e_tab, w_s_bf, w_in_bf, w_pa_bf, w_pb_bf, w_out_bf), m_tabs = _prepare(
        (c_prompt, c_sample), w_ada[l], b_ada[l],
        jnp.asarray(_channel_dft_table(), F32), w_fmix[l], w_s[l], w_in[l], w_pa[l], w_pb[l], w_out[l],
        seq_factors={x.shape[1] // DFT_LEN for x in (x_prompt, x_sample)})
    wts = dict(
        norm_g=norm_g[l].reshape(1, -1),
        w_in=w_in_bf,
        e_tab=e_tab,
        m_tab=m_tabs,
        b_fmix=b_fmix[l].reshape(1, -1),
        ln_g=sgu_ln_g[l].reshape(1, -1),
        ln_b=sgu_ln_b[l].reshape(1, -1),
        w_s=w_s_bf,
        bs_full=jnp.repeat(b_s[l].T, SGU_HEAD_DIM, axis=1),
        w_pa=w_pa_bf,
        w_pb=w_pb_bf,
        w_out=w_out_bf,
        final_g=final_g.reshape(1, -1),
    )
    y_prompt = _trunk(x_prompt, mod_p, wts)
    y_sample = _trunk(x_sample, mod_s, wts)
    return (y_prompt, y_sample)
```

```python
import functools
import math

import numpy as np
import jax
import jax.numpy as jnp
from jax import lax
from jax.experimental import pallas as pl
from jax.experimental.pallas import tpu as pltpu

D_MODEL = 1024
FNET_WIDTH = 512
FNET_GROUPS = 4
FNET_GROUP_DIM = 128
SGU_WIDTH = 512
SGU_HEADS = 4
SGU_HEAD_DIM = 128
CHUNK = 128
EPS = 1e-6
PREP_STEPS = 4
DFT_LEN = 256
SEQ_TILE = 256
PRE_TILE = 2048
PRE_SUB = 512
MAIN_TILE = 1024
MAIN_SUB = 256
FFT_WIDTH = 256

_IN_SIZES = (FNET_WIDTH, FNET_WIDTH, SGU_WIDTH, SGU_WIDTH, SGU_WIDTH, D_MODEL, D_MODEL)
_IN_ENDS = tuple(int(v) for v in np.cumsum(_IN_SIZES))
_A, _GA, _U, _V, _GB, _MA, _MB = ((end - size, end) for size, end in zip(_IN_SIZES, _IN_ENDS))
IN_WIDTH = _IN_ENDS[-1]

BF16 = jnp.bfloat16
F32 = jnp.float32

VMEM_LIMIT_BYTES = 56 * 1024 * 1024


def _dot(a, b):
    return jnp.dot(a, b, preferred_element_type=F32)


def _silu(x):
    return x * jax.nn.sigmoid(x)


def _split_bf16(x):
    hi = x.astype(BF16)
    return hi, (x - hi.astype(F32)).astype(BF16)


def _dot_3pass(a, b_split):
    a_hi, a_lo = _split_bf16(a)
    b_hi, b_lo = b_split
    return _dot(a_hi, b_hi) + (_dot(a_lo, b_hi) + _dot(a_hi, b_lo))


def _prep_kernel(*refs, n_cond, table_blocks):
    refs = list(refs)
    take = lambda k: [refs.pop(0) for _ in range(k)]
    c_refs = take(n_cond)
    wada_ref, bada_ref, cs_ref, wf_ref, ws_ref, win_ref, wpa_ref, wpb_ref, wo_ref, c2_ref, s2_ref = take(11)
    factor_refs = take(2 * len(table_blocks))
    mod_refs = take(n_cond)
    e_ref, ws_o, win_o, wpa_o, wpb_o, wo_o = take(6)
    table_refs = refs

    cond = jnp.concatenate([c_ref[...] for c_ref in c_refs], axis=0)
    mod = _dot_3pass(_silu(cond), _split_bf16(wada_ref[...])) + bada_ref[...]
    row = 0
    for mod_ref in mod_refs:
        mod_ref[...] = mod[row:row + mod_ref.shape[0]]
        row += mod_ref.shape[0]
    win_o[...] = win_ref[...].astype(BF16)
    wpa_o[...] = wpa_ref[...].astype(BF16)
    wpb_o[...] = wpb_ref[...].astype(BF16)
    wo_o[...] = wo_ref[...].astype(BF16)

    c2, s2 = c2_ref[...], s2_ref[...]
    for t, table_ref in enumerate(table_refs):
        c1_ref, s1_ref = factor_refs[2 * t], factor_refs[2 * t + 1]
        for q in range(table_blocks[t]):
            c1, s1 = c1_ref[q], s1_ref[q]
            table_ref[q, 0:DFT_LEN, :] = (c1 * c2 - s1 * s2).astype(BF16)
            table_ref[q, DFT_LEN:, :] = (-(s1 * c2 + c1 * s2)).astype(BF16)

    @pl.when(pl.program_id(0) == 0)
    def _():
        ws_o[...] = ws_ref[...].astype(BF16)
        for g in range(FNET_GROUPS):
            e_ref[g] = _dot_3pass(cs_ref[...], _split_bf16(wf_ref[g])).astype(BF16)


def _prepare(conds, w_ada, b_ada, cs_c, w_fmix, w_s, w_in, w_pa, w_pb, w_out, seq_factors):
    e_shape = (FNET_GROUPS, 2 * FNET_GROUP_DIM, FNET_GROUP_DIM)
    whole = lambda shape: pl.BlockSpec(shape, lambda j: (0,) * len(shape))
    cols = lambda a: pl.BlockSpec((a.shape[0], a.shape[1] // PREP_STEPS), lambda j: (0, j))
    lead = lambda shape: pl.BlockSpec((shape[0] // PREP_STEPS,) + shape[1:], lambda j: (j, 0, 0))
    streamed = (w_in, w_pa, w_pb, w_out)
    assert all(a.shape[1] % (PREP_STEPS * FNET_GROUP_DIM) == 0 for a in (w_ada,) + streamed)
    n1s = sorted(seq_factors)
    assert all(n1 % PREP_STEPS == 0 for n1 in n1s)
    c2, s2 = _shared_dft_factors()
    factors = [f for n1 in n1s for f in _block_dft_factors(n1)]
    t_shapes = [(n1, 2 * DFT_LEN, DFT_LEN) for n1 in n1s]
    mod_cols = w_ada.shape[1] // PREP_STEPS
    outs = pl.pallas_call(
        functools.partial(_prep_kernel, n_cond=len(conds), table_blocks=tuple(n1 // PREP_STEPS for n1 in n1s)),
        grid=(PREP_STEPS,),
        in_specs=[whole(c.shape) for c in conds]
                 + [cols(w_ada), pl.BlockSpec((1, mod_cols), lambda j: (0, j)),
                    whole(cs_c.shape), whole(w_fmix.shape), whole(w_s.shape)] + [cols(a) for a in streamed]
                 + [whole(c2.shape), whole(s2.shape)] + [lead(f.shape) for f in factors],
        out_specs=[pl.BlockSpec((c.shape[0], mod_cols), lambda j: (0, j)) for c in conds]
                  + [whole(e_shape), whole(w_s.shape)] + [cols(a) for a in streamed] + [lead(t) for t in t_shapes],
        out_shape=[jax.ShapeDtypeStruct((c.shape[0], w_ada.shape[1]), F32) for c in conds]
                  + [jax.ShapeDtypeStruct(e_shape, BF16), jax.ShapeDtypeStruct(w_s.shape, BF16)]
                  + [jax.ShapeDtypeStruct(a.shape, BF16) for a in streamed]
                  + [jax.ShapeDtypeStruct(t, BF16) for t in t_shapes],
        compiler_params=pltpu.CompilerParams(
            dimension_semantics=("arbitrary",),
            vmem_limit_bytes=VMEM_LIMIT_BYTES),
        name="adaln_prep",
    )(*conds, w_ada, b_ada.reshape(1, -1), cs_c, w_fmix, w_s, *streamed, c2, s2, *factors)
    k = len(conds)
    return outs[:k], outs[k:k + 6], dict(zip(n1s, outs[k + 6:]))


def _mod_row(mod_ref, part):
    return mod_ref[pl.ds(pl.program_id(0), 1), part * D_MODEL:(part + 1) * D_MODEL]


def _modulated_norm(x, mod_ref, g_ref):
    ms = jnp.mean(x * x, axis=-1, keepdims=True)
    xn = x * lax.rsqrt(ms + EPS) * g_ref[...]
    return xn * (1.0 + _mod_row(mod_ref, 1)) + _mod_row(mod_ref, 0)


def _regrouped_blocks(a, perm_ref, n1):
    rows = SEQ_TILE // n1
    aps = [_dot(perm_ref[...], a[s * SEQ_TILE:(s + 1) * SEQ_TILE]).astype(BF16)
           for s in range(a.shape[0] // SEQ_TILE)]
    return [jnp.concatenate([ap[q * rows:(q + 1) * rows] for ap in aps], axis=0) for q in range(n1)]


def _fnet_in(x_ref, mod_ref, g_ref, wa_ref):
    parts = []
    for r0 in range(0, x_ref.shape[0], PRE_SUB):
        h = _modulated_norm(x_ref[r0:r0 + PRE_SUB, :], mod_ref, g_ref).astype(BF16)
        parts.append(_dot(h, wa_ref[...]).astype(BF16))
    return jnp.concatenate(parts, axis=0)


def _pre_kernel(x_ref, mod_ref, g_ref, win_ref, perm_ref, a_ref, *, n1):
    for q, block in enumerate(_regrouped_blocks(_fnet_in(x_ref, mod_ref, g_ref, win_ref), perm_ref, n1)):
        a_ref[q] = block


def _pre_call(x, mod, norm_g, w_in, perm, n1):
    B, S, _ = x.shape
    rows = PRE_TILE // n1
    return pl.pallas_call(
        functools.partial(_pre_kernel, n1=n1),
        grid=(B, S // PRE_TILE),
        in_specs=[
            pl.BlockSpec((None, PRE_TILE, D_MODEL), lambda b, i: (b, i, 0)),
            pl.BlockSpec(mod.shape, lambda b, i: (0, 0)),
            pl.BlockSpec((1, D_MODEL), lambda b, i: (0, 0)),
            pl.BlockSpec((D_MODEL, FNET_WIDTH), lambda b, i: (0, _A[0] // FNET_WIDTH)),
            pl.BlockSpec((SEQ_TILE, SEQ_TILE), lambda b, i: (0, 0)),
        ],
        out_specs=pl.BlockSpec((None, n1, rows, FNET_WIDTH), lambda b, i: (b, 0, i, 0)),
        out_shape=jax.ShapeDtypeStruct((B, n1, DFT_LEN, FNET_WIDTH), BF16),
        compiler_params=pltpu.CompilerParams(
            dimension_semantics=("arbitrary", "arbitrary"),
            vmem_limit_bytes=VMEM_LIMIT_BYTES),
        name="fnet_pre",
    )(x, mod, norm_g, w_in, perm)


def _cmul_const(z, w):
    zr, zi = z
    wr, wi = w.real, w.imag
    tol = 1e-12
    if abs(wi) < tol:
        return (zr, zi) if abs(wr - 1) < tol else (zr * wr, zi * wr)
    if abs(wr) < tol:
        if abs(wi + 1) < tol:
            return (zi, -zr)
        if abs(wi - 1) < tol:
            return (-zi, zr)
    if abs(abs(wr) - abs(wi)) < tol:
        c = abs(wr)
        sr, si = math.copysign(1.0, wr), math.copysign(1.0, wi)
        re = (zr if sr > 0 else -zr) - (zi if si > 0 else -zi)
        im = (zi if sr > 0 else -zi) + (zr if si > 0 else -zr)
        return (re * c, im * c)
    return (zr * wr - zi * wi, zr * wi + zi * wr)


def _block_fft(zs):
    n = len(zs)
    if n == 1:
        return zs
    even = _block_fft(zs[0::2])
    odd = _block_fft(zs[1::2])
    out = [None] * n
    for k in range(n // 2):
        w = complex(math.cos(2 * math.pi * k / n), -math.sin(2 * math.pi * k / n))
        tr, ti = _cmul_const(odd[k], w)
        er, ei = even[k]
        out[k] = (er + tr, ei + ti)
        out[k + n // 2] = (er - tr, ei - ti)
    return out


def _run_skewed(stages, states):
    for t in range(len(stages) + len(states) - 1):
        for i, st in enumerate(states):
            if 0 <= t - i < len(stages):
                stages[t - i](st)


def _fourier_body(a_block, m_ref, e_ref, bf_ref, o_ref, n1):
    def block_dfts(st):
        st["g"] = []
        for q in range(n1):
            g = _dot(m_ref[q], a_block(q, st["cols"]))
            st["g"].append((g[:DFT_LEN], g[DFT_LEN:]))

    def across_blocks(st):
        st["u"] = _block_fft(st.pop("g"))

    def channel_map(st):
        us = st.pop("u")
        for g in st["groups"]:
            lo, hi = g * FNET_GROUP_DIM - st["cols"].start, (g + 1) * FNET_GROUP_DIM - st["cols"].start
            z = jnp.concatenate(
                [jnp.concatenate([ur[:, lo:hi], ui[:, lo:hi]], axis=1).astype(BF16) for ur, ui in us], axis=0)
            out = slice(g * FNET_GROUP_DIM, (g + 1) * FNET_GROUP_DIM)
            o_ref[:, out] = _dot(z, e_ref[g]) + bf_ref[:, out]

    halves = [dict(cols=slice(j * FFT_WIDTH, (j + 1) * FFT_WIDTH),
                   groups=range(j * FFT_WIDTH // FNET_GROUP_DIM, (j + 1) * FFT_WIDTH // FNET_GROUP_DIM))
              for j in range(FNET_WIDTH // FFT_WIDTH)]
    _run_skewed((block_dfts, across_blocks, channel_map), halves)


def _fourier_kernel(a_ref, m_ref, e_ref, bf_ref, o_ref, *, n1):
    _fourier_body(lambda q, cols: a_ref[q, :, cols], m_ref, e_ref, bf_ref, o_ref, n1)


def _front_kernel(x_ref, mod_ref, g_ref, win_ref, perm_ref, m_ref, e_ref, bf_ref, o_ref, *, n1):
    blocks = _regrouped_blocks(_fnet_in(x_ref, mod_ref, g_ref, win_ref), perm_ref, n1)
    _fourier_body(lambda q, cols: blocks[q][:, cols], m_ref, e_ref, bf_ref, o_ref, n1)


def _front_call(x, mod, norm_g, w_in, perm, m_tab, e_tab, b_fmix, n1):
    B, S, _ = x.shape
    const2 = lambda b: (0, 0)
    const3 = lambda b: (0, 0, 0)
    return pl.pallas_call(
        functools.partial(_front_kernel, n1=n1),
        grid=(B,),
        in_specs=[
            pl.BlockSpec((None, S, D_MODEL), lambda b: (b, 0, 0)),
            pl.BlockSpec(mod.shape, lambda b: (0, 0)),
            pl.BlockSpec((1, D_MODEL), const2),
            pl.BlockSpec((D_MODEL, FNET_WIDTH), lambda b: (0, _A[0] // FNET_WIDTH)),
            pl.BlockSpec((SEQ_TILE, SEQ_TILE), const2),
            pl.BlockSpec((n1, 2 * DFT_LEN, DFT_LEN), const3),
            pl.BlockSpec((FNET_GROUPS, 2 * FNET_GROUP_DIM, FNET_GROUP_DIM), const3),
            pl.BlockSpec((1, FNET_WIDTH), const2),
        ],
        out_specs=pl.BlockSpec((None, S, FNET_WIDTH), lambda b: (b, 0, 0)),
        out_shape=jax.ShapeDtypeStruct((B, S, FNET_WIDTH), F32),
        compiler_params=pltpu.CompilerParams(
            dimension_semantics=("arbitrary",),
            vmem_limit_bytes=VMEM_LIMIT_BYTES),
        name="fnet_front",
    )(x, mod, norm_g, w_in, perm, m_tab, e_tab, b_fmix)


def _fourier_call(a_perm, m_tab, e_tab, b_fmix):
    B, n1, _, _ = a_perm.shape
    S = n1 * DFT_LEN
    return pl.pallas_call(
        functools.partial(_fourier_kernel, n1=n1),
        grid=(B,),
        in_specs=[
            pl.BlockSpec((None, n1, DFT_LEN, FNET_WIDTH), lambda b: (b, 0, 0, 0)),
            pl.BlockSpec((n1, 2 * DFT_LEN, DFT_LEN), lambda b: (0, 0, 0)),
            pl.BlockSpec((FNET_GROUPS, 2 * FNET_GROUP_DIM, FNET_GROUP_DIM), lambda b: (0, 0, 0)),
            pl.BlockSpec((1, FNET_WIDTH), lambda b: (0, 0)),
        ],
        out_specs=pl.BlockSpec((None, S, FNET_WIDTH), lambda b: (b, 0, 0)),
        out_shape=jax.ShapeDtypeStruct((B, S, FNET_WIDTH), F32),
        compiler_params=pltpu.CompilerParams(
            dimension_semantics=("arbitrary",),
            vmem_limit_bytes=VMEM_LIMIT_BYTES),
        name="fnet_fft",
    )(a_perm, m_tab, e_tab, b_fmix)


def _main_kernel(x_ref, mod_ref, ya_ref, g_ref, win_ref, lng_ref, lnb_ref, ws_ref, bs_ref,
                 wpa_ref, wpb_ref, wo_ref, fg_ref, o_ref, *, tm):
    def proj(st, rng):
        return _dot(st["h"], win_ref[:, rng[0]:rng[1]])

    def load_norm(st):
        st["x"] = x_ref[st["rows"], :]
        st["h"] = _modulated_norm(st["x"], mod_ref, g_ref).astype(BF16)

    def project(st):
        st["v"] = proj(st, _V)
        st["ga"] = proj(st, _GA)
        st["ma"] = proj(st, _MA)

    def project2(st):
        st["u"] = proj(st, _U)
        st["gb"] = proj(st, _GB)

    def layernorm(st):
        v = st.pop("v")
        mu = jnp.mean(v, axis=-1, keepdims=True)
        vc = v - mu
        var = jnp.mean(vc * vc, axis=-1, keepdims=True)
        st["vln"] = (vc * lax.rsqrt(var + EPS) * lng_ref[...] + lnb_ref[...]).astype(BF16)

    def spatial_mix(st):
        vln = st.pop("vln")
        rows = []
        for c in range(MAIN_SUB // CHUNK):
            r0 = c * CHUNK
            cols = []
            for hd in range(SGU_HEADS):
                lo, hi = hd * SGU_HEAD_DIM, (hd + 1) * SGU_HEAD_DIM
                cols.append(_dot(ws_ref[hd], vln[r0:r0 + CHUNK, lo:hi]))
            rows.append(jnp.concatenate(cols, axis=1) + bs_ref[...])
        st["mixed"] = jnp.concatenate(rows, axis=0)

    def fourier_path(st):
        y_a = (ya_ref[st["rows"], :] * _silu(st.pop("ga"))).astype(BF16)
        st["merged"] = jax.nn.sigmoid(st.pop("ma")) * _dot(y_a, wpa_ref[...])
        st["mb"] = proj(st, _MB)

    def gating_path(st):
        y_b = (st.pop("u") * st.pop("mixed") * _silu(st.pop("gb"))).astype(BF16)
        st["merged"] = st["merged"] + jax.nn.sigmoid(st.pop("mb")) * _dot(y_b, wpb_ref[...])

    def out_proj(st):
        st["mg"] = st.pop("merged").astype(BF16)
        st["out0"] = _dot(st["mg"], wo_ref[:, 0:D_MODEL // 2])

    def out_proj2(st):
        st["out"] = jnp.concatenate([st.pop("out0"), _dot(st.pop("mg"), wo_ref[:, D_MODEL // 2:])], axis=1)

    def residual(st):
        st["xo"] = st.pop("x") + _mod_row(mod_ref, 2) * st.pop("out")
        st["ms"] = jnp.mean(st["xo"] * st["xo"], axis=-1, keepdims=True)

    def finish(st):
        o_ref[st["rows"], :] = st.pop("xo") * lax.rsqrt(st.pop("ms") + EPS) * fg_ref[...]

    stages = (load_norm, project, layernorm, project2, spatial_mix, fourier_path, gating_path, out_proj, out_proj2,
              residual, finish)
    subs = [dict(rows=slice(i * MAIN_SUB, (i + 1) * MAIN_SUB)) for i in range(tm // MAIN_SUB)]
    _run_skewed(stages, subs)


def _main_call(x, mod, ya, norm_g, w_in, ln_g, ln_b, w_s, bs_full, w_pa, w_pb, w_out,
               final_g, tm):
    B, S, _ = x.shape
    const2 = lambda b, i: (0, 0)
    return pl.pallas_call(
        functools.partial(_main_kernel, tm=tm),
        grid=(B, S // tm),
        in_specs=[
            pl.BlockSpec((None, tm, D_MODEL), lambda b, i: (b, i, 0)),
            pl.BlockSpec(mod.shape, lambda b, i: (0, 0)),
            pl.BlockSpec((None, tm, FNET_WIDTH), lambda b, i: (b, i, 0)),
            pl.BlockSpec((1, D_MODEL), const2),
            pl.BlockSpec((D_MODEL, IN_WIDTH), const2),
            pl.BlockSpec((1, SGU_WIDTH), const2),
            pl.BlockSpec((1, SGU_WIDTH), const2),
            pl.BlockSpec((SGU_HEADS, CHUNK, CHUNK), lambda b, i: (0, 0, 0)),
            pl.BlockSpec((CHUNK, SGU_WIDTH), const2),
            pl.BlockSpec((FNET_WIDTH, D_MODEL), const2),
            pl.BlockSpec((SGU_WIDTH, D_MODEL), const2),
            pl.BlockSpec((D_MODEL, D_MODEL), const2),
            pl.BlockSpec((1, D_MODEL), const2),
        ],
        out_specs=pl.BlockSpec((None, tm, D_MODEL), lambda b, i: (b, i, 0)),
        out_shape=jax.ShapeDtypeStruct((B, S, D_MODEL), F32),
        compiler_params=pltpu.CompilerParams(
            dimension_semantics=("arbitrary", "arbitrary"),
            vmem_limit_bytes=VMEM_LIMIT_BYTES),
        name="encoder_main",
    )(x, mod, ya, norm_g, w_in, ln_g, ln_b, w_s, bs_full, w_pa, w_pb, w_out, final_g)


def _channel_dft_table():
    c = np.arange(FNET_GROUP_DIM)
    ang = 2.0 * np.pi * ((c[:, None] * c[None, :]) % FNET_GROUP_DIM) / FNET_GROUP_DIM
    scale = 1.0 / math.sqrt(FNET_GROUP_DIM)
    return np.concatenate([np.cos(ang), np.sin(ang)], axis=0) * scale


def _regroup_matrix(n1):
    rows = SEQ_TILE // n1
    p = np.zeros((SEQ_TILE, SEQ_TILE), np.float32)
    for q in range(n1):
        for j in range(rows):
            p[q * rows + j, j * n1 + q] = 1.0
    return p


def _shared_dft_factors():
    k2 = np.arange(DFT_LEN)
    th2 = 2.0 * np.pi * ((k2[:, None] * k2[None, :]) % DFT_LEN) / DFT_LEN
    return jnp.asarray(np.cos(th2), F32), jnp.asarray(np.sin(th2), F32)


def _block_dft_factors(n1):
    S = n1 * DFT_LEN
    th1 = 2.0 * np.pi * ((np.arange(DFT_LEN)[None, :] * np.arange(n1)[:, None]) % S) / S
    scale = 1.0 / math.sqrt(S)
    return (jnp.asarray(np.cos(th1) * scale, F32)[:, :, None], jnp.asarray(np.sin(th1) * scale, F32)[:, :, None])


def _trunk(x, mod, wts):
    B, S, _ = x.shape
    assert S % DFT_LEN == 0
    n1 = S // DFT_LEN
    assert n1 & (n1 - 1) == 0 and SEQ_TILE % n1 == 0 and S % PRE_TILE == 0 and S % MAIN_TILE == 0
    perm = jnp.asarray(_regroup_matrix(n1), F32).astype(BF16)
    m_tab = wts["m_tab"][n1]
    if S == PRE_TILE:
        ya = _front_call(x, mod, wts["norm_g"], wts["w_in"], perm, m_tab, wts["e_tab"], wts["b_fmix"], n1)
    else:
        a_perm = _pre_call(x, mod, wts["norm_g"], wts["w_in"], perm, n1)
        ya = _fourier_call(a_perm, m_tab, wts["e_tab"], wts["b_fmix"])
    return _main_call(x, mod, ya, wts["norm_g"], wts["w_in"], wts["ln_g"], wts["ln_b"],
                      wts["w_s"], wts["bs_full"], wts["w_pa"], wts["w_pb"], wts["w_out"],
                      wts["final_g"], MAIN_TILE)


def kernel(x_prompt, x_sample, c_prompt, c_sample, norm_g, w_ada, b_ada, w_in, w_fmix, b_fmix,
           sgu_ln_g, sgu_ln_b, w_s, b_s, w_pa, w_pb, w_out, final_g):
    depth = norm_g.shape[0]
    assert depth == 1
    l = 0
    (mod_p, mod_s), (e_tab, w_s_bf, w_in_bf, w_pa_bf, w_pb_bf, w_out_bf), m_tabs = _prepare(
        (c_prompt, c_sample), w_ada[l], b_ada[l],
        jnp.asarray(_channel_dft_table(), F32), w_fmix[l], w_s[l], w_in[l], w_pa[l], w_pb[l], w_out[l],
        seq_factors={x.shape[1] // DFT_LEN for x in (x_prompt, x_sample)})
    wts = dict(
        norm_g=norm_g[l].reshape(1, -1),
        w_in=w_in_bf,
        e_tab=e_tab,
        m_tab=m_tabs,
        b_fmix=b_fmix[l].reshape(1, -1),
        ln_g=sgu_ln_g[l].reshape(1, -1),
        ln_b=sgu_ln_b[l].reshape(1, -1),
        w_s=w_s_bf,
        bs_full=jnp.repeat(b_s[l].T, SGU_HEAD_DIM, axis=1),
        w_pa=w_pa_bf,
        w_pb=w_pb_bf,
        w_out=w_out_bf,
        final_g=final_g.reshape(1, -1),
    )
    y_prompt = _trunk(x_prompt, mod_p, wts)
    y_sample = _trunk(x_sample, mod_s, wts)
    return (y_prompt, y_sample)
```

```python
import functools
import math

import numpy as np
import jax
import jax.numpy as jnp
from jax import lax
from jax.experimental import pallas as pl
from jax.experimental.pallas import tpu as pltpu

D_MODEL = 1024
FNET_WIDTH = 512
FNET_GROUPS = 4
FNET_GROUP_DIM = 128
SGU_WIDTH = 512
SGU_HEADS = 4
SGU_HEAD_DIM = 128
CHUNK = 128
EPS = 1e-6
PREP_STEPS = 4
DFT_LEN = 256
SEQ_TILE = 256
PRE_TILE = 2048
PRE_SUB = 512
MAIN_TILE = 1024
MAIN_SUB = 256
FFT_WIDTH = 256

_IN_SIZES = (FNET_WIDTH, FNET_WIDTH, SGU_WIDTH, SGU_WIDTH, SGU_WIDTH, D_MODEL, D_MODEL)
_IN_ENDS = tuple(int(v) for v in np.cumsum(_IN_SIZES))
_A, _GA, _U, _V, _GB, _MA, _MB = ((end - size, end) for size, end in zip(_IN_SIZES, _IN_ENDS))
IN_WIDTH = _IN_ENDS[-1]

BF16 = jnp.bfloat16
F32 = jnp.float32

VMEM_LIMIT_BYTES = 56 * 1024 * 1024


def _dot(a, b):
    return jnp.dot(a, b, preferred_element_type=F32)


def _silu(x):
    return x * jax.nn.sigmoid(x)


def _split_bf16(x):
    hi = x.astype(BF16)
    return hi, (x - hi.astype(F32)).astype(BF16)


def _dot_3pass(a, b_split):
    a_hi, a_lo = _split_bf16(a)
    b_hi, b_lo = b_split
    return _dot(a_hi, b_hi) + (_dot(a_lo, b_hi) + _dot(a_hi, b_lo))


def _prep_kernel(*refs, n_cond, table_blocks):
    refs = list(refs)
    take = lambda k: [refs.pop(0) for _ in range(k)]
    c_refs = take(n_cond)
    wada_ref, bada_ref, cs_ref, wf_ref, ws_ref, win_ref, wpa_ref, wpb_ref, wo_ref, c2_ref, s2_ref = take(11)
    factor_refs = take(2 * len(table_blocks))
    mod_refs = take(n_cond)
    e_ref, ws_o, win_o, wpa_o, wpb_o, wo_o = take(6)
    table_refs = refs

    cond = jnp.concatenate([c_ref[...] for c_ref in c_refs], axis=0)
    mod = _dot_3pass(_silu(cond), _split_bf16(wada_ref[...])) + bada_ref[...]
    row = 0
    for mod_ref in mod_refs:
        mod_ref[...] = mod[row:row + mod_ref.shape[0]]
        row += mod_ref.shape[0]
    win_o[...] = win_ref[...].astype(BF16)
    wpa_o[...] = wpa_ref[...].astype(BF16)
    wpb_o[...] = wpb_ref[...].astype(BF16)
    wo_o[...] = wo_ref[...].astype(BF16)

    c2, s2 = c2_ref[...], s2_ref[...]
    for t, table_ref in enumerate(table_refs):
        c1_ref, s1_ref = factor_refs[2 * t], factor_refs[2 * t + 1]
        for q in range(table_blocks[t]):
            c1, s1 = c1_ref[q], s1_ref[q]
            table_ref[q, 0:DFT_LEN, :] = (c1 * c2 - s1 * s2).astype(BF16)
            table_ref[q, DFT_LEN:, :] = (-(s1 * c2 + c1 * s2)).astype(BF16)

    @pl.when(pl.program_id(0) == 0)
    def _():
        ws_o[...] = ws_ref[...].astype(BF16)
        for g in range(FNET_GROUPS):
            e_ref[g] = _dot_3pass(cs_ref[...], _split_bf16(wf_ref[g])).astype(BF16)


def _prepare(conds, w_ada, b_ada, cs_c, w_fmix, w_s, w_in, w_pa, w_pb, w_out, seq_factors):
    e_shape = (FNET_GROUPS, 2 * FNET_GROUP_DIM, FNET_GROUP_DIM)
    whole = lambda shape: pl.BlockSpec(shape, lambda j: (0,) * len(shape))
    cols = lambda a: pl.BlockSpec((a.shape[0], a.shape[1] // PREP_STEPS), lambda j: (0, j))
    lead = lambda shape: pl.BlockSpec((shape[0] // PREP_STEPS,) + shape[1:], lambda j: (j, 0, 0))
    streamed = (w_in, w_pa, w_pb, w_out)
    assert all(a.shape[1] % (PREP_STEPS * FNET_GROUP_DIM) == 0 for a in (w_ada,) + streamed)
    n1s = sorted(seq_factors)
    assert all(n1 % PREP_STEPS == 0 for n1 in n1s)
    c2, s2 = _shared_dft_factors()
    factors = [f for n1 in n1s for f in _block_dft_factors(n1)]
    t_shapes = [(n1, 2 * DFT_LEN, DFT_LEN) for n1 in n1s]
    mod_cols = w_ada.shape[1] // PREP_STEPS
    outs = pl.pallas_call(
        functools.partial(_prep_kernel, n_cond=len(conds), table_blocks=tuple(n1 // PREP_STEPS for n1 in n1s)),
        grid=(PREP_STEPS,),
        in_specs=[whole(c.shape) for c in conds]
                 + [cols(w_ada), pl.BlockSpec((1, mod_cols), lambda j: (0, j)),
                    whole(cs_c.shape), whole(w_fmix.shape), whole(w_s.shape)] + [cols(a) for a in streamed]
                 + [whole(c2.shape), whole(s2.shape)] + [lead(f.shape) for f in factors],
        out_specs=[pl.BlockSpec((c.shape[0], mod_cols), lambda j: (0, j)) for c in conds]
                  + [whole(e_shape), whole(w_s.shape)] + [cols(a) for a in streamed] + [lead(t) for t in t_shapes],
        out_shape=[jax.ShapeDtypeStruct((c.shape[0], w_ada.shape[1]), F32) for c in conds]
                  + [jax.ShapeDtypeStruct(e_shape, BF16), jax.ShapeDtypeStruct(w_s.shape, BF16)]
                  + [jax.ShapeDtypeStruct(a.shape, BF16) for a in streamed]
                  + [jax.ShapeDtypeStruct(t, BF16) for t in t_shapes],
        compiler_params=pltpu.CompilerParams(
            dimension_semantics=("arbitrary",),
            vmem_limit_bytes=VMEM_LIMIT_BYTES),
        name="adaln_prep",
    )(*conds, w_ada, b_ada.reshape(1, -1), cs_c, w_fmix, w_s, *streamed, c2, s2, *factors)
    k = len(conds)
    return outs[:k], outs[k:k + 6], dict(zip(n1s, outs[k + 6:]))


def _mod_row(mod_ref, part, batch=None):
    batch = pl.program_id(0) if batch is None else batch
    return mod_ref[pl.ds(batch, 1), part * D_MODEL:(part + 1) * D_MODEL]


def _modulated_norm(x, mod_ref, g_ref, batch=None):
    ms = jnp.mean(x * x, axis=-1, keepdims=True)
    xn = x * lax.rsqrt(ms + EPS) * g_ref[...]
    return xn * (1.0 + _mod_row(mod_ref, 1, batch)) + _mod_row(mod_ref, 0, batch)


def _regrouped_blocks(a, perm_ref, n1):
    rows = SEQ_TILE // n1
    aps = [_dot(perm_ref[...], a[s * SEQ_TILE:(s + 1) * SEQ_TILE]).astype(BF16)
           for s in range(a.shape[0] // SEQ_TILE)]
    return [jnp.concatenate([ap[q * rows:(q + 1) * rows] for ap in aps], axis=0) for q in range(n1)]


def _fnet_in(x_ref, mod_ref, g_ref, w_a, batch=None):
    parts = []
    for r0 in range(0, x_ref.shape[0], PRE_SUB):
        h = _modulated_norm(x_ref[r0:r0 + PRE_SUB, :], mod_ref, g_ref, batch).astype(BF16)
        parts.append(_dot(h, w_a).astype(BF16))
    return jnp.concatenate(parts, axis=0)


def _pre_kernel(x_ref, mod_ref, g_ref, wa_ref, perm_ref, a_ref, *, n1):
    for q, block in enumerate(_regrouped_blocks(_fnet_in(x_ref, mod_ref, g_ref, wa_ref[...]), perm_ref, n1)):
        a_ref[q] = block


def _pre_call(x, mod, norm_g, w_in, perm, n1):
    B, S, _ = x.shape
    rows = PRE_TILE // n1
    return pl.pallas_call(
        functools.partial(_pre_kernel, n1=n1),
        grid=(B, S // PRE_TILE),
        in_specs=[
            pl.BlockSpec((None, PRE_TILE, D_MODEL), lambda b, i: (b, i, 0)),
            pl.BlockSpec(mod.shape, lambda b, i: (0, 0)),
            pl.BlockSpec((1, D_MODEL), lambda b, i: (0, 0)),
            pl.BlockSpec((D_MODEL, FNET_WIDTH), lambda b, i: (0, _A[0] // FNET_WIDTH)),
            pl.BlockSpec((SEQ_TILE, SEQ_TILE), lambda b, i: (0, 0)),
        ],
        out_specs=pl.BlockSpec((None, n1, rows, FNET_WIDTH), lambda b, i: (b, 0, i, 0)),
        out_shape=jax.ShapeDtypeStruct((B, n1, DFT_LEN, FNET_WIDTH), BF16),
        compiler_params=pltpu.CompilerParams(
            dimension_semantics=("arbitrary", "arbitrary"),
            vmem_limit_bytes=VMEM_LIMIT_BYTES),
        name="fnet_pre",
    )(x, mod, norm_g, w_in, perm)


def _cmul_const(z, w):
    zr, zi = z
    wr, wi = w.real, w.imag
    tol = 1e-12
    if abs(wi) < tol:
        return (zr, zi) if abs(wr - 1) < tol else (zr * wr, zi * wr)
    if abs(wr) < tol:
        if abs(wi + 1) < tol:
            return (zi, -zr)
        if abs(wi - 1) < tol:
            return (-zi, zr)
    if abs(abs(wr) - abs(wi)) < tol:
        c = abs(wr)
        sr, si = math.copysign(1.0, wr), math.copysign(1.0, wi)
        re = (zr if sr > 0 else -zr) - (zi if si > 0 else -zi)
        im = (zi if sr > 0 else -zi) + (zr if si > 0 else -zr)
        return (re * c, im * c)
    return (zr * wr - zi * wi, zr * wi + zi * wr)


def _block_fft(zs):
    n = len(zs)
    if n == 1:
        return zs
    even = _block_fft(zs[0::2])
    odd = _block_fft(zs[1::2])
    out = [None] * n
    for k in range(n // 2):
        w = complex(math.cos(2 * math.pi * k / n), -math.sin(2 * math.pi * k / n))
        tr, ti = _cmul_const(odd[k], w)
        er, ei = even[k]
        out[k] = (er + tr, ei + ti)
        out[k + n // 2] = (er - tr, ei - ti)
    return out


def _run_skewed(pipelines):
    for t in range(max(len(stages) + i for i, (stages, _) in enumerate(pipelines))):
        for i, (stages, st) in enumerate(pipelines):
            if 0 <= t - i < len(stages):
                stages[t - i](st)


def _fourier_body(a_block, m_ref, e_ref, bf_ref, o_ref, n1):
    def block_dfts(st):
        st["g"] = []
        for q in range(n1):
            g = _dot(m_ref[q], a_block(q, st["cols"]))
            st["g"].append((g[:DFT_LEN], g[DFT_LEN:]))

    def across_blocks(st):
        st["u"] = _block_fft(st.pop("g"))

    def channel_map(st):
        us = st.pop("u")
        for g in st["groups"]:
            lo, hi = g * FNET_GROUP_DIM - st["cols"].start, (g + 1) * FNET_GROUP_DIM - st["cols"].start
            z = jnp.concatenate(
                [jnp.concatenate([ur[:, lo:hi], ui[:, lo:hi]], axis=1).astype(BF16) for ur, ui in us], axis=0)
            out = slice(g * FNET_GROUP_DIM, (g + 1) * FNET_GROUP_DIM)
            o_ref[:, out] = _dot(z, e_ref[g]) + bf_ref[:, out]

    halves = [dict(cols=slice(j * FFT_WIDTH, (j + 1) * FFT_WIDTH),
                   groups=range(j * FFT_WIDTH // FNET_GROUP_DIM, (j + 1) * FFT_WIDTH // FNET_GROUP_DIM))
              for j in range(FNET_WIDTH // FFT_WIDTH)]
    _run_skewed([((block_dfts, across_blocks, channel_map), st) for st in halves])


def _fourier_kernel(a_ref, m_ref, e_ref, bf_ref, o_ref, *, n1):
    _fourier_body(lambda q, cols: a_ref[q, :, cols], m_ref, e_ref, bf_ref, o_ref, n1)


def _front_kernel(x_ref, mod_ref, g_ref, win_ref, perm_ref, m_ref, e_ref, bf_ref, o_ref, *, n1):
    blocks = _regrouped_blocks(_fnet_in(x_ref, mod_ref, g_ref, win_ref[...]), perm_ref, n1)
    _fourier_body(lambda q, cols: blocks[q][:, cols], m_ref, e_ref, bf_ref, o_ref, n1)


def _front_call(x, mod, norm_g, w_in, perm, m_tab, e_tab, b_fmix, n1):
    B, S, _ = x.shape
    const2 = lambda b: (0, 0)
    const3 = lambda b: (0, 0, 0)
    return pl.pallas_call(
        functools.partial(_front_kernel, n1=n1),
        grid=(B,),
        in_specs=[
            pl.BlockSpec((None, S, D_MODEL), lambda b: (b, 0, 0)),
            pl.BlockSpec(mod.shape, lambda b: (0, 0)),
            pl.BlockSpec((1, D_MODEL), const2),
            pl.BlockSpec((D_MODEL, FNET_WIDTH), lambda b: (0, _A[0] // FNET_WIDTH)),
            pl.BlockSpec((SEQ_TILE, SEQ_TILE), const2),
            pl.BlockSpec((n1, 2 * DFT_LEN, DFT_LEN), const3),
            pl.BlockSpec((FNET_GROUPS, 2 * FNET_GROUP_DIM, FNET_GROUP_DIM), const3),
            pl.BlockSpec((1, FNET_WIDTH), const2),
        ],
        out_specs=pl.BlockSpec((None, S, FNET_WIDTH), lambda b: (b, 0, 0)),
        out_shape=jax.ShapeDtypeStruct((B, S, FNET_WIDTH), F32),
        compiler_params=pltpu.CompilerParams(
            dimension_semantics=("arbitrary",),
            vmem_limit_bytes=VMEM_LIMIT_BYTES),
        name="fnet_front",
    )(x, mod, norm_g, w_in, perm, m_tab, e_tab, b_fmix)


def _fourier_call(a_perm, m_tab, e_tab, b_fmix):
    B, n1, _, _ = a_perm.shape
    S = n1 * DFT_LEN
    return pl.pallas_call(
        functools.partial(_fourier_kernel, n1=n1),
        grid=(B,),
        in_specs=[
            pl.BlockSpec((None, n1, DFT_LEN, FNET_WIDTH), lambda b: (b, 0, 0, 0)),
            pl.BlockSpec((n1, 2 * DFT_LEN, DFT_LEN), lambda b: (0, 0, 0)),
            pl.BlockSpec((FNET_GROUPS, 2 * FNET_GROUP_DIM, FNET_GROUP_DIM), lambda b: (0, 0, 0)),
            pl.BlockSpec((1, FNET_WIDTH), lambda b: (0, 0)),
        ],
        out_specs=pl.BlockSpec((None, S, FNET_WIDTH), lambda b: (b, 0, 0)),
        out_shape=jax.ShapeDtypeStruct((B, S, FNET_WIDTH), F32),
        compiler_params=pltpu.CompilerParams(
            dimension_semantics=("arbitrary",),
            vmem_limit_bytes=VMEM_LIMIT_BYTES),
        name="fnet_fft",
    )(a_perm, m_tab, e_tab, b_fmix)


def _main_kernel(*refs, tm, rider):
    (x_ref, mod_ref, ya_ref, g_ref, win_ref, lng_ref, lnb_ref, ws_ref, bs_ref,
     wpa_ref, wpb_ref, wo_ref, fg_ref) = refs[:13]
    if rider is None:
        (o_ref,) = refs[13:]
    else:
        rx_ref, rmod_ref, perm_ref, o_ref, ra_ref = refs[13:]

    def proj(st, rng):
        return _dot(st["h"], win_ref[:, rng[0]:rng[1]])

    def load_norm(st):
        st["x"] = x_ref[st["rows"], :]
        st["h"] = _modulated_norm(st["x"], mod_ref, g_ref).astype(BF16)

    def project(st):
        st["v"] = proj(st, _V)
        st["ga"] = proj(st, _GA)
        st["ma"] = proj(st, _MA)

    def project2(st):
        st["u"] = proj(st, _U)
        st["gb"] = proj(st, _GB)

    def layernorm(st):
        v = st.pop("v")
        mu = jnp.mean(v, axis=-1, keepdims=True)
        vc = v - mu
        var = jnp.mean(vc * vc, axis=-1, keepdims=True)
        st["vln"] = (vc * lax.rsqrt(var + EPS) * lng_ref[...] + lnb_ref[...]).astype(BF16)

    def spatial_mix(st):
        vln = st.pop("vln")
        rows = []
        for c in range(MAIN_SUB // CHUNK):
            r0 = c * CHUNK
            cols = []
            for hd in range(SGU_HEADS):
                lo, hi = hd * SGU_HEAD_DIM, (hd + 1) * SGU_HEAD_DIM
                cols.append(_dot(ws_ref[hd], vln[r0:r0 + CHUNK, lo:hi]))
            rows.append(jnp.concatenate(cols, axis=1) + bs_ref[...])
        st["mixed"] = jnp.concatenate(rows, axis=0)

    def fourier_path(st):
        y_a = (ya_ref[st["rows"], :] * _silu(st.pop("ga"))).astype(BF16)
        st["merged"] = jax.nn.sigmoid(st.pop("ma")) * _dot(y_a, wpa_ref[...])
        st["mb"] = proj(st, _MB)

    def gating_path(st):
        y_b = (st.pop("u") * st.pop("mixed") * _silu(st.pop("gb"))).astype(BF16)
        st["merged"] = st["merged"] + jax.nn.sigmoid(st.pop("mb")) * _dot(y_b, wpb_ref[...])

    def out_proj(st):
        st["mg"] = st.pop("merged").astype(BF16)
        st["out0"] = _dot(st["mg"], wo_ref[:, 0:D_MODEL // 2])

    def out_proj2(st):
        st["out"] = jnp.concatenate([st.pop("out0"), _dot(st.pop("mg"), wo_ref[:, D_MODEL // 2:])], axis=1)

    def residual(st):
        st["xo"] = st.pop("x") + _mod_row(mod_ref, 2) * st.pop("out")
        st["ms"] = jnp.mean(st["xo"] * st["xo"], axis=-1, keepdims=True)

    def finish(st):
        o_ref[st["rows"], :] = st.pop("xo") * lax.rsqrt(st.pop("ms") + EPS) * fg_ref[...]

    stages = (load_norm, project, layernorm, project2, spatial_mix, fourier_path, gating_path, out_proj, out_proj2,
              residual, finish)
    pipelines = [(stages, dict(rows=slice(k * MAIN_SUB, (k + 1) * MAIN_SUB))) for k in range(tm // MAIN_SUB)]
    if rider is not None:
        n1, tiles_per_batch = rider

        def rider_project(st):
            step = pl.program_id(0) * pl.num_programs(1) + pl.program_id(1)
            st["a"] = _fnet_in(rx_ref, rmod_ref, g_ref, win_ref[:, _A[0]:_A[1]], batch=step // tiles_per_batch)

        def rider_regroup(st):
            for q, block in enumerate(_regrouped_blocks(st.pop("a"), perm_ref, n1)):
                ra_ref[q] = block

        pipelines.insert(len(pipelines) // 2, ((rider_project, rider_regroup), {}))
    _run_skewed(pipelines)


def _main_call(x, mod, ya, wts, tm, rider=None):
    B, S, _ = x.shape
    steps = S // tm
    const2 = lambda b, i: (0, 0)
    in_specs = [
        pl.BlockSpec((None, tm, D_MODEL), lambda b, i: (b, i, 0)),
        pl.BlockSpec(mod.shape, const2),
        pl.BlockSpec((None, tm, FNET_WIDTH), lambda b, i: (b, i, 0)),
        pl.BlockSpec((1, D_MODEL), const2),
        pl.BlockSpec((D_MODEL, IN_WIDTH), const2),
        pl.BlockSpec((1, SGU_WIDTH), const2),
        pl.BlockSpec((1, SGU_WIDTH), const2),
        pl.BlockSpec((SGU_HEADS, CHUNK, CHUNK), lambda b, i: (0, 0, 0)),
        pl.BlockSpec((CHUNK, SGU_WIDTH), const2),
        pl.BlockSpec((FNET_WIDTH, D_MODEL), const2),
        pl.BlockSpec((SGU_WIDTH, D_MODEL), const2),
        pl.BlockSpec((D_MODEL, D_MODEL), const2),
        pl.BlockSpec((1, D_MODEL), const2),
    ]
    args = [x, mod, ya, wts["norm_g"], wts["w_in"], wts["ln_g"], wts["ln_b"], wts["w_s"], wts["bs_full"],
            wts["w_pa"], wts["w_pb"], wts["w_out"], wts["final_g"]]
    out_specs = pl.BlockSpec((None, tm, D_MODEL), lambda b, i: (b, i, 0))
    out_shape = jax.ShapeDtypeStruct((B, S, D_MODEL), F32)
    rider_arg = None
    if rider is not None:
        rx, rmod, n1 = rider
        rb, rs, _ = rx.shape
        rtile = rb * rs // (B * steps)
        assert rtile * B * steps == rb * rs and rtile % PRE_SUB == 0 and rs % rtile == 0
        per_batch = rs // rtile
        where = lambda b, i: divmod(b * steps + i, per_batch)
        in_specs += [
            pl.BlockSpec((None, rtile, D_MODEL), lambda b, i: (*where(b, i), 0)),
            pl.BlockSpec(rmod.shape, const2),
            pl.BlockSpec((SEQ_TILE, SEQ_TILE), const2),
        ]
        args += [rx, rmod, jnp.asarray(_regroup_matrix(n1), F32).astype(BF16)]
        out_specs = (out_specs, pl.BlockSpec((None, n1, rtile // n1, FNET_WIDTH),
                                             lambda b, i: (where(b, i)[0], 0, where(b, i)[1], 0)))
        out_shape = (out_shape, jax.ShapeDtypeStruct((rb, n1, DFT_LEN, FNET_WIDTH), BF16))
        rider_arg = (n1, per_batch)
    return pl.pallas_call(
        functools.partial(_main_kernel, tm=tm, rider=rider_arg),
        grid=(B, steps),
        in_specs=in_specs,
        out_specs=out_specs,
        out_shape=out_shape,
        compiler_params=pltpu.CompilerParams(
            dimension_semantics=("arbitrary", "arbitrary"),
            vmem_limit_bytes=VMEM_LIMIT_BYTES),
        name="encoder_main",
    )(*args)


def _channel_dft_table():
    c = np.arange(FNET_GROUP_DIM)
    ang = 2.0 * np.pi * ((c[:, None] * c[None, :]) % FNET_GROUP_DIM) / FNET_GROUP_DIM
    scale = 1.0 / math.sqrt(FNET_GROUP_DIM)
    return np.concatenate([np.cos(ang), np.sin(ang)], axis=0) * scale


def _regroup_matrix(n1):
    rows = SEQ_TILE // n1
    p = np.zeros((SEQ_TILE, SEQ_TILE), np.float32)
    for q in range(n1):
        for j in range(rows):
            p[q * rows + j, j * n1 + q] = 1.0
    return p


def _shared_dft_factors():
    k2 = np.arange(DFT_LEN)
    th2 = 2.0 * np.pi * ((k2[:, None] * k2[None, :]) % DFT_LEN) / DFT_LEN
    return jnp.asarray(np.cos(th2), F32), jnp.asarray(np.sin(th2), F32)


def _block_dft_factors(n1):
    S = n1 * DFT_LEN
    th1 = 2.0 * np.pi * ((np.arange(DFT_LEN)[None, :] * np.arange(n1)[:, None]) % S) / S
    scale = 1.0 / math.sqrt(S)
    return (jnp.asarray(np.cos(th1) * scale, F32)[:, :, None], jnp.asarray(np.sin(th1) * scale, F32)[:, :, None])


def _seq_factor(x):
    S = x.shape[1]
    n1 = S // DFT_LEN
    assert n1 * DFT_LEN == S and n1 & (n1 - 1) == 0 and SEQ_TILE % n1 == 0
    assert S % PRE_TILE == 0 and S % MAIN_TILE == 0
    return n1


def _fourier_input(x, mod, wts):
    n1 = _seq_factor(x)
    perm = jnp.asarray(_regroup_matrix(n1), F32).astype(BF16)
    m_tab = wts["m_tab"][n1]
    if x.shape[1] == PRE_TILE:
        return _front_call(x, mod, wts["norm_g"], wts["w_in"], perm, m_tab, wts["e_tab"], wts["b_fmix"], n1)
    a_perm = _pre_call(x, mod, wts["norm_g"], wts["w_in"], perm, n1)
    return _fourier_call(a_perm, m_tab, wts["e_tab"], wts["b_fmix"])


def kernel(x_prompt, x_sample, c_prompt, c_sample, norm_g, w_ada, b_ada, w_in, w_fmix, b_fmix,
           sgu_ln_g, sgu_ln_b, w_s, b_s, w_pa, w_pb, w_out, final_g):
    depth = norm_g.shape[0]
    assert depth == 1
    l = 0
    (mod_p, mod_s), (e_tab, w_s_bf, w_in_bf, w_pa_bf, w_pb_bf, w_out_bf), m_tabs = _prepare(
        (c_prompt, c_sample), w_ada[l], b_ada[l],
        jnp.asarray(_channel_dft_table(), F32), w_fmix[l], w_s[l], w_in[l], w_pa[l], w_pb[l], w_out[l],
        seq_factors={x.shape[1] // DFT_LEN for x in (x_prompt, x_sample)})
    wts = dict(
        norm_g=norm_g[l].reshape(1, -1),
        w_in=w_in_bf,
        e_tab=e_tab,
        m_tab=m_tabs,
        b_fmix=b_fmix[l].reshape(1, -1),
        ln_g=sgu_ln_g[l].reshape(1, -1),
        ln_b=sgu_ln_b[l].reshape(1, -1),
        w_s=w_s_bf,
        bs_full=jnp.repeat(b_s[l].T, SGU_HEAD_DIM, axis=1),
        w_pa=w_pa_bf,
        w_pb=w_pb_bf,
        w_out=w_out_bf,
        final_g=final_g.reshape(1, -1),
    )
    ya_p = _fourier_input(x_prompt, mod_p, wts)
    n1_s = _seq_factor(x_sample)
    steps_p = x_prompt.shape[0] * (x_prompt.shape[1] // MAIN_TILE)
    tokens_s = x_sample.shape[0] * x_sample.shape[1]
    rtile = tokens_s // steps_p
    if rtile * steps_p == tokens_s and rtile % PRE_SUB == 0 and x_sample.shape[1] % rtile == 0:
        y_prompt, a_s = _main_call(x_prompt, mod_p, ya_p, wts, MAIN_TILE, rider=(x_sample, mod_s, n1_s))
        ya_s = _fourier_call(a_s, wts["m_tab"][n1_s], wts["e_tab"], wts["b_fmix"])
    else:
        y_prompt = _main_call(x_prompt, mod_p, ya_p, wts, MAIN_TILE)
        ya_s = _fourier_input(x_sample, mod_s, wts)
    y_sample = _main_call(x_sample, mod_s, ya_s, wts, MAIN_TILE)
    return (y_prompt, y_sample)
```

```python
import functools
import math

import numpy as np
import jax
import jax.numpy as jnp
from jax import lax
from jax.experimental import pallas as pl
from jax.experimental.pallas import tpu as pltpu

D_MODEL = 1024
FNET_WIDTH = 512
FNET_GROUPS = 4
FNET_GROUP_DIM = 128
SGU_WIDTH = 512
SGU_HEADS = 4
SGU_HEAD_DIM = 128
CHUNK = 128
EPS = 1e-6
PREP_STEPS = 4
DFT_LEN = 256
SEQ_TILE = 256
PRE_TILE = 2048
PRE_SUB = 512
MAIN_TILE = 1024
MAIN_SUB = 256
FFT_WIDTH = 256

_IN_SIZES = (FNET_WIDTH, FNET_WIDTH, SGU_WIDTH, SGU_WIDTH, SGU_WIDTH, D_MODEL, D_MODEL)
_IN_ENDS = tuple(int(v) for v in np.cumsum(_IN_SIZES))
_A, _GA, _U, _V, _GB, _MA, _MB = ((end - size, end) for size, end in zip(_IN_SIZES, _IN_ENDS))
IN_WIDTH = _IN_ENDS[-1]

BF16 = jnp.bfloat16
F32 = jnp.float32

VMEM_LIMIT_BYTES = 56 * 1024 * 1024


def _dot(a, b):
    return jnp.dot(a, b, preferred_element_type=F32)


def _silu(x):
    return x * jax.nn.sigmoid(x)


def _split_bf16(x):
    hi = x.astype(BF16)
    return hi, (x - hi.astype(F32)).astype(BF16)


def _dot_3pass(a, b_split):
    a_hi, a_lo = _split_bf16(a)
    b_hi, b_lo = b_split
    return _dot(a_hi, b_hi) + (_dot(a_lo, b_hi) + _dot(a_hi, b_lo))


def _prep_kernel(*refs, n_cond, table_blocks):
    refs = list(refs)
    take = lambda k: [refs.pop(0) for _ in range(k)]
    c_refs = take(n_cond)
    wada_ref, bada_ref, cs_ref, wf_ref, ws_ref, win_ref, wpa_ref, wpb_ref, wo_ref, c2_ref, s2_ref = take(11)
    factor_refs = take(2 * len(table_blocks))
    mod_refs = take(n_cond)
    e_ref, ws_o, win_o, wpa_o, wpb_o, wo_o = take(6)
    table_refs = refs

    cond = jnp.concatenate([c_ref[...] for c_ref in c_refs], axis=0)
    mod = _dot_3pass(_silu(cond), _split_bf16(wada_ref[...])) + bada_ref[...]
    row = 0
    for mod_ref in mod_refs:
        mod_ref[...] = mod[row:row + mod_ref.shape[0]]
        row += mod_ref.shape[0]
    win_o[...] = win_ref[...].astype(BF16)
    wpa_o[...] = wpa_ref[...].astype(BF16)
    wpb_o[...] = wpb_ref[...].astype(BF16)
    wo_o[...] = wo_ref[...].astype(BF16)

    c2, s2 = c2_ref[...], s2_ref[...]
    for t, table_ref in enumerate(table_refs):
        c1_ref, s1_ref = factor_refs[2 * t], factor_refs[2 * t + 1]
        for q in range(table_blocks[t]):
            c1, s1 = c1_ref[q], s1_ref[q]
            table_ref[q, 0:DFT_LEN, :] = (c1 * c2 - s1 * s2).astype(BF16)
            table_ref[q, DFT_LEN:, :] = (-(s1 * c2 + c1 * s2)).astype(BF16)

    @pl.when(pl.program_id(0) == 0)
    def _():
        ws_o[...] = ws_ref[...].astype(BF16)
        for g in range(FNET_GROUPS):
            e_ref[g] = _dot_3pass(cs_ref[...], _split_bf16(wf_ref[g])).astype(BF16)


def _prepare(conds, w_ada, b_ada, cs_c, w_fmix, w_s, w_in, w_pa, w_pb, w_out, seq_factors):
    e_shape = (FNET_GROUPS, 2 * FNET_GROUP_DIM, FNET_GROUP_DIM)
    whole = lambda shape: pl.BlockSpec(shape, lambda j: (0,) * len(shape))
    cols = lambda a: pl.BlockSpec((a.shape[0], a.shape[1] // PREP_STEPS), lambda j: (0, j))
    lead = lambda shape: pl.BlockSpec((shape[0] // PREP_STEPS,) + shape[1:], lambda j: (j, 0, 0))
    streamed = (w_in, w_pa, w_pb, w_out)
    assert all(a.shape[1] % (PREP_STEPS * FNET_GROUP_DIM) == 0 for a in (w_ada,) + streamed)
    n1s = sorted(seq_factors)
    assert all(n1 % PREP_STEPS == 0 for n1 in n1s)
    c2, s2 = _shared_dft_factors()
    factors = [f for n1 in n1s for f in _block_dft_factors(n1)]
    t_shapes = [(n1, 2 * DFT_LEN, DFT_LEN) for n1 in n1s]
    mod_cols = w_ada.shape[1] // PREP_STEPS
    outs = pl.pallas_call(
        functools.partial(_prep_kernel, n_cond=len(conds), table_blocks=tuple(n1 // PREP_STEPS for n1 in n1s)),
        grid=(PREP_STEPS,),
        in_specs=[whole(c.shape) for c in conds]
                 + [cols(w_ada), pl.BlockSpec((1, mod_cols), lambda j: (0, j)),
                    whole(cs_c.shape), whole(w_fmix.shape), whole(w_s.shape)] + [cols(a) for a in streamed]
                 + [whole(c2.shape), whole(s2.shape)] + [lead(f.shape) for f in factors],
        out_specs=[pl.BlockSpec((c.shape[0], mod_cols), lambda j: (0, j)) for c in conds]
                  + [whole(e_shape), whole(w_s.shape)] + [cols(a) for a in streamed] + [lead(t) for t in t_shapes],
        out_shape=[jax.ShapeDtypeStruct((c.shape[0], w_ada.shape[1]), F32) for c in conds]
                  + [jax.ShapeDtypeStruct(e_shape, BF16), jax.ShapeDtypeStruct(w_s.shape, BF16)]
                  + [jax.ShapeDtypeStruct(a.shape, BF16) for a in streamed]
                  + [jax.ShapeDtypeStruct(t, BF16) for t in t_shapes],
        compiler_params=pltpu.CompilerParams(
            dimension_semantics=("arbitrary",),
            vmem_limit_bytes=VMEM_LIMIT_BYTES),
        name="adaln_prep",
    )(*conds, w_ada, b_ada.reshape(1, -1), cs_c, w_fmix, w_s, *streamed, c2, s2, *factors)
    k = len(conds)
    return outs[:k], outs[k:k + 6], dict(zip(n1s, outs[k + 6:]))


def _mod_row(mod_ref, part):
    return mod_ref[pl.ds(pl.program_id(0), 1), part * D_MODEL:(part + 1) * D_MODEL]


def _modulated_norm(x, mod_ref, g_ref):
    ms = jnp.mean(x * x, axis=-1, keepdims=True)
    xn = x * lax.rsqrt(ms + EPS) * g_ref[...]
    return xn * (1.0 + _mod_row(mod_ref, 1)) + _mod_row(mod_ref, 0)


def _regrouped_blocks(a, perm_ref, n1):
    rows = SEQ_TILE // n1
    aps = [_dot(perm_ref[...], a[s * SEQ_TILE:(s + 1) * SEQ_TILE]).astype(BF16)
           for s in range(a.shape[0] // SEQ_TILE)]
    return [jnp.concatenate([ap[q * rows:(q + 1) * rows] for ap in aps], axis=0) for q in range(n1)]


def _fnet_in(x_ref, mod_ref, g_ref, wa_ref):
    parts = []
    for r0 in range(0, x_ref.shape[0], PRE_SUB):
        h = _modulated_norm(x_ref[r0:r0 + PRE_SUB, :], mod_ref, g_ref).astype(BF16)
        parts.append(_dot(h, wa_ref[...]).astype(BF16))
    return jnp.concatenate(parts, axis=0)


X_RING = 3


def _pre_kernel(x_hbm, mod_ref, g_ref, win_ref, perm_ref, a_ref, xbuf, sem, *, n1, tiles):
    step = pl.program_id(0) * tiles + pl.program_id(1)
    n_steps = pl.num_programs(0) * tiles

    def tile_copy(k):
        slot = lax.rem(k, X_RING)
        src = x_hbm.at[k // tiles, pl.ds(pl.multiple_of(lax.rem(k, tiles) * PRE_TILE, PRE_TILE), PRE_TILE), :]
        return pltpu.make_async_copy(src, xbuf.at[slot], sem.at[slot])

    @pl.when(step == 0)
    def _():
        for k in range(X_RING - 1):
            tile_copy(k).start()

    @pl.when(step + X_RING - 1 < n_steps)
    def _():
        tile_copy(step + X_RING - 1).start()

    tile_copy(step).wait()
    x_ref = xbuf.at[lax.rem(step, X_RING)]
    for q, block in enumerate(_regrouped_blocks(_fnet_in(x_ref, mod_ref, g_ref, win_ref), perm_ref, n1)):
        a_ref[q] = block


def _pre_call(x, mod, norm_g, w_in, perm, n1):
    B, S, _ = x.shape
    rows = PRE_TILE // n1
    tiles = S // PRE_TILE
    assert B * tiles >= X_RING - 1
    return pl.pallas_call(
        functools.partial(_pre_kernel, n1=n1, tiles=tiles),
        grid=(B, tiles),
        in_specs=[
            pl.BlockSpec(memory_space=pl.ANY),
            pl.BlockSpec(mod.shape, lambda b, i: (0, 0)),
            pl.BlockSpec((1, D_MODEL), lambda b, i: (0, 0)),
            pl.BlockSpec((D_MODEL, FNET_WIDTH), lambda b, i: (0, _A[0] // FNET_WIDTH)),
            pl.BlockSpec((SEQ_TILE, SEQ_TILE), lambda b, i: (0, 0)),
        ],
        out_specs=pl.BlockSpec((None, n1, rows, FNET_WIDTH), lambda b, i: (b, 0, i, 0)),
        out_shape=jax.ShapeDtypeStruct((B, n1, DFT_LEN, FNET_WIDTH), BF16),
        scratch_shapes=[pltpu.VMEM((X_RING, PRE_TILE, D_MODEL), F32), pltpu.SemaphoreType.DMA((X_RING,))],
        compiler_params=pltpu.CompilerParams(
            dimension_semantics=("arbitrary", "arbitrary"),
            vmem_limit_bytes=VMEM_LIMIT_BYTES),
        name="fnet_pre",
    )(x, mod, norm_g, w_in, perm)


def _cmul_const(z, w):
    zr, zi = z
    wr, wi = w.real, w.imag
    tol = 1e-12
    if abs(wi) < tol:
        return (zr, zi) if abs(wr - 1) < tol else (zr * wr, zi * wr)
    if abs(wr) < tol:
        if abs(wi + 1) < tol:
            return (zi, -zr)
        if abs(wi - 1) < tol:
            return (-zi, zr)
    if abs(abs(wr) - abs(wi)) < tol:
        c = abs(wr)
        sr, si = math.copysign(1.0, wr), math.copysign(1.0, wi)
        re = (zr if sr > 0 else -zr) - (zi if si > 0 else -zi)
        im = (zi if sr > 0 else -zi) + (zr if si > 0 else -zr)
        return (re * c, im * c)
    return (zr * wr - zi * wi, zr * wi + zi * wr)


def _block_fft(zs):
    n = len(zs)
    if n == 1:
        return zs
    even = _block_fft(zs[0::2])
    odd = _block_fft(zs[1::2])
    out = [None] * n
    for k in range(n // 2):
        w = complex(math.cos(2 * math.pi * k / n), -math.sin(2 * math.pi * k / n))
        tr, ti = _cmul_const(odd[k], w)
        er, ei = even[k]
        out[k] = (er + tr, ei + ti)
        out[k + n // 2] = (er - tr, ei - ti)
    return out


def _run_skewed(stages, states):
    for t in range(len(stages) + len(states) - 1):
        for i, st in enumerate(states):
            if 0 <= t - i < len(stages):
                stages[t - i](st)


def _fourier_body(a_block, m_ref, e_ref, bf_ref, o_ref, n1):
    def block_dfts(st):
        st["g"] = []
        for q in range(n1):
            g = _dot(m_ref[q], a_block(q, st["cols"]))
            st["g"].append((g[:DFT_LEN], g[DFT_LEN:]))

    def across_blocks(st):
        st["u"] = _block_fft(st.pop("g"))

    def channel_map(st):
        us = st.pop("u")
        for g in st["groups"]:
            lo, hi = g * FNET_GROUP_DIM - st["cols"].start, (g + 1) * FNET_GROUP_DIM - st["cols"].start
            z = jnp.concatenate(
                [jnp.concatenate([ur[:, lo:hi], ui[:, lo:hi]], axis=1).astype(BF16) for ur, ui in us], axis=0)
            out = slice(g * FNET_GROUP_DIM, (g + 1) * FNET_GROUP_DIM)
            o_ref[:, out] = _dot(z, e_ref[g]) + bf_ref[:, out]

    halves = [dict(cols=slice(j * FFT_WIDTH, (j + 1) * FFT_WIDTH),
                   groups=range(j * FFT_WIDTH // FNET_GROUP_DIM, (j + 1) * FFT_WIDTH // FNET_GROUP_DIM))
              for j in range(FNET_WIDTH // FFT_WIDTH)]
    _run_skewed((block_dfts, across_blocks, channel_map), halves)


def _fourier_kernel(a_ref, m_ref, e_ref, bf_ref, o_ref, *, n1):
    _fourier_body(lambda q, cols: a_ref[q, :, cols], m_ref, e_ref, bf_ref, o_ref, n1)


def _front_kernel(x_ref, mod_ref, g_ref, win_ref, perm_ref, m_ref, e_ref, bf_ref, o_ref, *, n1):
    blocks = _regrouped_blocks(_fnet_in(x_ref, mod_ref, g_ref, win_ref), perm_ref, n1)
    _fourier_body(lambda q, cols: blocks[q][:, cols], m_ref, e_ref, bf_ref, o_ref, n1)


def _front_call(x, mod, norm_g, w_in, perm, m_tab, e_tab, b_fmix, n1):
    B, S, _ = x.shape
    const2 = lambda b: (0, 0)
    const3 = lambda b: (0, 0, 0)
    return pl.pallas_call(
        functools.partial(_front_kernel, n1=n1),
        grid=(B,),
        in_specs=[
            pl.BlockSpec((None, S, D_MODEL), lambda b: (b, 0, 0)),
            pl.BlockSpec(mod.shape, lambda b: (0, 0)),
            pl.BlockSpec((1, D_MODEL), const2),
            pl.BlockSpec((D_MODEL, FNET_WIDTH), lambda b: (0, _A[0] // FNET_WIDTH)),
            pl.BlockSpec((SEQ_TILE, SEQ_TILE), const2),
            pl.BlockSpec((n1, 2 * DFT_LEN, DFT_LEN), const3),
            pl.BlockSpec((FNET_GROUPS, 2 * FNET_GROUP_DIM, FNET_GROUP_DIM), const3),
            pl.BlockSpec((1, FNET_WIDTH), const2),
        ],
        out_specs=pl.BlockSpec((None, S, FNET_WIDTH), lambda b: (b, 0, 0)),
        out_shape=jax.ShapeDtypeStruct((B, S, FNET_WIDTH), F32),
        compiler_params=pltpu.CompilerParams(
            dimension_semantics=("arbitrary",),
            vmem_limit_bytes=VMEM_LIMIT_BYTES),
        name="fnet_front",
    )(x, mod, norm_g, w_in, perm, m_tab, e_tab, b_fmix)


def _fourier_call(a_perm, m_tab, e_tab, b_fmix):
    B, n1, _, _ = a_perm.shape
    S = n1 * DFT_LEN
    return pl.pallas_call(
        functools.partial(_fourier_kernel, n1=n1),
        grid=(B,),
        in_specs=[
            pl.BlockSpec((None, n1, DFT_LEN, FNET_WIDTH), lambda b: (b, 0, 0, 0)),
            pl.BlockSpec((n1, 2 * DFT_LEN, DFT_LEN), lambda b: (0, 0, 0)),
            pl.BlockSpec((FNET_GROUPS, 2 * FNET_GROUP_DIM, FNET_GROUP_DIM), lambda b: (0, 0, 0)),
            pl.BlockSpec((1, FNET_WIDTH), lambda b: (0, 0)),
        ],
        out_specs=pl.BlockSpec((None, S, FNET_WIDTH), lambda b: (b, 0, 0)),
        out_shape=jax.ShapeDtypeStruct((B, S, FNET_WIDTH), F32),
        compiler_params=pltpu.CompilerParams(
            dimension_semantics=("arbitrary",),
            vmem_limit_bytes=VMEM_LIMIT_BYTES),
        name="fnet_fft",
    )(a_perm, m_tab, e_tab, b_fmix)


def _main_kernel(x_ref, mod_ref, ya_ref, g_ref, win_ref, lng_ref, lnb_ref, ws_ref, bs_ref,
                 wpa_ref, wpb_ref, wo_ref, fg_ref, o_ref, *, tm):
    def proj(st, rng):
        return _dot(st["h"], win_ref[:, rng[0]:rng[1]])

    def load_norm(st):
        st["x"] = x_ref[st["rows"], :]
        st["h"] = _modulated_norm(st["x"], mod_ref, g_ref).astype(BF16)

    def project(st):
        st["v"] = proj(st, _V)
        st["ga"] = proj(st, _GA)
        st["ma"] = proj(st, _MA)

    def project2(st):
        st["u"] = proj(st, _U)
        st["gb"] = proj(st, _GB)

    def layernorm(st):
        v = st.pop("v")
        mu = jnp.mean(v, axis=-1, keepdims=True)
        vc = v - mu
        var = jnp.mean(vc * vc, axis=-1, keepdims=True)
        st["vln"] = (vc * lax.rsqrt(var + EPS) * lng_ref[...] + lnb_ref[...]).astype(BF16)

    def spatial_mix(st):
        vln = st.pop("vln")
        rows = []
        for c in range(MAIN_SUB // CHUNK):
            r0 = c * CHUNK
            cols = []
            for hd in range(SGU_HEADS):
                lo, hi = hd * SGU_HEAD_DIM, (hd + 1) * SGU_HEAD_DIM
                cols.append(_dot(ws_ref[hd], vln[r0:r0 + CHUNK, lo:hi]))
            rows.append(jnp.concatenate(cols, axis=1) + bs_ref[...])
        st["mixed"] = jnp.concatenate(rows, axis=0)

    def fourier_path(st):
        y_a = (ya_ref[st["rows"], :] * _silu(st.pop("ga"))).astype(BF16)
        st["merged"] = jax.nn.sigmoid(st.pop("ma")) * _dot(y_a, wpa_ref[...])
        st["mb"] = proj(st, _MB)

    def gating_path(st):
        y_b = (st.pop("u") * st.pop("mixed") * _silu(st.pop("gb"))).astype(BF16)
        st["merged"] = st["merged"] + jax.nn.sigmoid(st.pop("mb")) * _dot(y_b, wpb_ref[...])

    def out_proj(st):
        st["mg"] = st.pop("merged").astype(BF16)
        st["out0"] = _dot(st["mg"], wo_ref[:, 0:D_MODEL // 2])

    def out_proj2(st):
        st["out"] = jnp.concatenate([st.pop("out0"), _dot(st.pop("mg"), wo_ref[:, D_MODEL // 2:])], axis=1)

    def residual(st):
        st["xo"] = st.pop("x") + _mod_row(mod_ref, 2) * st.pop("out")
        st["ms"] = jnp.mean(st["xo"] * st["xo"], axis=-1, keepdims=True)

    def finish(st):
        o_ref[st["rows"], :] = st.pop("xo") * lax.rsqrt(st.pop("ms") + EPS) * fg_ref[...]

    stages = (load_norm, project, layernorm, project2, spatial_mix, fourier_path, gating_path, out_proj, out_proj2,
              residual, finish)
    subs = [dict(rows=slice(i * MAIN_SUB, (i + 1) * MAIN_SUB)) for i in range(tm // MAIN_SUB)]
    _run_skewed(stages, subs)


def _main_call(x, mod, ya, norm_g, w_in, ln_g, ln_b, w_s, bs_full, w_pa, w_pb, w_out,
               final_g, tm):
    B, S, _ = x.shape
    const2 = lambda b, i: (0, 0)
    return pl.pallas_call(
        functools.partial(_main_kernel, tm=tm),
        grid=(B, S // tm),
        in_specs=[
            pl.BlockSpec((None, tm, D_MODEL), lambda b, i: (b, i, 0)),
            pl.BlockSpec(mod.shape, lambda b, i: (0, 0)),
            pl.BlockSpec((None, tm, FNET_WIDTH), lambda b, i: (b, i, 0)),
            pl.BlockSpec((1, D_MODEL), const2),
            pl.BlockSpec((D_MODEL, IN_WIDTH), const2),
            pl.BlockSpec((1, SGU_WIDTH), const2),
            pl.BlockSpec((1, SGU_WIDTH), const2),
            pl.BlockSpec((SGU_HEADS, CHUNK, CHUNK), lambda b, i: (0, 0, 0)),
            pl.BlockSpec((CHUNK, SGU_WIDTH), const2),
            pl.BlockSpec((FNET_WIDTH, D_MODEL), const2),
            pl.BlockSpec((SGU_WIDTH, D_MODEL), const2),
            pl.BlockSpec((D_MODEL, D_MODEL), const2),
            pl.BlockSpec((1, D_MODEL), const2),
        ],
        out_specs=pl.BlockSpec((None, tm, D_MODEL), lambda b, i: (b, i, 0)),
        out_shape=jax.ShapeDtypeStruct((B, S, D_MODEL), F32),
        compiler_params=pltpu.CompilerParams(
            dimension_semantics=("arbitrary", "arbitrary"),
            vmem_limit_bytes=VMEM_LIMIT_BYTES),
        name="encoder_main",
    )(x, mod, ya, norm_g, w_in, ln_g, ln_b, w_s, bs_full, w_pa, w_pb, w_out, final_g)


def _channel_dft_table():
    c = np.arange(FNET_GROUP_DIM)
    ang = 2.0 * np.pi * ((c[:, None] * c[None, :]) % FNET_GROUP_DIM) / FNET_GROUP_DIM
    scale = 1.0 / math.sqrt(FNET_GROUP_DIM)
    return np.concatenate([np.cos(ang), np.sin(ang)], axis=0) * scale


def _regroup_matrix(n1):
    rows = SEQ_TILE // n1
    p = np.zeros((SEQ_TILE, SEQ_TILE), np.float32)
    for q in range(n1):
        for j in range(rows):
            p[q * rows + j, j * n1 + q] = 1.0
    return p


def _shared_dft_factors():
    k2 = np.arange(DFT_LEN)
    th2 = 2.0 * np.pi * ((k2[:, None] * k2[None, :]) % DFT_LEN) / DFT_LEN
    return jnp.asarray(np.cos(th2), F32), jnp.asarray(np.sin(th2), F32)


def _block_dft_factors(n1):
    S = n1 * DFT_LEN
    th1 = 2.0 * np.pi * ((np.arange(DFT_LEN)[None, :] * np.arange(n1)[:, None]) % S) / S
    scale = 1.0 / math.sqrt(S)
    return (jnp.asarray(np.cos(th1) * scale, F32)[:, :, None], jnp.asarray(np.sin(th1) * scale, F32)[:, :, None])


def _trunk(x, mod, wts):
    B, S, _ = x.shape
    assert S % DFT_LEN == 0
    n1 = S // DFT_LEN
    assert n1 & (n1 - 1) == 0 and SEQ_TILE % n1 == 0 and S % PRE_TILE == 0 and S % MAIN_TILE == 0
    perm = jnp.asarray(_regroup_matrix(n1), F32).astype(BF16)
    m_tab = wts["m_tab"][n1]
    if S == PRE_TILE:
        ya = _front_call(x, mod, wts["norm_g"], wts["w_in"], perm, m_tab, wts["e_tab"], wts["b_fmix"], n1)
    else:
        a_perm = _pre_call(x, mod, wts["norm_g"], wts["w_in"], perm, n1)
        ya = _fourier_call(a_perm, m_tab, wts["e_tab"], wts["b_fmix"])
    return _main_call(x, mod, ya, wts["norm_g"], wts["w_in"], wts["ln_g"], wts["ln_b"],
                      wts["w_s"], wts["bs_full"], wts["w_pa"], wts["w_pb"], wts["w_out"],
                      wts["final_g"], MAIN_TILE)


def kernel(x_prompt, x_sample, c_prompt, c_sample, norm_g, w_ada, b_ada, w_in, w_fmix, b_fmix,
           sgu_ln_g, sgu_ln_b, w_s, b_s, w_pa, w_pb, w_out, final_g):
    depth = norm_g.shape[0]
    assert depth == 1
    l = 0
    (mod_p, mod_s), (e_tab, w_s_bf, w_in_bf, w_pa_bf, w_pb_bf, w_out_bf), m_tabs = _prepare(
        (c_prompt, c_sample), w_ada[l], b_ada[l],
        jnp.asarray(_channel_dft_table(), F32), w_fmix[l], w_s[l], w_in[l], w_pa[l], w_pb[l], w_out[l],
        seq_factors={x.shape[1] // DFT_LEN for x in (x_prompt, x_sample)})
    wts = dict(
        norm_g=norm_g[l].reshape(1, -1),
        w_in=w_in_bf,
        e_tab=e_tab,
        m_tab=m_tabs,
        b_fmix=b_fmix[l].reshape(1, -1),
        ln_g=sgu_ln_g[l].reshape(1, -1),
        ln_b=sgu_ln_b[l].reshape(1, -1),
        w_s=w_s_bf,
        bs_full=jnp.repeat(b_s[l].T, SGU_HEAD_DIM, axis=1),
        w_pa=w_pa_bf,
        w_pb=w_pb_bf,
        w_out=w_out_bf,
        final_g=final_g.reshape(1, -1),
    )
    y_prompt = _trunk(x_prompt, mod_p, wts)
    y_sample = _trunk(x_sample, mod_s, wts)
    return (y_prompt, y_sample)
```
